```python
import math
import jax, jax.numpy as jnp
from jax import lax
import numpy as np

D_MODEL = 1024
BATCH = 2
SEQ = 8192
DEPTH = 4
DEC_BATCH = 32
DEC_SEQ = 1
PAST_LEN = 8192
PAGE_SIZE = 128

SSM_WIDTH = D_MODEL // 2
SSM_GROUP = 16
SSM_GROUPS = SSM_WIDTH // SSM_GROUP
SSM_N = 64
CONV_WIDTH = D_MODEL // 2
CONV_K = 31
HEAD_DIM = 64
HEADS_PER_GROUP = 4
ATTN_GROUPS = ((128, 1), (512, 4), (2048, 16))
ATTN_HEADS = HEADS_PER_GROUP * len(ATTN_GROUPS)
ATTN_WIDTH = ATTN_HEADS * HEAD_DIM
ATTN_OUT = HEADS_PER_GROUP * HEAD_DIM
ATTN_BLOCK = 128
ROPE_THETA = 10000.0
FFN_HIDDEN = 2816
FFN_CONV_K = 3
IN_WIDTH = SSM_WIDTH + 2 * CONV_WIDTH + 3 * ATTN_WIDTH
EPS = 1e-6
F32 = jnp.float32

kernel_name = 'hybrid_s5_conformer_dilated_attn_step'


def rmsnorm(x, g):
    xf = x.astype(F32)
    y = xf * lax.rsqrt(jnp.mean(xf * xf, axis=-1, keepdims=True) + EPS)
    return (y * g.astype(F32)).astype(x.dtype)


def layernorm(x, g, b):
    xf = x.astype(F32)
    mu = jnp.mean(xf, axis=-1, keepdims=True)
    var = jnp.mean(jnp.square(xf - mu), axis=-1, keepdims=True)
    return ((xf - mu) * lax.rsqrt(var + EPS) * g.astype(F32) + b.astype(F32)).astype(x.dtype)


def rope(x, pos):
    half = HEAD_DIM // 2
    inv = ROPE_THETA ** (-jnp.arange(half, dtype=F32) / half)
    ang = pos.astype(F32)[:, None] * inv[None, :]
    cos = jnp.cos(ang)[:, None, :]
    sin = jnp.sin(ang)[:, None, :]
    xf = x.astype(F32)
    x1, x2 = xf[..., :half], xf[..., half:]
    return jnp.concatenate([x1 * cos - x2 * sin, x1 * sin + x2 * cos], axis=-1).astype(x.dtype)


def causal_dwconv(x, buf, w, b):
    k = w.shape[0]
    ch = x.shape[-1]
    xp = jnp.concatenate([buf.astype(x.dtype), x], axis=1)
    y = lax.conv_general_dilated(xp, w.astype(x.dtype)[:, None, :], window_strides=(1,), padding='VALID',
                                 dimension_numbers=('NWC', 'WIO', 'NWC'), feature_group_count=ch)
    return y + b.astype(x.dtype), xp[:, -(k - 1):]


def ssm_branch(u, s0, a_re, a_im, log_dt, b_re, b_im, c_re, c_im, d_skip, w_glu, b_glu):
    bsz, seq, _ = u.shape
    uf = u.astype(F32).reshape(bsz, seq, SSM_GROUPS, SSM_GROUP)
    lam = lax.complex(a_re.astype(F32), a_im.astype(F32))
    dt = jnp.exp(log_dt.astype(F32))[:, None]
    lam_bar = jnp.exp(lam * dt)
    bmat = lax.complex(b_re.astype(F32), b_im.astype(F32))
    cmat = lax.complex(c_re.astype(F32), c_im.astype(F32))
    b_bar = ((lam_bar - 1.0) / lam)[..., None] * bmat
    bu = jnp.einsum('gnc,bsgc->bsgn', b_bar, uf.astype(jnp.complex64))
    init = lax.complex(s0[..., 0].astype(F32), s0[..., 1].astype(F32))
    bu = bu.at[:, 0].add(lam_bar * init)
    a = jnp.broadcast_to(lam_bar, bu.shape)

    def combine(left, right):
        a_l, b_l = left
        a_r, b_r = right
        return a_l * a_r, a_r * b_l + b_r

    _, states = lax.associative_scan(combine, (a, bu), axis=1)
    y = jnp.einsum('gcn,bsgn->bsgc', cmat, states).real + d_skip.astype(F32).reshape(SSM_GROUPS, SSM_GROUP) * uf
    y = jax.nn.gelu(y.reshape(bsz, seq, SSM_WIDTH))
    y = y * jax.nn.sigmoid(y @ w_glu.astype(F32) + b_glu.astype(F32))
    last = states[:, -1]
    new_state = jnp.stack([last.real, last.imag], axis=-1).astype(s0.dtype)
    return y.astype(u.dtype), new_state


def conv_branch(u, buf, w, b, ln_g, ln_b):
    a = u[..., :CONV_WIDTH] * jax.nn.sigmoid(u[..., CONV_WIDTH:])
    y, new_buf = causal_dwconv(a, buf, w, b)
    return jax.nn.silu(layernorm(y, ln_g, ln_b)), new_buf


def dilated_band_attention(q, k, v, window, dil):
    bsz, seq, nh, hd = q.shape
    n_back = window // dil
    span = dil * ATTN_BLOCK
    seq_pad = -(-seq // span) * span
    m_len = seq_pad // dil
    nb = m_len // ATTN_BLOCK

    def strided(t):
        t = jnp.pad(t.astype(F32), ((0, 0), (0, seq_pad - seq), (0, 0), (0, 0)))
        t = t.reshape(bsz, m_len, dil, nh, hd).transpose(0, 2, 1, 3, 4)
        return t.reshape(bsz * dil, nb, ATTN_BLOCK, nh, hd)

    def with_prev(t):
        prev = jnp.pad(t[:, :-1], ((0, 0), (1, 0), (0, 0), (0, 0), (0, 0)))
        return jnp.concatenate([prev, t], axis=2)

    qb = strided(q)
    kb = with_prev(strided(k))
    vb = with_prev(strided(v))
    s = jnp.einsum('znqhd,znkhd->znhqk', qb, kb) * (hd ** -0.5)
    q_idx = jnp.arange(ATTN_BLOCK)[:, None] + ATTN_BLOCK
    k_idx = jnp.arange(2 * ATTN_BLOCK)[None, :]
    dist = q_idx - k_idx
    key_m = jnp.arange(nb)[:, None, None] * ATTN_BLOCK + k_idx[None] - ATTN_BLOCK
    valid = ((dist >= 0) & (dist <= n_back))[None] & (key_m >= 0)
    s = jnp.where(valid[None, :, None], s, -jnp.inf)
    m = jnp.max(s, axis=-1, keepdims=True)
    p = jnp.exp(s - m)
    l = jnp.sum(p, axis=-1, keepdims=True)
    o = jnp.einsum('znhqk,znkhd->znqhd', p / l, vb)
    lse = (m + jnp.log(l))[..., 0].transpose(0, 1, 3, 2)

    def unstrided(t):
        rest = t.shape[3:]
        t = t.reshape(bsz, dil, m_len, *rest)
        return jnp.swapaxes(t, 1, 2).reshape(bsz, seq_pad, *rest)[:, :seq]

    return unstrided(o), unstrided(lse)


def dilated_step_attention(q, k, v, buf_k, buf_v, window, dil):
    buf_len = buf_k.shape[1]
    t_new, hd = q.shape[1], q.shape[-1]
    kc = jnp.concatenate([buf_k.astype(k.dtype), k], axis=1)
    vc = jnp.concatenate([buf_v.astype(v.dtype), v], axis=1)
    n_keys = window // dil + 1
    idx = buf_len + jnp.arange(t_new)[:, None] - dil * jnp.arange(n_keys)[None, :]
    valid = idx >= 0
    idx = jnp.maximum(idx, 0)
    kg = kc[:, idx].astype(F32)
    vg = vc[:, idx].astype(F32)
    s = jnp.einsum('bthd,btkhd->bthk', q.astype(F32), kg) * (hd ** -0.5)
    s = jnp.where(valid[None, :, None, :], s, -jnp.inf)
    m = jnp.max(s, axis=-1, keepdims=True)
    p = jnp.exp(s - m)
    l = jnp.sum(p, axis=-1, keepdims=True)
    o = jnp.einsum('bthk,btkhd->bthd', p / l, vg)
    lse = (m + jnp.log(l))[..., 0]
    new_buf = jnp.stack([kc[:, -buf_len:], vc[:, -buf_len:]], axis=2).astype(buf_k.dtype)
    return o, lse, new_buf


def attn_branch(zq, zk, zv, pos, kv_bufs, gq, gk):
    bsz, seq, _ = zq.shape
    q = rope(rmsnorm(zq.reshape(bsz, seq, ATTN_HEADS, HEAD_DIM), gq), pos)
    k = rope(rmsnorm(zk.reshape(bsz, seq, ATTN_HEADS, HEAD_DIM), gk), pos)
    v = zv.reshape(bsz, seq, ATTN_HEADS, HEAD_DIM)
    outs, lses, new_bufs = [], [], []
    for gi, (window, dil) in enumerate(ATTN_GROUPS):
        hs = slice(gi * HEADS_PER_GROUP, (gi + 1) * HEADS_PER_GROUP)
        qg, kg, vg = q[:, :, hs], k[:, :, hs], v[:, :, hs]
        if kv_bufs is None:
            o, lse = dilated_band_attention(qg, kg, vg, window, dil)
            keep = min(window, seq)
            new_bufs.append(jnp.stack([kg[:, -keep:], vg[:, -keep:]], axis=2))
        else:
            buf = kv_bufs[gi]
            o, lse, nbuf = dilated_step_attention(qg, kg, vg, buf[:, :, 0], buf[:, :, 1], window, dil)
            new_bufs.append(nbuf)
        outs.append(o)
        lses.append(lse)
    wts = jax.nn.softmax(jnp.stack(lses, axis=0), axis=0)
    o = jnp.sum(wts[..., None] * jnp.stack(outs, axis=0), axis=0)
    return o.reshape(bsz, seq, ATTN_OUT).astype(zq.dtype), new_bufs


def trunk_layer(x, c, pos, ssm_st, conv_st, kv_st, ffn_st, p):
    dt = x.dtype
    mod = jax.nn.silu(c) @ p['w_ada'] + p['b_ada']
    sh1, sc1, gt1, sh2, sc2, gt2 = jnp.split(mod[:, None, :], 6, axis=-1)
    h = rmsnorm(x, p['g_norm_mix']) * (1.0 + sc1) + sh1
    z = h @ p['w_in']
    c1 = SSM_WIDTH
    c2 = c1 + 2 * CONV_WIDTH
    c3 = c2 + ATTN_WIDTH
    c4 = c3 + ATTN_WIDTH
    y_ssm, new_ssm = ssm_branch(z[..., :c1], ssm_st, p['ssm_a_re'], p['ssm_a_im'], p['ssm_log_dt'],
                                p['ssm_b_re'], p['ssm_b_im'], p['ssm_c_re'], p['ssm_c_im'], p['ssm_d'],
                                p['ssm_w_glu'], p['ssm_b_glu'])
    y_conv, new_conv = conv_branch(z[..., c1:c2], conv_st, p['conv_w'], p['conv_b'],
                                   p['conv_ln_g'], p['conv_ln_b'])
    y_attn, new_kv = attn_branch(z[..., c2:c3], z[..., c3:c4], z[..., c4:], pos, kv_st,
                                 p['attn_gq'], p['attn_gk'])
    g_ssm, g_conv, g_attn = jnp.split(jax.nn.sigmoid(h @ p['w_gate'] + p['b_gate']), 3, axis=-1)
    merged = (g_ssm * (y_ssm @ p['w_br_ssm']) + g_conv * (y_conv @ p['w_br_conv'])
              + g_attn * (y_attn @ p['w_br_attn']))
    x = x + gt1 * (merged @ p['w_out'])
    h2 = rmsnorm(x, p['g_norm_ffn']) * (1.0 + sc2) + sh2
    up, new_ffn = causal_dwconv(h2 @ p['ffn_w_up'], ffn_st, p['ffn_conv_w'], p['ffn_conv_b'])
    a, b = jnp.split(up, 2, axis=-1)
    x = x + gt2 * ((jax.nn.gelu(a) * b) @ p['ffn_w_down'])
    return x.astype(dt), (new_ssm, new_conv, new_kv[0], new_kv[1], new_kv[2], new_ffn)


def setup_inputs(seed: int = 0) -> dict:
    key = jax.random.key(seed)
    ks = iter(jax.random.split(key, 64))

    def nrm(shape, scale):
        return scale * jax.random.normal(next(ks), shape, F32)

    D = D_MODEL
    F2 = 2 * FFN_HIDDEN
    kv_shape = lambda w: (DEPTH, DEC_BATCH, min(w, PAST_LEN), 2, HEADS_PER_GROUP, HEAD_DIM)
    return {
        'x_prompt': nrm((BATCH, SEQ, D), 1.0),
        'x_sample': nrm((DEC_BATCH, DEC_SEQ, D), 1.0),
        'state_ssm': nrm((DEPTH, DEC_BATCH, SSM_GROUPS, SSM_N, 2), 0.1),
        'cache_conv': nrm((DEPTH, DEC_BATCH, CONV_K - 1, CONV_WIDTH), 0.5),
        'cache_kv_w128': nrm(kv_shape(128), 1.0),
        'cache_kv_w512': nrm(kv_shape(512), 1.0),
        'cache_kv_w2048': nrm(kv_shape(2048), 1.0),
        'cache_ffn': nrm((DEPTH, DEC_BATCH, FFN_CONV_K - 1, F2), 0.5),
        'c_prompt': nrm((BATCH, D), 1.0),
        'c_sample': nrm((DEC_BATCH, D), 1.0),
        'w_ada': nrm((DEPTH, D, 6 * D), 0.5 * D ** -0.5),
        'b_ada': nrm((DEPTH, 6 * D), 0.02),
        'g_norm_mix': 1.0 + nrm((DEPTH, D), 0.02),
        'w_in': nrm((DEPTH, D, IN_WIDTH), D ** -0.5),
        'ssm_a_re': -0.5 + nrm((DEPTH, SSM_GROUPS, SSM_N), 0.01),
        'ssm_a_im': math.pi * jnp.arange(SSM_N, dtype=F32)[None, None, :] + nrm((DEPTH, SSM_GROUPS, SSM_N), 0.01),
        'ssm_log_dt': jax.random.uniform(next(ks), (DEPTH, SSM_GROUPS), F32, math.log(1e-3), math.log(1e-1)),
        'ssm_b_re': nrm((DEPTH, SSM_GROUPS, SSM_N, SSM_GROUP), (2 * SSM_GROUP) ** -0.5),
        'ssm_b_im': nrm((DEPTH, SSM_GROUPS, SSM_N, SSM_GROUP), (2 * SSM_GROUP) ** -0.5),
        'ssm_c_re': nrm((DEPTH, SSM_GROUPS, SSM_GROUP, SSM_N), SSM_N ** -0.5),
        'ssm_c_im': nrm((DEPTH, SSM_GROUPS, SSM_GROUP, SSM_N), SSM_N ** -0.5),
        'ssm_d': nrm((DEPTH, SSM_WIDTH), 0.5),
        'ssm_w_glu': nrm((DEPTH, SSM_WIDTH, SSM_WIDTH), SSM_WIDTH ** -0.5),
        'ssm_b_glu': nrm((DEPTH, SSM_WIDTH), 0.02),
        'conv_w': nrm((DEPTH, CONV_K, CONV_WIDTH), CONV_K ** -0.5),
        'conv_b': nrm((DEPTH, CONV_WIDTH), 0.02),
        'conv_ln_g': 1.0 + nrm((DEPTH, CONV_WIDTH), 0.02),
        'conv_ln_b': nrm((DEPTH, CONV_WIDTH), 0.02),
        'attn_gq': 1.0 + nrm((DEPTH, HEAD_DIM), 0.02),
        'attn_gk': 1.0 + nrm((DEPTH, HEAD_DIM), 0.02),
        'w_gate': nrm((DEPTH, D, 3 * D), D ** -0.5),
        'b_gate': nrm((DEPTH, 3 * D), 0.02),
        'w_br_ssm': nrm((DEPTH, SSM_WIDTH, D), SSM_WIDTH ** -0.5),
        'w_br_conv': nrm((DEPTH, CONV_WIDTH, D), CONV_WIDTH ** -0.5),
        'w_br_attn': nrm((DEPTH, ATTN_OUT, D), ATTN_OUT ** -0.5),
        'w_out': nrm((DEPTH, D, D), D ** -0.5),
        'g_norm_ffn': 1.0 + nrm((DEPTH, D), 0.02),
        'ffn_w_up': nrm((DEPTH, D, F2), D ** -0.5),
        'ffn_conv_w': nrm((DEPTH, FFN_CONV_K, F2), FFN_CONV_K ** -0.5),
        'ffn_conv_b': nrm((DEPTH, F2), 0.02),
        'ffn_w_down': nrm((DEPTH, FFN_HIDDEN, D), FFN_HIDDEN ** -0.5),
    }


def reference(x_prompt, x_sample, state_ssm, cache_conv, cache_kv_w128, cache_kv_w512, cache_kv_w2048,
              cache_ffn, c_prompt, c_sample, w_ada, b_ada, g_norm_mix, w_in, ssm_a_re, ssm_a_im, ssm_log_dt,
              ssm_b_re, ssm_b_im, ssm_c_re, ssm_c_im, ssm_d, ssm_w_glu, ssm_b_glu, conv_w, conv_b, conv_ln_g,
              conv_ln_b, attn_gq, attn_gk, w_gate, b_gate, w_br_ssm, w_br_conv, w_br_attn, w_out, g_norm_ffn,
              ffn_w_up, ffn_conv_w, ffn_conv_b, ffn_w_down):
    bp, seq_p, _ = x_prompt.shape
    dt = x_prompt.dtype
    pos_p = jnp.arange(seq_p, dtype=jnp.int32)
    pos_s = PAST_LEN + jnp.arange(x_sample.shape[1], dtype=jnp.int32)
    yp, ys = x_prompt, x_sample
    new_p = [[] for _ in range(6)]
    new_s = [[] for _ in range(6)]
    for l in range(DEPTH):
        lp = dict(w_ada=w_ada[l], b_ada=b_ada[l], g_norm_mix=g_norm_mix[l], w_in=w_in[l],
                  ssm_a_re=ssm_a_re[l], ssm_a_im=ssm_a_im[l], ssm_log_dt=ssm_log_dt[l],
                  ssm_b_re=ssm_b_re[l], ssm_b_im=ssm_b_im[l], ssm_c_re=ssm_c_re[l], ssm_c_im=ssm_c_im[l],
                  ssm_d=ssm_d[l], ssm_w_glu=ssm_w_glu[l], ssm_b_glu=ssm_b_glu[l],
                  conv_w=conv_w[l], conv_b=conv_b[l], conv_ln_g=conv_ln_g[l], conv_ln_b=conv_ln_b[l],
                  attn_gq=attn_gq[l], attn_gk=attn_gk[l], w_gate=w_gate[l], b_gate=b_gate[l],
                  w_br_ssm=w_br_ssm[l], w_br_conv=w_br_conv[l], w_br_attn=w_br_attn[l], w_out=w_out[l],
                  g_norm_ffn=g_norm_ffn[l], ffn_w_up=ffn_w_up[l], ffn_conv_w=ffn_conv_w[l],
                  ffn_conv_b=ffn_conv_b[l], ffn_w_down=ffn_w_down[l])
        yp, st_p = trunk_layer(yp, c_prompt, pos_p,
                               jnp.zeros((bp, SSM_GROUPS, SSM_N, 2), F32),
                               jnp.zeros((bp, CONV_K - 1, CONV_WIDTH), dt),
                               None,
                               jnp.zeros((bp, FFN_CONV_K - 1, 2 * FFN_HIDDEN), dt), lp)
        ys, st_s = trunk_layer(ys, c_sample, pos_s, state_ssm[l], cache_conv[l],
                               (cache_kv_w128[l], cache_kv_w512[l], cache_kv_w2048[l]), cache_ffn[l], lp)
        for i in range(6):
            new_p[i].append(st_p[i])
            new_s[i].append(st_s[i])
    ssm_p, conv_p, kv128_p, kv512_p, kv2048_p, ffn_p = [jnp.stack(t, axis=0) for t in new_p]
    ssm_s, conv_s, kv128_s, kv512_s, kv2048_s, ffn_s = [jnp.stack(t, axis=0) for t in new_s]
    return (yp, ys, ssm_p, ssm_s, conv_p, conv_s, kv128_p, kv128_s, kv512_p, kv512_s, kv2048_p, kv2048_s, ffn_p, ffn_s)
```

```python
import functools
import math

import jax
import jax.numpy as jnp
from jax import lax
from jax.experimental import pallas as pl
from jax.experimental.pallas import tpu as pltpu

F32 = jnp.float32
BF16 = jnp.bfloat16

D_MODEL = 1024
SSM_WIDTH = 512
SSM_GROUP = 16
SSM_GROUPS = 32
SSM_N = 64
CONV_WIDTH = 512
CONV_K = 31
HEAD_DIM = 64
HEADS_PER_GROUP = 4
ATTN_GROUPS = ((128, 1), (512, 4), (2048, 16))
ATTN_HEADS = 12
ATTN_WIDTH = 768
ATTN_OUT = 256
ATTN_BLOCK = 128
ROPE_THETA = 10000.0
FFN_HIDDEN = 2816
FFN_CONV_K = 3
IN_WIDTH = SSM_WIDTH + 2 * CONV_WIDTH + 3 * ATTN_WIDTH
EPS = 1e-6
PAST_LEN = 8192

SUBLANES = 8
LANES = 128
MXU_DIM = 256
SLAB_GROUPS = 8
N_SLABS = SSM_GROUPS // SLAB_GROUPS
SLAB_CH = SLAB_GROUPS * SSM_GROUP
SLAB_ST = SLAB_GROUPS * SSM_N
STATE_W = 2 * SSM_GROUPS * SSM_N
NEG_BIG = -1e30
GELU_C = math.sqrt(2.0 / math.pi)

TM_INPROJ = 512
L_SSM = 256
T_CONV = 256
TM_MERGE = 256
TM_FFN = 256
FFN_CHUNK = 256
HIST = 8
CONV_HALO = 32
KV_TAIL = 2048


def _cparams(sem, vmem_mb):
    return pltpu.CompilerParams(dimension_semantics=sem, vmem_limit_bytes=vmem_mb * 1024 * 1024)


def _sigmoid(x):
    return 1.0 / (1.0 + jnp.exp(-x))


def _gelu_tanh(x):
    return 0.5 * x * (1.0 + jnp.tanh(GELU_C * (x + 0.044715 * (x * x * x))))


def _mm(a, w, precise):
    if precise:
        return jnp.dot(a, w, precision=lax.Precision.HIGHEST, preferred_element_type=F32)
    return jnp.dot(a.astype(BF16), w, preferred_element_type=F32)


def _norm_mod(x, g, scale, shift):
    ms = jnp.mean(x * x, axis=-1, keepdims=True)
    return (x * lax.rsqrt(ms + EPS) * g) * (1.0 + scale) + shift


def _ada_body(c_ref, w_ref, b_ref, o_ref):
    c = c_ref[...]
    s = c * _sigmoid(c)
    o_ref[...] = jnp.dot(s, w_ref[...], precision=lax.Precision.HIGHEST,
                         preferred_element_type=F32) + b_ref[...]


def _ada_call(c_all, w_ada, b_ada):
    depth, d, n6 = w_ada.shape
    rows = c_all.shape[0]
    tn = 1536
    return pl.pallas_call(
        _ada_body,
        grid=(depth, n6 // tn),
        in_specs=[
            pl.BlockSpec((rows, d), lambda l, j: (0, 0)),
            pl.BlockSpec((None, d, tn), lambda l, j: (l, 0, j)),
            pl.BlockSpec((None, 1, tn), lambda l, j: (l, 0, j)),
        ],
        out_specs=pl.BlockSpec((None, rows, tn), lambda l, j: (l, 0, j)),
        out_shape=jax.ShapeDtypeStruct((depth, rows, n6), F32),
        compiler_params=_cparams(("arbitrary", "arbitrary"), 40),
        name="ada_mod",
    )(c_all, w_ada, b_ada.reshape(depth, 1, n6))


def _head_norm_rope(z, gvec, ones, cos, sins, precise):
    sq = z * z
    parts = []
    for c in range(ATTN_WIDTH // MXU_DIM):
        parts.append(_mm(sq[:, c * MXU_DIM:(c + 1) * MXU_DIM], ones, precise))
    ms = jnp.concatenate(parts, axis=1) * (1.0 / HEAD_DIM)
    y = z * lax.rsqrt(ms + EPS) * gvec
    lane = lax.broadcasted_iota(jnp.int32, (1, LANES), 1)
    first = (lane % HEAD_DIM) < (HEAD_DIM // 2)
    outs = []
    for c in range(ATTN_WIDTH // LANES):
        yc = y[:, c * LANES:(c + 1) * LANES]
        partner = jnp.where(first, pltpu.roll(yc, LANES - HEAD_DIM // 2, 1),
                            pltpu.roll(yc, HEAD_DIM // 2, 1))
        outs.append(yc * cos + partner * sins)
    return jnp.concatenate(outs, axis=1)


def _inproj_body(*refs, sample):
    if sample:
        (x_ref, mod_ref, g_ref, w_ref, gq_ref, gk_ref, cos_ref, sin_ref, ones_ref,
         zs_ref, zc_ref, q_ref, k_ref, v_ref) = refs
    else:
        (x_ref, mod_ref, g_ref, w_ref, gq_ref, gk_ref, cos_ref, sin_ref, ones_ref, p4_ref, p16_ref,
         zs_ref, zc_ref, kt_ref, vt_ref,
         q0_ref, k0_ref, v0_ref, q1_ref, k1_ref, v1_ref, q2_ref, k2_ref, v2_ref) = refs
    precise = sample
    x = x_ref[...]
    h = _norm_mod(x, g_ref[...], mod_ref[1], mod_ref[0])
    hm = h if precise else h.astype(BF16)
    c1 = SSM_WIDTH
    c2 = c1 + 2 * CONV_WIDTH
    c3 = c2 + ATTN_WIDTH
    c4 = c3 + ATTN_WIDTH
    zs_ref[...] = _mm(hm, w_ref[:, 0:c1], precise)
    zc_ref[...] = _mm(hm, w_ref[:, c1:c2], precise)
    zq = _mm(hm, w_ref[:, c2:c3], precise)
    zk = _mm(hm, w_ref[:, c3:c4], precise)
    zv = _mm(hm, w_ref[:, c4:IN_WIDTH], precise)
    cos = cos_ref[...]
    sins = sin_ref[...]
    ones = ones_ref[...]
    q = _head_norm_rope(zq, gq_ref[...], ones, cos, sins, precise)
    k = _head_norm_rope(zk, gk_ref[...], ones, cos, sins, precise)
    if sample:
        q_ref[...] = q
        k_ref[...] = k
        v_ref[...] = zv
        return
    kt_ref[...] = k
    vt_ref[...] = zv
    qb = (q * (HEAD_DIM ** -0.5)).astype(BF16)
    kb = k.astype(BF16)
    vb = zv.astype(BF16)
    gw = HEADS_PER_GROUP * HEAD_DIM
    q0_ref[0] = qb[:, 0:gw]
    k0_ref[0] = kb[:, 0:gw]
    v0_ref[0] = vb[:, 0:gw]
    for gi, (p_ref, outs) in enumerate(((p4_ref, (q1_ref, k1_ref, v1_ref)),
                                        (p16_ref, (q2_ref, k2_ref, v2_ref))), start=1):
        pm = p_ref[...]
        for src, o_ref in zip((qb, kb, vb), outs):
            dil, rows = o_ref.shape[0], o_ref.shape[1]
            perm = jnp.dot(pm, src[:, gi * gw:(gi + 1) * gw], preferred_element_type=F32).astype(BF16)
            for r in range(dil):
                o_ref[r] = perm[r * rows:(r + 1) * rows]


def _inproj_prompt(x, modp, l, gnorm, w_in, gq, gk, cos, sins, ones, p4, p16):
    b_sz, seq, d = x.shape
    tm = TM_INPROJ
    nt = seq // tm
    tail = min(KV_TAIL, seq)
    ft = nt - tail // tm
    gw = HEADS_PER_GROUP * HEAD_DIM
    tmap = lambda b, i: (b, jnp.maximum(i - ft, 0), 0)
    in_specs = [
        pl.BlockSpec((None, tm, d), lambda b, i: (b, i, 0)),
        pl.BlockSpec((None, None, 6, 1, d), lambda b, i: (l, b, 0, 0, 0)),
        pl.BlockSpec((None, 1, d), lambda b, i: (l, 0, 0)),
        pl.BlockSpec((None, d, IN_WIDTH), lambda b, i: (l, 0, 0)),
        pl.BlockSpec((None, 1, ATTN_WIDTH), lambda b, i: (l, 0, 0)),
        pl.BlockSpec((None, 1, ATTN_WIDTH), lambda b, i: (l, 0, 0)),
        pl.BlockSpec((tm, LANES), lambda b, i: (i, 0)),
        pl.BlockSpec((tm, LANES), lambda b, i: (i, 0)),
        pl.BlockSpec((MXU_DIM, MXU_DIM), lambda b, i: (0, 0)),
        pl.BlockSpec((tm, tm), lambda b, i: (0, 0)),
        pl.BlockSpec((tm, tm), lambda b, i: (0, 0)),
    ]
    out_shape = [
        jax.ShapeDtypeStruct((b_sz, seq, SSM_WIDTH), F32),
        jax.ShapeDtypeStruct((b_sz, seq, 2 * CONV_WIDTH), F32),
        jax.ShapeDtypeStruct((b_sz, tail, ATTN_WIDTH), F32),
        jax.ShapeDtypeStruct((b_sz, tail, ATTN_WIDTH), F32),
    ]
    out_specs = [
        pl.BlockSpec((None, tm, SSM_WIDTH), lambda b, i: (b, i, 0)),
        pl.BlockSpec((None, tm, 2 * CONV_WIDTH), lambda b, i: (b, i, 0)),
        pl.BlockSpec((None, tm, ATTN_WIDTH), tmap),
        pl.BlockSpec((None, tm, ATTN_WIDTH), tmap),
    ]
    for _, dil in ATTN_GROUPS:
        for _ in range(3):
            out_shape.append(jax.ShapeDtypeStruct((b_sz, dil, seq // dil, gw), BF16))
            out_specs.append(pl.BlockSpec((None, dil, tm // dil, gw), lambda b, i: (b, 0, i, 0)))
    return pl.pallas_call(
        functools.partial(_inproj_body, sample=False),
        grid=(b_sz, nt),
        in_specs=in_specs,
        out_specs=out_specs,
        out_shape=out_shape,
        compiler_params=_cparams(("arbitrary", "arbitrary"), 56),
        name="inproj_prompt",
    )(x, modp, gnorm, w_in, gq, gk, cos, sins, ones, p4, p16)


def _inproj_sample(x, mods, l, gnorm, w_in, gq, gk, cos, sins, ones):
    n, d = x.shape
    lsel = lambda i: (l, 0, 0)
    in_specs = [
        pl.BlockSpec((n, d), lambda i: (0, 0)),
        pl.BlockSpec((None, 6, n, d), lambda i: (l, 0, 0, 0)),
        pl.BlockSpec((None, 1, d), lsel),
        pl.BlockSpec((None, d, IN_WIDTH), lsel),
        pl.BlockSpec((None, 1, ATTN_WIDTH), lsel),
        pl.BlockSpec((None, 1, ATTN_WIDTH), lsel),
        pl.BlockSpec((n, LANES), lambda i: (0, 0)),
        pl.BlockSpec((n, LANES), lambda i: (0, 0)),
        pl.BlockSpec((MXU_DIM, MXU_DIM), lambda i: (0, 0)),
    ]
    widths = (SSM_WIDTH, 2 * CONV_WIDTH, ATTN_WIDTH, ATTN_WIDTH, ATTN_WIDTH)
    return pl.pallas_call(
        functools.partial(_inproj_body, sample=True),
        grid=(1,),
        in_specs=in_specs,
        out_specs=[pl.BlockSpec((n, w), lambda i: (0, 0)) for w in widths],
        out_shape=[jax.ShapeDtypeStruct((n, w), F32) for w in widths],
        compiler_params=_cparams(("arbitrary",), 48),
        name="inproj_sample",
    )(x, mods, gnorm, w_in, gq, gk, cos, sins, ones)


def _ssm_tail(y_raw, u, d_ref, wglu_ref, bglu_ref, precise):
    y = _gelu_tanh(y_raw + d_ref[...] * u)
    return y * _sigmoid(_mm(y, wglu_ref[...], precise) + bglu_ref[...])


def _ssm_prompt_body(zs_ref, s0_ref, pm_ref, pmt_ref, b_ref, c_ref, lam_ref, lamp_ref, pw_ref,
                     d_ref, wglu_ref, bglu_ref, y_ref, st_ref,
                     v_ref, sb_ref, cs_ref, carry_ref):
    ci = pl.program_id(1)
    chunk = zs_ref.shape[0]
    seg = chunk // SUBLANES

    @pl.when(ci == 0)
    def _():
        carry_ref[...] = s0_ref[...]

    u = zs_ref[...]
    up = jnp.dot(pm_ref[...], u.astype(BF16), preferred_element_type=F32).astype(BF16)
    y_parts = []
    for m in range(N_SLABS):
        base = m * 2 * SLAB_ST
        v_ref[...] = jnp.dot(up[:, m * SLAB_CH:(m + 1) * SLAB_CH], b_ref[m], preferred_element_type=F32)
        lr = jnp.broadcast_to(lam_ref[m, 0:1, :], (SUBLANES, SLAB_ST))
        li = jnp.broadcast_to(lam_ref[m, 1:2, :], (SUBLANES, SLAB_ST))

        def scan_step(i, carry):
            sr, si = carry
            off = pl.multiple_of(i * SUBLANES, SUBLANES)
            vr = v_ref[pl.ds(off, SUBLANES), 0:SLAB_ST]
            vi = v_ref[pl.ds(off, SUBLANES), SLAB_ST:2 * SLAB_ST]
            nsr = lr * sr - li * si + vr
            nsi = lr * si + li * sr + vi
            v_ref[pl.ds(off, SUBLANES), 0:SLAB_ST] = nsr
            v_ref[pl.ds(off, SUBLANES), SLAB_ST:2 * SLAB_ST] = nsi
            return nsr, nsi

        zero = jnp.zeros((SUBLANES, SLAB_ST), F32)
        sr, si = lax.fori_loop(0, seg, scan_step, (zero, zero))
        er = carry_ref[0:1, base:base + SLAB_ST]
        ei = carry_ref[0:1, base + SLAB_ST:base + 2 * SLAB_ST]
        pr = lamp_ref[m, 0:1, :]
        pi = lamp_ref[m, 1:2, :]
        for r in range(SUBLANES):
            cs_ref[r:r + 1, 0:SLAB_ST] = er
            cs_ref[r:r + 1, SLAB_ST:2 * SLAB_ST] = ei
            ner = sr[r:r + 1] + pr * er - pi * ei
            nei = si[r:r + 1] + pr * ei + pi * er
            er, ei = ner, nei
        carry_ref[0:1, base:base + SLAB_ST] = er
        carry_ref[0:1, base + SLAB_ST:base + 2 * SLAB_ST] = ei
        csr = cs_ref[:, 0:SLAB_ST]
        csi = cs_ref[:, SLAB_ST:2 * SLAB_ST]

        def fix_step(i2, _):
            off = pl.multiple_of(i2 * 2 * SUBLANES, 2 * SUBLANES)
            rows_r, rows_i = [], []
            for h in range(2):
                i = i2 * 2 + h
                qr = jnp.broadcast_to(pw_ref[m, 0, pl.ds(i, 1), :], (SUBLANES, SLAB_ST))
                qi = jnp.broadcast_to(pw_ref[m, 1, pl.ds(i, 1), :], (SUBLANES, SLAB_ST))
                o8 = pl.multiple_of(off + h * SUBLANES, SUBLANES)
                rows_r.append(v_ref[pl.ds(o8, SUBLANES), 0:SLAB_ST] + qr * csr - qi * csi)
                rows_i.append(v_ref[pl.ds(o8, SUBLANES), SLAB_ST:2 * SLAB_ST] + qr * csi + qi * csr)
            sb_ref[pl.ds(off, 2 * SUBLANES), 0:SLAB_ST] = jnp.concatenate(rows_r, axis=0).astype(BF16)
            sb_ref[pl.ds(off, 2 * SUBLANES), SLAB_ST:2 * SLAB_ST] = jnp.concatenate(rows_i, axis=0).astype(BF16)
            return 0

        lax.fori_loop(0, seg // 2, fix_step, 0)
        y_parts.append(jnp.dot(sb_ref[...], c_ref[m], preferred_element_type=F32))
    y_perm = jnp.concatenate(y_parts, axis=1)
    hi = y_perm.astype(BF16)
    lo = (y_perm - hi.astype(F32)).astype(BF16)
    pmt = pmt_ref[...]
    y_nat = jnp.dot(pmt, hi, preferred_element_type=F32) + jnp.dot(pmt, lo, preferred_element_type=F32)
    y_ref[...] = _ssm_tail(y_nat, u, d_ref, wglu_ref, bglu_ref, False).astype(BF16)
    st_ref[...] = carry_ref[...]


def _ssm_prompt(zs, s0, l, pm, pmt, tabs, ssm_d, w_glu, b_glu):
    b_sz, seq, _ = zs.shape
    chunk = L_SSM
    seg = chunk // SUBLANES
    full = lambda *shape: pl.BlockSpec(shape, lambda b, c: (0,) * len(shape))
    lsel = lambda b, c: (l, 0, 0)
    return pl.pallas_call(
        _ssm_prompt_body,
        grid=(b_sz, seq // chunk),
        in_specs=[
            pl.BlockSpec((None, chunk, SSM_WIDTH), lambda b, c: (b, c, 0)),
            pl.BlockSpec((None, 1, STATE_W), lambda b, c: (b, 0, 0)),
            full(chunk, chunk), full(chunk, chunk),
            full(N_SLABS, SLAB_CH, 2 * SLAB_ST),
            full(N_SLABS, 2 * SLAB_ST, SLAB_CH),
            full(N_SLABS, 2, SLAB_ST), full(N_SLABS, 2, SLAB_ST),
            full(N_SLABS, 2, seg, SLAB_ST),
            pl.BlockSpec((None, 1, SSM_WIDTH), lsel),
            pl.BlockSpec((None, SSM_WIDTH, SSM_WIDTH), lsel),
            pl.BlockSpec((None, 1, SSM_WIDTH), lsel),
        ],
        out_specs=[
            pl.BlockSpec((None, chunk, SSM_WIDTH), lambda b, c: (b, c, 0)),
            pl.BlockSpec((None, 1, STATE_W), lambda b, c: (b, 0, 0)),
        ],
        out_shape=[jax.ShapeDtypeStruct((b_sz, seq, SSM_WIDTH), BF16),
                   jax.ShapeDtypeStruct((b_sz, 1, STATE_W), F32)],
        scratch_shapes=[
            pltpu.VMEM((chunk, 2 * SLAB_ST), F32),
            pltpu.VMEM((chunk, 2 * SLAB_ST), BF16),
            pltpu.VMEM((SUBLANES, 2 * SLAB_ST), F32),
            pltpu.VMEM((1, STATE_W), F32),
        ],
        compiler_params=_cparams(("arbitrary", "arbitrary"), 40),
        name="ssm_prompt",
    )(zs, s0, pm, pmt, tabs["b_bf"], tabs["c_bf"], tabs["lam"], tabs["lam_seg"], tabs["pw"],
      ssm_d, w_glu, b_glu)


def _ssm_sample_body(u_ref, s0_ref, b_ref, c_ref, lam_ref, d_ref, wglu_ref, bglu_ref, y_ref, st_ref):
    u = u_ref[...]
    y_parts = []
    for m in range(N_SLABS):
        base = m * 2 * SLAB_ST
        v = _mm(u[:, m * SLAB_CH:(m + 1) * SLAB_CH], b_ref[m], True)
        lr = lam_ref[m, 0:1, :]
        li = lam_ref[m, 1:2, :]
        sr0 = s0_ref[:, base:base + SLAB_ST]
        si0 = s0_ref[:, base + SLAB_ST:base + 2 * SLAB_ST]
        sr = lr * sr0 - li * si0 + v[:, 0:SLAB_ST]
        si = lr * si0 + li * sr0 + v[:, SLAB_ST:2 * SLAB_ST]
        st_ref[:, base:base + SLAB_ST] = sr
        st_ref[:, base + SLAB_ST:base + 2 * SLAB_ST] = si
        y_parts.append(_mm(jnp.concatenate([sr, si], axis=1), c_ref[m], True))
    y_raw = jnp.concatenate(y_parts, axis=1)
    y_ref[...] = _ssm_tail(y_raw, u, d_ref, wglu_ref, bglu_ref, True)


def _ssm_sample(zs, s0, l, tabs, ssm_d, w_glu, b_glu):
    n = zs.shape[0]
    full = lambda *shape: pl.BlockSpec(shape, lambda i: (0,) * len(shape))
    lsel = lambda i: (l, 0, 0)
    return pl.pallas_call(
        _ssm_sample_body,
        grid=(1,),
        in_specs=[
            full(n, SSM_WIDTH), full(n, STATE_W),
            full(N_SLABS, SLAB_CH, 2 * SLAB_ST), full(N_SLABS, 2 * SLAB_ST, SLAB_CH),
            full(N_SLABS, 2, SLAB_ST),
            pl.BlockSpec((None, 1, SSM_WIDTH), lsel),
            pl.BlockSpec((None, SSM_WIDTH, SSM_WIDTH), lsel),
            pl.BlockSpec((None, 1, SSM_WIDTH), lsel),
        ],
        out_specs=[full(n, SSM_WIDTH), full(n, STATE_W)],
        out_shape=[jax.ShapeDtypeStruct((n, SSM_WIDTH), F32), jax.ShapeDtypeStruct((n, STATE_W), F32)],
        compiler_params=_cparams(("arbitrary",), 32),
        name="ssm_sample",
    )(zs, s0, tabs["b_f32"], tabs["c_f32"], tabs["lam"], ssm_d, w_glu, b_glu)


def _ssm_tables(a_re, a_im, log_dt, b_re, b_im, c_re, c_im, seg):
    dt = jnp.exp(log_dt)[:, None]
    xr, xi = a_re * dt, a_im * dt
    mag = jnp.exp(xr)
    lr, li = mag * jnp.cos(xi), mag * jnp.sin(xi)
    den = a_re * a_re + a_im * a_im
    nr, ni = lr - 1.0, li
    cr = (nr * a_re + ni * a_im) / den
    cim = (ni * a_re - nr * a_im) / den
    bbr = cr[..., None] * b_re - cim[..., None] * b_im
    bbi = cr[..., None] * b_im + cim[..., None] * b_re
    eye = jnp.eye(SLAB_GROUPS, dtype=F32)

    def b_slab(t):
        t = t.reshape(N_SLABS, SLAB_GROUPS, SSM_N, SSM_GROUP)
        return jnp.einsum('mgnc,gh->mgchn', t, eye).reshape(N_SLABS, SLAB_CH, SLAB_ST)

    def c_slab(t):
        t = t.reshape(N_SLABS, SLAB_GROUPS, SSM_GROUP, SSM_N)
        return jnp.einsum('mgcn,gh->mgnhc', t, eye).reshape(N_SLABS, SLAB_ST, SLAB_CH)

    b_mat = jnp.concatenate([b_slab(bbr), b_slab(bbi)], axis=2)
    c_mat = jnp.concatenate([c_slab(c_re), c_slab(-c_im)], axis=1)

    def power(k):
        mk = jnp.exp(k * xr)
        return jnp.stack([(mk * jnp.cos(k * xi)).reshape(N_SLABS, SLAB_ST),
                          (mk * jnp.sin(k * xi)).reshape(N_SLABS, SLAB_ST)], axis=1)

    pw = jnp.stack([power(float(k)) for k in range(1, seg + 1)], axis=2)
    return dict(b_f32=b_mat, c_f32=c_mat, b_bf=b_mat.astype(BF16), c_bf=c_mat.astype(BF16),
                lam=power(1.0), lam_seg=power(float(seg)), pw=pw)


def _state_to_slab(s):
    b = s.shape[0]
    return s.reshape(b, N_SLABS, SLAB_GROUPS, SSM_N, 2).transpose(0, 1, 4, 2, 3).reshape(b, STATE_W)


def _slab_to_state(x):
    b = x.shape[0]
    return x.reshape(b, N_SLABS, 2, SLAB_GROUPS, SSM_N).transpose(0, 1, 3, 4, 2).reshape(b, SSM_GROUPS, SSM_N, 2)


def _ln_silu(y, g, b):
    mu = jnp.mean(y, axis=-1, keepdims=True)
    var = jnp.mean(jnp.square(y - mu), axis=-1, keepdims=True)
    t = (y - mu) * lax.rsqrt(var + EPS) * g + b
    return t * _sigmoid(t)


def _conv_prompt_body(zc_ref, w_ref, b_ref, lg_ref, lb_ref, y_ref, tail_ref, abuf_ref):
    i = pl.program_id(1)
    rows = zc_ref.shape[0]

    @pl.when(i == 0)
    def _():
        abuf_ref[0:CONV_HALO, :] = jnp.zeros((CONV_HALO, CONV_WIDTH), F32)

    z = zc_ref[...]
    a = z[:, 0:CONV_WIDTH] * _sigmoid(z[:, CONV_WIDTH:2 * CONV_WIDTH])
    abuf_ref[CONV_HALO:CONV_HALO + rows, :] = a
    acc = jnp.zeros((rows, CONV_WIDTH), F32) + b_ref[...]
    first = CONV_HALO - (CONV_K - 1)
    for j in range(CONV_K):
        acc = acc + w_ref[j:j + 1, :] * abuf_ref[first + j:first + j + rows, :]
    y_ref[...] = _ln_silu(acc, lg_ref[...], lb_ref[...]).astype(BF16)
    last = a[rows - CONV_HALO:rows]
    tail_ref[...] = last
    abuf_ref[0:CONV_HALO, :] = last


def _conv_prompt(zc, l, conv_w, conv_b, ln_g, ln_b):
    b_sz, seq, _ = zc.shape
    rows = T_CONV
    lsel = lambda b, i: (l, 0, 0)
    return pl.pallas_call(
        _conv_prompt_body,
        grid=(b_sz, seq // rows),
        in_specs=[
            pl.BlockSpec((None, rows, 2 * CONV_WIDTH), lambda b, i: (b, i, 0)),
            pl.BlockSpec((None, CONV_K, CONV_WIDTH), lsel),
            pl.BlockSpec((None, 1, CONV_WIDTH), lsel),
            pl.BlockSpec((None, 1, CONV_WIDTH), lsel),
            pl.BlockSpec((None, 1, CONV_WIDTH), lsel),
        ],
        out_specs=[
            pl.BlockSpec((None, rows, CONV_WIDTH), lambda b, i: (b, i, 0)),
            pl.BlockSpec((None, CONV_HALO, CONV_WIDTH), lambda b, i: (b, 0, 0)),
        ],
        out_shape=[jax.ShapeDtypeStruct((b_sz, seq, CONV_WIDTH), BF16),
                   jax.ShapeDtypeStruct((b_sz, CONV_HALO, CONV_WIDTH), F32)],
        scratch_shapes=[pltpu.VMEM((rows + CONV_HALO, CONV_WIDTH), F32)],
        compiler_params=_cparams(("arbitrary", "arbitrary"), 32),
        name="conv_prompt",
    )(zc, conv_w, conv_b, ln_g, ln_b)


def _conv_sample_body(zc_ref, hist_ref, w_ref, b_ref, lg_ref, lb_ref, y_ref, a_ref):
    z = zc_ref[...]
    a = z[:, 0:CONV_WIDTH] * _sigmoid(z[:, CONV_WIDTH:2 * CONV_WIDTH])
    a_ref[...] = a
    acc = b_ref[...] + w_ref[CONV_K - 1] * a
    for j in range(CONV_K - 1):
        acc = acc + w_ref[j] * hist_ref[j]
    y_ref[...] = _ln_silu(acc, lg_ref[...], lb_ref[...])


def _conv_sample(zc, hist_t, l, conv_w4, conv_b, ln_g, ln_b):
    n = zc.shape[0]
    full = lambda *shape: pl.BlockSpec(shape, lambda i: (0,) * len(shape))
    lsel = lambda i: (l, 0, 0)
    return pl.pallas_call(
        _conv_sample_body,
        grid=(1,),
        in_specs=[
            full(n, 2 * CONV_WIDTH), full(CONV_K - 1, n, CONV_WIDTH),
            pl.BlockSpec((None, CONV_K, 1, CONV_WIDTH), lambda i: (l, 0, 0, 0)),
            pl.BlockSpec((None, 1, CONV_WIDTH), lsel),
            pl.BlockSpec((None, 1, CONV_WIDTH), lsel),
            pl.BlockSpec((None, 1, CONV_WIDTH), lsel),
        ],
        out_specs=[full(n, CONV_WIDTH), full(n, CONV_WIDTH)],
        out_shape=[jax.ShapeDtypeStruct((n, CONV_WIDTH), F32)] * 2,
        compiler_params=_cparams(("arbitrary",), 32),
        name="conv_sample",
    )(zc, hist_t, conv_w4, conv_b, ln_g, ln_b)


def _attn_prompt_body(q_ref, kc_ref, kp_ref, vc_ref, vp_ref, oa_ref, ob_ref, la_ref, lb_ref, *, dil):
    j = pl.program_id(1)
    r = pl.program_id(2)
    blk = ATTN_BLOCK
    q = q_ref[...]
    k = jnp.concatenate([kp_ref[...], kc_ref[...]], axis=0)
    v = jnp.concatenate([vp_ref[...], vc_ref[...]], axis=0)
    qi = lax.broadcasted_iota(jnp.int32, (blk, 2 * blk), 0)
    ki = lax.broadcasted_iota(jnp.int32, (blk, 2 * blk), 1)
    dist = qi + blk - ki
    band = jnp.where(dist >= 0, jnp.where(dist <= blk, 1, 0), 0)
    first_ok = jnp.where(ki >= blk, 1, jnp.where(j > 0, 1, 0))
    valid = (band * first_ok) > 0
    lane = lax.broadcasted_iota(jnp.int32, (1, LANES), 1)
    lo = lane < HEAD_DIM
    o_parts, lse_parts = [], []
    for hp in range(HEADS_PER_GROUP // 2):
        sl = slice(hp * LANES, (hp + 1) * LANES)
        q2, k2, v2 = q[:, sl], k[:, sl], v[:, sl]
        res = []
        for half in range(2):
            keep = lo if half == 0 else jnp.logical_not(lo)
            qm = jnp.where(keep, q2, jnp.zeros_like(q2))
            s = lax.dot_general(qm, k2, (((1,), (1,)), ((), ())), preferred_element_type=F32)
            s = jnp.where(valid, s, NEG_BIG)
            m = jnp.max(s, axis=-1, keepdims=True)
            p = jnp.exp(s - m)
            den = jnp.sum(p, axis=-1, keepdims=True)
            o = jnp.dot(p.astype(BF16), v2, preferred_element_type=F32)
            res.append((o / den, m + jnp.log(den)))
        o_parts.append(jnp.where(lo, res[0][0], res[1][0]))
        lse_parts.append(jnp.where(lo, res[0][1], res[1][1]))
    for val, ref in zip(o_parts + lse_parts, (oa_ref, ob_ref, la_ref, lb_ref)):
        if dil == 1:
            ref[...] = val
        else:
            ref[pl.ds(r, blk, stride=dil), :] = val


def _attn_prompt(q, k, v, dil):
    b_sz, _, m_len, gw = q.shape
    blk = ATTN_BLOCK
    seq = m_len * dil
    cur = lambda b, j, r: (b, r, j, 0)
    prev = lambda b, j, r: (b, r, jnp.maximum(j - 1, 0), 0)
    bs = lambda f: pl.BlockSpec((None, None, blk, gw), f)
    span = blk * dil
    return pl.pallas_call(
        functools.partial(_attn_prompt_body, dil=dil),
        grid=(b_sz, m_len // blk, dil),
        in_specs=[bs(cur), bs(cur), bs(prev), bs(cur), bs(prev)],
        out_specs=[pl.BlockSpec((None, span, LANES), lambda b, j, r: (b, j, 0))] * 4,
        out_shape=[jax.ShapeDtypeStruct((b_sz, seq, LANES), F32)] * 4,
        compiler_params=_cparams(("arbitrary", "arbitrary", "arbitrary"), 40),
        name=f"attn_prompt_d{dil}",
    )(q, k, k, v, v)


def _attn_sample_body(q_ref, kn_ref, vn_ref, c0_ref, c1_ref, c2_ref, y_ref):
    scale = HEAD_DIM ** -0.5
    outs, lses = [], []
    for g, c_ref in enumerate((c0_ref, c1_ref, c2_ref)):
        qg = q_ref[g]
        kn = kn_ref[g]
        vn = vn_ref[g]
        kb = c_ref[:, 0]
        vb = c_ref[:, 1]
        sh = jnp.sum(kb * qg[None], axis=-1, keepdims=True) * scale
        sn = jnp.sum(kn * qg, axis=-1, keepdims=True) * scale
        m = jnp.maximum(jnp.max(sh, axis=0), sn)
        ph = jnp.exp(sh - m[None])
        pn = jnp.exp(sn - m)
        den = jnp.sum(ph, axis=0) + pn
        outs.append((jnp.sum(ph * vb, axis=0) + pn * vn) / den)
        lses.append(m + jnp.log(den))
    top = jnp.maximum(jnp.maximum(lses[0], lses[1]), lses[2])
    ws = [jnp.exp(t - top) for t in lses]
    tot = ws[0] + ws[1] + ws[2]
    y_ref[...] = (ws[0] * outs[0] + ws[1] * outs[1] + ws[2] * outs[2]) / tot


def _attn_sample(q, kn, vn, caches, l):
    n = q.shape[0]
    ng = len(ATTN_GROUPS)
    small = pl.BlockSpec((None, ng, HEADS_PER_GROUP, HEAD_DIM), lambda b: (b, 0, 0, 0))
    c_specs, views = [], []
    for (win, dil), c in zip(ATTN_GROUPS, caches):
        depth, nb, buf_len = c.shape[:3]
        assert buf_len == win and buf_len % dil == 0
        nk = buf_len // dil
        views.append(c.reshape(depth, nb, nk, dil, 2, HEADS_PER_GROUP, HEAD_DIM))
        c_specs.append(pl.BlockSpec((None, None, nk, None, 2, HEADS_PER_GROUP, HEAD_DIM),
                                    lambda b: (l, b, 0, 0, 0, 0, 0)))
    return pl.pallas_call(
        _attn_sample_body,
        grid=(n,),
        in_specs=[small, small, small] + c_specs,
        out_specs=pl.BlockSpec((None, HEADS_PER_GROUP, HEAD_DIM), lambda b: (b, 0, 0)),
        out_shape=jax.ShapeDtypeStruct((n, HEADS_PER_GROUP, HEAD_DIM), F32),
        compiler_params=_cparams(("arbitrary",), 32),
        name="attn_sample",
    )(q, kn, vn, *views)


def _merge_body(*refs, sample):
    if sample:
        (x_ref, mod_ref, g_ref, wg_ref, bg_ref, ys_ref, yc_ref, ya_ref,
         ws_ref, wc_ref, wa_ref, wo_ref, o_ref) = refs
    else:
        (x_ref, mod_ref, g_ref, wg_ref, bg_ref, ys_ref, yc_ref,
         *attn_refs, ws_ref, wc_ref, wa_ref, wo_ref, o_ref) = refs
    precise = sample
    d = D_MODEL
    x = x_ref[...]
    h = _norm_mod(x, g_ref[...], mod_ref[1], mod_ref[0])
    hm = h if precise else h.astype(BF16)
    if sample:
        ya = ya_ref[...]
    else:
        halves = []
        for hp in range(2):
            o0, l0, o1, l1, o2, l2 = [attn_refs[4 * g + s_ + hp][...] for g in range(3) for s_ in (0, 2)]
            top = jnp.maximum(jnp.maximum(l0, l1), l2)
            e0, e1, e2 = jnp.exp(l0 - top), jnp.exp(l1 - top), jnp.exp(l2 - top)
            halves.append((e0 * o0 + e1 * o1 + e2 * o2) / (e0 + e1 + e2))
        ya = jnp.concatenate(halves, axis=1)
    merged = None
    for bi, (y, w_ref) in enumerate(((ys_ref[...], ws_ref), (yc_ref[...], wc_ref), (ya, wa_ref))):
        gate = _sigmoid(_mm(hm, wg_ref[:, bi * d:(bi + 1) * d], precise) + bg_ref[:, bi * d:(bi + 1) * d])
        part = gate * _mm(y, w_ref[...], precise)
        merged = part if merged is None else merged + part
    o_ref[...] = x + mod_ref[2] * _mm(merged, wo_ref[...], precise)


def _merge_prompt(x, modp, l, gnorm, w_gate, b_gate, y_ssm, y_conv, attn, w_bs, w_bc, w_ba, w_out):
    b_sz, seq, d = x.shape
    tm = TM_MERGE
    gw = ATTN_OUT
    row = lambda w: pl.BlockSpec((None, tm, w), lambda b, i: (b, i, 0))
    lsel = lambda b, i: (l, 0, 0)
    wsp = lambda r, c: pl.BlockSpec((None, r, c), lsel)
    in_specs = [
        row(d),
        pl.BlockSpec((None, None, 6, 1, d), lambda b, i: (l, b, 0, 0, 0)),
        wsp(1, d), wsp(d, 3 * d), wsp(1, 3 * d),
        row(SSM_WIDTH), row(CONV_WIDTH),
    ] + [row(LANES)] * 12 + [wsp(SSM_WIDTH, d), wsp(CONV_WIDTH, d), wsp(ATTN_OUT, d), wsp(d, d)]
    flat = [t for group in attn for t in group]
    return pl.pallas_call(
        functools.partial(_merge_body, sample=False),
        grid=(b_sz, seq // tm),
        in_specs=in_specs,
        out_specs=row(d),
        out_shape=jax.ShapeDtypeStruct((b_sz, seq, d), F32),
        compiler_params=_cparams(("arbitrary", "arbitrary"), 56),
        name="merge_prompt",
    )(x, modp, gnorm, w_gate, b_gate, y_ssm, y_conv, *flat, w_bs, w_bc, w_ba, w_out)


def _merge_sample(x, mods, l, gnorm, w_gate, b_gate, y_ssm, y_conv, y_attn, w_bs, w_bc, w_ba, w_out):
    n, d = x.shape
    full = lambda *shape: pl.BlockSpec(shape, lambda i: (0,) * len(shape))
    lsel = lambda i: (l, 0, 0)
    wsp = lambda r, c: pl.BlockSpec((None, r, c), lsel)
    in_specs = [
        full(n, d),
        pl.BlockSpec((None, 6, n, d), lambda i: (l, 0, 0, 0)),
        wsp(1, d), wsp(d, 3 * d), wsp(1, 3 * d),
        full(n, SSM_WIDTH), full(n, CONV_WIDTH), full(n, ATTN_OUT),
        wsp(SSM_WIDTH, d), wsp(CONV_WIDTH, d), wsp(ATTN_OUT, d), wsp(d, d),
    ]
    return pl.pallas_call(
        functools.partial(_merge_body, sample=True),
        grid=(1,),
        in_specs=in_specs,
        out_specs=full(n, d),
        out_shape=jax.ShapeDtypeStruct((n, d), F32),
        compiler_params=_cparams(("arbitrary",), 56),
        name="merge_sample",
    )(x, mods, gnorm, w_gate, b_gate, y_ssm, y_conv, y_attn, w_bs, w_bc, w_ba, w_out)


def _ffn_prompt_body(x_ref, mod_ref, g_ref, wup_ref, cw_ref, cb_ref, wdn_ref, o_ref, tail_ref,
                     hist_ref, ua_ref, ub_ref):
    i = pl.program_id(1)
    rows = x_ref.shape[0]
    f2 = 2 * FFN_HIDDEN

    @pl.when(i == 0)
    def _():
        hist_ref[...] = jnp.zeros((HIST, f2), F32)

    x = x_ref[...]
    hb = _norm_mod(x, g_ref[...], mod_ref[4], mod_ref[3]).astype(BF16)
    acc = jnp.zeros((rows, D_MODEL), F32)
    cw = FFN_CHUNK
    for c in range(FFN_HIDDEN // cw):
        halves = []
        for base, u_ref in ((c * cw, ua_ref), (FFN_HIDDEN + c * cw, ub_ref)):
            cols = slice(base, base + cw)
            up = jnp.dot(hb, wup_ref[:, cols], preferred_element_type=F32)
            u_ref[0:HIST, :] = hist_ref[:, cols]
            u_ref[HIST:HIST + rows, :] = up
            hist_ref[:, cols] = up[rows - HIST:rows]
            halves.append(cw_ref[0:1, cols] * u_ref[HIST - 2:HIST - 2 + rows, :]
                          + cw_ref[1:2, cols] * u_ref[HIST - 1:HIST - 1 + rows, :]
                          + cw_ref[2:3, cols] * up + cb_ref[:, cols])
        act = _gelu_tanh(halves[0]) * halves[1]
        acc = acc + jnp.dot(act.astype(BF16), wdn_ref[c * cw:(c + 1) * cw, :], preferred_element_type=F32)
    o_ref[...] = x + mod_ref[5] * acc
    tail_ref[...] = hist_ref[...]


def _ffn_prompt(x, modp, l, gnorm, w_up, conv_w, conv_b, w_down):
    b_sz, seq, d = x.shape
    tm = TM_FFN
    f2 = 2 * FFN_HIDDEN
    lsel = lambda b, i: (l, 0, 0)
    once = pl.Buffered(1)
    return pl.pallas_call(
        _ffn_prompt_body,
        grid=(b_sz, seq // tm),
        in_specs=[
            pl.BlockSpec((None, tm, d), lambda b, i: (b, i, 0)),
            pl.BlockSpec((None, None, 6, 1, d), lambda b, i: (l, b, 0, 0, 0)),
            pl.BlockSpec((None, 1, d), lsel),
            pl.BlockSpec((None, d, f2), lsel, pipeline_mode=once),
            pl.BlockSpec((None, FFN_CONV_K, f2), lsel),
            pl.BlockSpec((None, 1, f2), lsel),
            pl.BlockSpec((None, FFN_HIDDEN, d), lsel, pipeline_mode=once),
        ],
        out_specs=[
            pl.BlockSpec((None, tm, d), lambda b, i: (b, i, 0)),
            pl.BlockSpec((None, HIST, f2), lambda b, i: (b, 0, 0)),
        ],
        out_shape=[jax.ShapeDtypeStruct((b_sz, seq, d), F32),
                   jax.ShapeDtypeStruct((b_sz, HIST, f2), F32)],
        scratch_shapes=[
            pltpu.VMEM((HIST, f2), F32),
            pltpu.VMEM((tm + HIST, FFN_CHUNK), F32),
            pltpu.VMEM((tm + HIST, FFN_CHUNK), F32),
        ],
        compiler_params=_cparams(("arbitrary", "arbitrary"), 56),
        name="ffn_prompt",
    )(x, modp, gnorm, w_up, conv_w, conv_b, w_down)


def _ffn_sample_body(x_ref, mod_ref, g_ref, wup_ref, c0_ref, c1_ref, cw_ref, cb_ref, wdn_ref, o_ref, up_ref):
    x = x_ref[...]
    h2 = _norm_mod(x, g_ref[...], mod_ref[4], mod_ref[3])
    up = _mm(h2, wup_ref[...], True)
    up_ref[...] = up
    cv = cw_ref[0:1, :] * c0_ref[...] + cw_ref[1:2, :] * c1_ref[...] + cw_ref[2:3, :] * up + cb_ref[...]
    act = _gelu_tanh(cv[:, 0:FFN_HIDDEN]) * cv[:, FFN_HIDDEN:2 * FFN_HIDDEN]
    o_ref[...] = x + mod_ref[5] * _mm(act, wdn_ref[...], True)


def _ffn_sample(x, mods, l, gnorm, w_up, c0, c1, conv_w, conv_b, w_down):
    n, d = x.shape
    f2 = 2 * FFN_HIDDEN
    full = lambda *shape: pl.BlockSpec(shape, lambda i: (0,) * len(shape))
    lsel = lambda i: (l, 0, 0)
    once = pl.Buffered(1)
    return pl.pallas_call(
        _ffn_sample_body,
        grid=(1,),
        in_specs=[
            full(n, d),
            pl.BlockSpec((None, 6, n, d), lambda i: (l, 0, 0, 0)),
            pl.BlockSpec((None, 1, d), lsel),
            pl.BlockSpec((None, d, f2), lsel, pipeline_mode=once),
            full(n, f2), full(n, f2),
            pl.BlockSpec((None, FFN_CONV_K, f2), lsel),
            pl.BlockSpec((None, 1, f2), lsel),
            pl.BlockSpec((None, FFN_HIDDEN, d), lsel, pipeline_mode=once),
        ],
        out_specs=[full(n, d), full(n, f2)],
        out_shape=[jax.ShapeDtypeStruct((n, d), F32), jax.ShapeDtypeStruct((n, f2), F32)],
        compiler_params=_cparams(("arbitrary",), 56),
        name="ffn_sample",
    )(x, mods, gnorm, w_up, c0, c1, conv_w, conv_b, w_down)


def _kv_roll_body(c0, c1, c2, n0, n1, n2, o0, o1, o2, sem):
    depth = c0.shape[0]
    copies = []
    for c, n, o in ((c0, n0, o0), (c1, n1, o1), (c2, n2, o2)):
        buf_len = c.shape[2]
        for l in range(depth):
            copies.append(pltpu.make_async_copy(c.at[l, :, pl.ds(1, buf_len - 1)],
                                                o.at[l, :, pl.ds(0, buf_len - 1)], sem.at[len(copies)]))
            copies.append(pltpu.make_async_copy(n.at[l], o.at[l, :, pl.ds(buf_len - 1, 1)],
                                                sem.at[len(copies)]))
    for cp in copies:
        cp.start()
    for cp in copies:
        cp.wait()


def _kv_roll(caches, news):
    depth = caches[0].shape[0]
    any_spec = pl.BlockSpec(memory_space=pl.ANY)
    return pl.pallas_call(
        _kv_roll_body,
        in_specs=[any_spec] * 6,
        out_specs=[any_spec] * 3,
        out_shape=[jax.ShapeDtypeStruct(c.shape, c.dtype) for c in caches],
        scratch_shapes=[pltpu.SemaphoreType.DMA((2 * depth * len(caches),))],
        name="kv_roll",
    )(*caches, *news)


def _rope_tables(pos):
    half = HEAD_DIM // 2
    inv = ROPE_THETA ** (-jnp.arange(half, dtype=F32) / half)
    ang = pos.astype(F32)[:, None] * inv[None, :]
    cos, sin = jnp.cos(ang), jnp.sin(ang)
    cos_h = jnp.concatenate([cos, cos], axis=1)
    sin_h = jnp.concatenate([-sin, sin], axis=1)
    reps = LANES // HEAD_DIM
    return jnp.tile(cos_h, (1, reps)), jnp.tile(sin_h, (1, reps))


def _residue_perm(rows, dil):
    i = jnp.arange(rows)
    src = (i % (rows // dil)) * dil + i // (rows // dil)
    return (src[:, None] == jnp.arange(rows)[None, :]).astype(BF16)


def _block_ones(dtype):
    i = jnp.arange(MXU_DIM) // HEAD_DIM
    return (i[:, None] == i[None, :]).astype(dtype)


def kernel(x_prompt, x_sample, state_ssm, cache_conv, cache_kv_w128, cache_kv_w512, cache_kv_w2048, cache_ffn, c_prompt, c_sample, w_ada, b_ada, g_norm_mix, w_in, ssm_a_re, ssm_a_im, ssm_log_dt, ssm_b_re, ssm_b_im, ssm_c_re, ssm_c_im, ssm_d, ssm_w_glu, ssm_b_glu, conv_w, conv_b, conv_ln_g, conv_ln_b, attn_gq, attn_gk, w_gate, b_gate, w_br_ssm, w_br_conv, w_br_attn, w_out, g_norm_ffn, ffn_w_up, ffn_conv_w, ffn_conv_b, ffn_w_down):
    bp, seq, d = x_prompt.shape
    ns = x_sample.shape[0]
    depth = w_ada.shape[0]
    f2 = 2 * FFN_HIDDEN
    assert x_sample.shape[1] == 1 and seq % (ATTN_BLOCK * ATTN_GROUPS[-1][1]) == 0
    caches = (cache_kv_w128, cache_kv_w512, cache_kv_w2048)

    pad = (-(bp + ns)) % SUBLANES
    c_all = jnp.concatenate([c_prompt, c_sample, jnp.zeros((pad, d), F32)], axis=0)
    mod = _ada_call(c_all, w_ada, b_ada)
    modp = mod[:, :bp].reshape(depth, bp, 6, 1, d)
    mods = mod[:, bp:bp + ns].reshape(depth, ns, 6, d).transpose(0, 2, 1, 3)

    bf = lambda w: w.astype(BF16)
    w_in_b, w_gate_b, w_out_b = bf(w_in), bf(w_gate), bf(w_out)
    w_bs_b, w_bc_b, w_ba_b = bf(w_br_ssm), bf(w_br_conv), bf(w_br_attn)
    w_up_b, w_dn_b, w_glu_b = bf(ffn_w_up), bf(ffn_w_down), bf(ssm_w_glu)

    row3 = lambda t: t.reshape(depth, 1, t.shape[-1])
    g_mix, g_ffn = row3(g_norm_mix), row3(g_norm_ffn)
    gq = row3(jnp.tile(attn_gq, (1, ATTN_HEADS)))
    gk = row3(jnp.tile(attn_gk, (1, ATTN_HEADS)))
    b_gate3, ssm_d3, b_glu3 = row3(b_gate), row3(ssm_d), row3(ssm_b_glu)
    conv_b3, ln_g3, ln_b3, ffn_cb3 = row3(conv_b), row3(conv_ln_g), row3(conv_ln_b), row3(ffn_conv_b)
    conv_w4 = conv_w.reshape(depth, CONV_K, 1, CONV_WIDTH)

    cos_p, sin_p = _rope_tables(jnp.arange(seq, dtype=jnp.int32))
    cos_s, sin_s = _rope_tables(jnp.full((ns,), PAST_LEN, dtype=jnp.int32))
    ones_b, ones_f = _block_ones(BF16), _block_ones(F32)
    p4 = _residue_perm(TM_INPROJ, ATTN_GROUPS[1][1])
    p16 = _residue_perm(TM_INPROJ, ATTN_GROUPS[2][1])
    seg = L_SSM // SUBLANES
    pm = _residue_perm(L_SSM, seg)
    pmt = pm.T
    zero_state = jnp.zeros((bp, 1, STATE_W), F32)

    yp = x_prompt
    ys = x_sample.reshape(ns, d)
    ssm_p, ssm_s, conv_p, conv_s, ffn_p, ffn_s = [], [], [], [], [], []
    kv_p = [[] for _ in ATTN_GROUPS]
    kv_new = [[] for _ in ATTN_GROUPS]
    gw = HEADS_PER_GROUP * HEAD_DIM
    for l in range(depth):
        tabs = _ssm_tables(ssm_a_re[l], ssm_a_im[l], ssm_log_dt[l], ssm_b_re[l], ssm_b_im[l],
                           ssm_c_re[l], ssm_c_im[l], seg)

        (zs, zc, k_tail, v_tail, q0, k0, v0, q1, k1, v1, q2, k2, v2) = _inproj_prompt(
            yp, modp, l, g_mix, w_in_b, gq, gk, cos_p, sin_p, ones_b, p4, p16)
        y_ssm, st = _ssm_prompt(zs, zero_state, l, pm, pmt, tabs, ssm_d3, w_glu_b, b_glu3)
        y_conv, a_tail = _conv_prompt(zc, l, conv_w, conv_b3, ln_g3, ln_b3)
        attn = [_attn_prompt(q0, k0, v0, 1), _attn_prompt(q1, k1, v1, ATTN_GROUPS[1][1]),
                _attn_prompt(q2, k2, v2, ATTN_GROUPS[2][1])]
        x_mid = _merge_prompt(yp, modp, l, g_mix, w_gate_b, b_gate3, y_ssm, y_conv, attn,
                              w_bs_b, w_bc_b, w_ba_b, w_out_b)
        yp, up_tail = _ffn_prompt(x_mid, modp, l, g_ffn, w_up_b, ffn_conv_w, ffn_cb3, w_dn_b)
        ssm_p.append(_slab_to_state(st.reshape(bp, STATE_W)))
        conv_p.append(a_tail[:, CONV_HALO - (CONV_K - 1):])
        ffn_p.append(up_tail[:, HIST - (FFN_CONV_K - 1):])
        for gi, (win, _) in enumerate(ATTN_GROUPS):
            keep = min(win, seq)
            cols = slice(gi * gw, (gi + 1) * gw)
            kk = k_tail[:, -keep:, cols].reshape(bp, keep, HEADS_PER_GROUP, HEAD_DIM)
            vv = v_tail[:, -keep:, cols].reshape(bp, keep, HEADS_PER_GROUP, HEAD_DIM)
            kv_p[gi].append(jnp.stack([kk, vv], axis=2))

        zs_s, zc_s, q_s, k_s, v_s = _inproj_sample(ys, mods, l, g_mix, w_in, gq, gk, cos_s, sin_s, ones_f)
        y_ssm_s, st_s = _ssm_sample(zs_s, _state_to_slab(state_ssm[l]), l, tabs, ssm_d3, ssm_w_glu, b_glu3)
        hist_t = cache_conv[l].transpose(1, 0, 2)
        y_conv_s, a_s = _conv_sample(zc_s, hist_t, l, conv_w4, conv_b3, ln_g3, ln_b3)
        heads = lambda t: t.reshape(ns, len(ATTN_GROUPS), HEADS_PER_GROUP, HEAD_DIM)
        y_attn_s = _attn_sample(heads(q_s), heads(k_s), heads(v_s), caches, l).reshape(ns, ATTN_OUT)
        x_mid_s = _merge_sample(ys, mods, l, g_mix, w_gate, b_gate3, y_ssm_s, y_conv_s, y_attn_s,
                                w_br_ssm, w_br_conv, w_br_attn, w_out)
        c_old, c_new = cache_ffn[l, :, 0], cache_ffn[l, :, 1]
        ys, up_s = _ffn_sample(x_mid_s, mods, l, g_ffn, ffn_w_up, c_old, c_new, ffn_conv_w, ffn_cb3, ffn_w_down)
        ssm_s.append(_slab_to_state(st_s))
        conv_s.append(jnp.concatenate([cache_conv[l][:, 1:], a_s[:, None, :]], axis=1))
        ffn_s.append(jnp.stack([c_new, up_s], axis=1))
        kh, vh = heads(k_s), heads(v_s)
        for gi in range(len(ATTN_GROUPS)):
            kv_new[gi].append(jnp.stack([kh[:, gi], vh[:, gi]], axis=1)[:, None])

    news = [jnp.stack(t, axis=0) for t in kv_new]
    kv_s = _kv_roll(caches, news)
    st = lambda t: jnp.stack(t, axis=0)
    return (yp, ys.reshape(ns, 1, d), st(ssm_p), st(ssm_s), st(conv_p), st(conv_s),
            st(kv_p[0]), kv_s[0], st(kv_p[1]), kv_s[1], st(kv_p[2]), kv_s[2], st(ffn_p), st(ffn_s))
```

```python
import functools
import math

import jax
import jax.numpy as jnp
from jax import lax
from jax.experimental import pallas as pl
from jax.experimental.pallas import tpu as pltpu

F32 = jnp.float32
BF16 = jnp.bfloat16

D_MODEL = 1024
SSM_WIDTH = 512
SSM_GROUP = 16
SSM_GROUPS = 32
SSM_N = 64
CONV_WIDTH = 512
CONV_K = 31
HEAD_DIM = 64
HEADS_PER_GROUP = 4
ATTN_GROUPS = ((128, 1), (512, 4), (2048, 16))
ATTN_HEADS = 12
ATTN_WIDTH = 768
ATTN_OUT = 256
ATTN_BLOCK = 128
ROPE_THETA = 10000.0
FFN_HIDDEN = 2816
FFN_CONV_K = 3
IN_WIDTH = SSM_WIDTH + 2 * CONV_WIDTH + 3 * ATTN_WIDTH
EPS = 1e-6
PAST_LEN = 8192

SUBLANES = 8
LANES = 128
MXU_DIM = 256
SLAB_GROUPS = 8
N_SLABS = SSM_GROUPS // SLAB_GROUPS
SLAB_CH = SLAB_GROUPS * SSM_GROUP
SLAB_ST = SLAB_GROUPS * SSM_N
STATE_W = 2 * SSM_GROUPS * SSM_N
NEG_BIG = -1e30
GELU_C = math.sqrt(2.0 / math.pi)

TM_INPROJ = 512
L_SSM = 256
T_CONV = 256
TM_MERGE = 256
TM_FFN = 256
FFN_CHUNK = 256
HIST = 8
CONV_HALO = 32
KV_TAIL = 2048


def _cparams(sem, vmem_mb):
    return pltpu.CompilerParams(dimension_semantics=sem, vmem_limit_bytes=vmem_mb * 1024 * 1024)


def _sigmoid(x):
    return 1.0 / (1.0 + jnp.exp(-x))


def _gelu_tanh(x):
    return 0.5 * x * (1.0 + jnp.tanh(GELU_C * (x + 0.044715 * (x * x * x))))


def _mm(a, w, precise):
    if precise:
        return jnp.dot(a, w, precision=lax.Precision.HIGHEST, preferred_element_type=F32)
    return jnp.dot(a.astype(BF16), w, preferred_element_type=F32)


def _norm_mod(x, g, scale, shift):
    ms = jnp.mean(x * x, axis=-1, keepdims=True)
    return (x * lax.rsqrt(ms + EPS) * g) * (1.0 + scale) + shift


def _ada_body(c_ref, w_ref, b_ref, o_ref):
    c = c_ref[...]
    s = c * _sigmoid(c)
    o_ref[...] = jnp.dot(s, w_ref[...], precision=lax.Precision.HIGHEST,
                         preferred_element_type=F32) + b_ref[...]


def _ada_call(c_all, w_ada, b_ada):
    depth, d, n6 = w_ada.shape
    rows = c_all.shape[0]
    tn = 1536
    return pl.pallas_call(
        _ada_body,
        grid=(depth, n6 // tn),
        in_specs=[
            pl.BlockSpec((rows, d), lambda l, j: (0, 0)),
            pl.BlockSpec((None, d, tn), lambda l, j: (l, 0, j)),
            pl.BlockSpec((None, 1, tn), lambda l, j: (l, 0, j)),
        ],
        out_specs=pl.BlockSpec((None, rows, tn), lambda l, j: (l, 0, j)),
        out_shape=jax.ShapeDtypeStruct((depth, rows, n6), F32),
        compiler_params=_cparams(("arbitrary", "arbitrary"), 40),
        name="ada_mod",
    )(c_all, w_ada, b_ada.reshape(depth, 1, n6))


def _head_norm_rope(z, gvec, ones, cos, sins, precise):
    sq = z * z
    parts = []
    for c in range(ATTN_WIDTH // MXU_DIM):
        parts.append(_mm(sq[:, c * MXU_DIM:(c + 1) * MXU_DIM], ones, precise))
    ms = jnp.concatenate(parts, axis=1) * (1.0 / HEAD_DIM)
    y = z * lax.rsqrt(ms + EPS) * gvec
    lane = lax.broadcasted_iota(jnp.int32, (1, LANES), 1)
    first = (lane % HEAD_DIM) < (HEAD_DIM // 2)
    outs = []
    for c in range(ATTN_WIDTH // LANES):
        yc = y[:, c * LANES:(c + 1) * LANES]
        partner = jnp.where(first, pltpu.roll(yc, LANES - HEAD_DIM // 2, 1),
                            pltpu.roll(yc, HEAD_DIM // 2, 1))
        outs.append(yc * cos + partner * sins)
    return jnp.concatenate(outs, axis=1)


def _inproj_body(*refs, sample):
    if sample:
        (x_ref, mod_ref, g_ref, w_ref, gq_ref, gk_ref, cos_ref, sin_ref, ones_ref,
         zs_ref, zc_ref, q_ref, k_ref, v_ref) = refs
    else:
        (x_ref, mod_ref, g_ref, w_ref, gq_ref, gk_ref, cos_ref, sin_ref, ones_ref, p4_ref, p16_ref,
         zs_ref, zc_ref, kt_ref, vt_ref,
         q0_ref, k0_ref, v0_ref, q1_ref, k1_ref, v1_ref, q2_ref, k2_ref, v2_ref) = refs
    precise = sample
    x = x_ref[...]
    h = _norm_mod(x, g_ref[...], mod_ref[1], mod_ref[0])
    hm = h if precise else h.astype(BF16)
    c1 = SSM_WIDTH
    c2 = c1 + 2 * CONV_WIDTH
    c3 = c2 + ATTN_WIDTH
    c4 = c3 + ATTN_WIDTH
    zs_ref[...] = _mm(hm, w_ref[:, 0:c1], precise)
    zc_ref[...] = _mm(hm, w_ref[:, c1:c2], precise)
    zq = _mm(hm, w_ref[:, c2:c3], precise)
    zk = _mm(hm, w_ref[:, c3:c4], precise)
    zv = _mm(hm, w_ref[:, c4:IN_WIDTH], precise)
    cos = cos_ref[...]
    sins = sin_ref[...]
    ones = ones_ref[...]
    q = _head_norm_rope(zq, gq_ref[...], ones, cos, sins, precise)
    k = _head_norm_rope(zk, gk_ref[...], ones, cos, sins, precise)
    if sample:
        q_ref[...] = q
        k_ref[...] = k
        v_ref[...] = zv
        return
    kt_ref[...] = k
    vt_ref[...] = zv
    qb = (q * (HEAD_DIM ** -0.5)).astype(BF16)
    kb = k.astype(BF16)
    vb = zv.astype(BF16)
    gw = HEADS_PER_GROUP * HEAD_DIM
    q0_ref[0] = qb[:, 0:gw]
    k0_ref[0] = kb[:, 0:gw]
    v0_ref[0] = vb[:, 0:gw]
    for gi, (p_ref, outs) in enumerate(((p4_ref, (q1_ref, k1_ref, v1_ref)),
                                        (p16_ref, (q2_ref, k2_ref, v2_ref))), start=1):
        pm = p_ref[...]
        for src, o_ref in zip((qb, kb, vb), outs):
            dil, rows = o_ref.shape[0], o_ref.shape[1]
            perm = jnp.dot(pm, src[:, gi * gw:(gi + 1) * gw], preferred_element_type=F32).astype(BF16)
            for r in range(dil):
                o_ref[r] = perm[r * rows:(r + 1) * rows]


def _inproj_prompt(x, modp, l, gnorm, w_in, gq, gk, cos, sins, ones, p4, p16):
    b_sz, seq, d = x.shape
    tm = TM_INPROJ
    nt = seq // tm
    tail = min(KV_TAIL, seq)
    ft = nt - tail // tm
    gw = HEADS_PER_GROUP * HEAD_DIM
    tmap = lambda b, i: (b, jnp.maximum(i - ft, 0), 0)
    in_specs = [
        pl.BlockSpec((None, tm, d), lambda b, i: (b, i, 0)),
        pl.BlockSpec((None, None, 6, 1, d), lambda b, i: (l, b, 0, 0, 0)),
        pl.BlockSpec((None, 1, d), lambda b, i: (l, 0, 0)),
        pl.BlockSpec((None, d, IN_WIDTH), lambda b, i: (l, 0, 0)),
        pl.BlockSpec((None, 1, ATTN_WIDTH), lambda b, i: (l, 0, 0)),
        pl.BlockSpec((None, 1, ATTN_WIDTH), lambda b, i: (l, 0, 0)),
        pl.BlockSpec((tm, LANES), lambda b, i: (i, 0)),
        pl.BlockSpec((tm, LANES), lambda b, i: (i, 0)),
        pl.BlockSpec((MXU_DIM, MXU_DIM), lambda b, i: (0, 0)),
        pl.BlockSpec((tm, tm), lambda b, i: (0, 0)),
        pl.BlockSpec((tm, tm), lambda b, i: (0, 0)),
    ]
    out_shape = [
        jax.ShapeDtypeStruct((b_sz, seq, SSM_WIDTH), F32),
        jax.ShapeDtypeStruct((b_sz, seq, 2 * CONV_WIDTH), F32),
        jax.ShapeDtypeStruct((b_sz, tail, ATTN_WIDTH), F32),
        jax.ShapeDtypeStruct((b_sz, tail, ATTN_WIDTH), F32),
    ]
    out_specs = [
        pl.BlockSpec((None, tm, SSM_WIDTH), lambda b, i: (b, i, 0)),
        pl.BlockSpec((None, tm, 2 * CONV_WIDTH), lambda b, i: (b, i, 0)),
        pl.BlockSpec((None, tm, ATTN_WIDTH), tmap),
        pl.BlockSpec((None, tm, ATTN_WIDTH), tmap),
    ]
    for _, dil in ATTN_GROUPS:
        for _ in range(3):
            out_shape.append(jax.ShapeDtypeStruct((b_sz, dil, seq // dil, gw), BF16))
            out_specs.append(pl.BlockSpec((None, dil, tm // dil, gw), lambda b, i: (b, 0, i, 0)))
    return pl.pallas_call(
        functools.partial(_inproj_body, sample=False),
        grid=(b_sz, nt),
        in_specs=in_specs,
        out_specs=out_specs,
        out_shape=out_shape,
        compiler_params=_cparams(("arbitrary", "arbitrary"), 56),
        name="inproj_prompt",
    )(x, modp, gnorm, w_in, gq, gk, cos, sins, ones, p4, p16)


def _inproj_sample(x, mods, l, gnorm, w_in, gq, gk, cos, sins, ones):
    n, d = x.shape
    lsel = lambda i: (l, 0, 0)
    in_specs = [
        pl.BlockSpec((n, d), lambda i: (0, 0)),
        pl.BlockSpec((None, 6, n, d), lambda i: (l, 0, 0, 0)),
        pl.BlockSpec((None, 1, d), lsel),
        pl.BlockSpec((None, d, IN_WIDTH), lsel),
        pl.BlockSpec((None, 1, ATTN_WIDTH), lsel),
        pl.BlockSpec((None, 1, ATTN_WIDTH), lsel),
        pl.BlockSpec((n, LANES), lambda i: (0, 0)),
        pl.BlockSpec((n, LANES), lambda i: (0, 0)),
        pl.BlockSpec((MXU_DIM, MXU_DIM), lambda i: (0, 0)),
    ]
    widths = (SSM_WIDTH, 2 * CONV_WIDTH, ATTN_WIDTH, ATTN_WIDTH, ATTN_WIDTH)
    return pl.pallas_call(
        functools.partial(_inproj_body, sample=True),
        grid=(1,),
        in_specs=in_specs,
        out_specs=[pl.BlockSpec((n, w), lambda i: (0, 0)) for w in widths],
        out_shape=[jax.ShapeDtypeStruct((n, w), F32) for w in widths],
        compiler_params=_cparams(("arbitrary",), 48),
        name="inproj_sample",
    )(x, mods, gnorm, w_in, gq, gk, cos, sins, ones)


def _ssm_tail(y_raw, u, d_ref, wglu_ref, bglu_ref, precise):
    y = _gelu_tanh(y_raw + d_ref[...] * u)
    return y * _sigmoid(_mm(y, wglu_ref[...], precise) + bglu_ref[...])


def _ssm_prompt_body(zs_ref, s0_ref, pm_ref, pmt_ref, b_ref, c_ref, lam_ref, lamp_ref, pw_ref,
                     d_ref, wglu_ref, bglu_ref, y_ref, st_ref,
                     v_ref, sb_ref, cs_ref, carry_ref):
    ci = pl.program_id(1)
    chunk = zs_ref.shape[0]
    seg = chunk // SUBLANES

    @pl.when(ci == 0)
    def _():
        carry_ref[...] = s0_ref[...]

    u = zs_ref[...]
    up = jnp.dot(pm_ref[...], u.astype(BF16), preferred_element_type=F32).astype(BF16)
    y_parts = []
    for m in range(N_SLABS):
        base = m * 2 * SLAB_ST
        v_ref[...] = jnp.dot(up[:, m * SLAB_CH:(m + 1) * SLAB_CH], b_ref[m], preferred_element_type=F32)
        lr = jnp.broadcast_to(lam_ref[m, 0:1, :], (SUBLANES, SLAB_ST))
        li = jnp.broadcast_to(lam_ref[m, 1:2, :], (SUBLANES, SLAB_ST))

        def scan_step(i, carry):
            sr, si = carry
            off = pl.multiple_of(i * SUBLANES, SUBLANES)
            vr = v_ref[pl.ds(off, SUBLANES), 0:SLAB_ST]
            vi = v_ref[pl.ds(off, SUBLANES), SLAB_ST:2 * SLAB_ST]
            nsr = lr * sr - li * si + vr
            nsi = lr * si + li * sr + vi
            v_ref[pl.ds(off, SUBLANES), 0:SLAB_ST] = nsr
            v_ref[pl.ds(off, SUBLANES), SLAB_ST:2 * SLAB_ST] = nsi
            return nsr, nsi

        zero = jnp.zeros((SUBLANES, SLAB_ST), F32)
        sr, si = lax.fori_loop(0, seg, scan_step, (zero, zero))
        er = carry_ref[0:1, base:base + SLAB_ST]
        ei = carry_ref[0:1, base + SLAB_ST:base + 2 * SLAB_ST]
        pr = lamp_ref[m, 0:1, :]
        pi = lamp_ref[m, 1:2, :]
        for r in range(SUBLANES):
            cs_ref[r:r + 1, 0:SLAB_ST] = er
            cs_ref[r:r + 1, SLAB_ST:2 * SLAB_ST] = ei
            ner = sr[r:r + 1] + pr * er - pi * ei
            nei = si[r:r + 1] + pr * ei + pi * er
            er, ei = ner, nei
        carry_ref[0:1, base:base + SLAB_ST] = er
        carry_ref[0:1, base + SLAB_ST:base + 2 * SLAB_ST] = ei
        csr = cs_ref[:, 0:SLAB_ST]
        csi = cs_ref[:, SLAB_ST:2 * SLAB_ST]

        def fix_step(i2, _):
            off = pl.multiple_of(i2 * 2 * SUBLANES, 2 * SUBLANES)
            rows_r, rows_i = [], []
            for h in range(2):
                i = i2 * 2 + h
                qr = jnp.broadcast_to(pw_ref[m, 0, pl.ds(i, 1), :], (SUBLANES, SLAB_ST))
                qi = jnp.broadcast_to(pw_ref[m, 1, pl.ds(i, 1), :], (SUBLANES, SLAB_ST))
                o8 = pl.multiple_of(off + h * SUBLANES, SUBLANES)
                rows_r.append(v_ref[pl.ds(o8, SUBLANES), 0:SLAB_ST] + qr * csr - qi * csi)
                rows_i.append(v_ref[pl.ds(o8, SUBLANES), SLAB_ST:2 * SLAB_ST] + qr * csi + qi * csr)
            sb_ref[pl.ds(off, 2 * SUBLANES), 0:SLAB_ST] = jnp.concatenate(rows_r, axis=0).astype(BF16)
            sb_ref[pl.ds(off, 2 * SUBLANES), SLAB_ST:2 * SLAB_ST] = jnp.concatenate(rows_i, axis=0).astype(BF16)
            return 0

        lax.fori_loop(0, seg // 2, fix_step, 0)
        y_parts.append(jnp.dot(sb_ref[...], c_ref[m], preferred_element_type=F32))
    y_perm = jnp.concatenate(y_parts, axis=1)
    hi = y_perm.astype(BF16)
    lo = (y_perm - hi.astype(F32)).astype(BF16)
    pmt = pmt_ref[...]
    y_nat = jnp.dot(pmt, hi, preferred_element_type=F32) + jnp.dot(pmt, lo, preferred_element_type=F32)
    y_ref[...] = _ssm_tail(y_nat, u, d_ref, wglu_ref, bglu_ref, False).astype(BF16)
    st_ref[...] = carry_ref[...]


def _ssm_prompt(zs, s0, l, pm, pmt, tabs, ssm_d, w_glu, b_glu):
    b_sz, seq, _ = zs.shape
    chunk = L_SSM
    seg = chunk // SUBLANES
    full = lambda *shape: pl.BlockSpec(shape, lambda b, c: (0,) * len(shape))
    lsel = lambda b, c: (l, 0, 0)
    return pl.pallas_call(
        _ssm_prompt_body,
        grid=(b_sz, seq // chunk),
        in_specs=[
            pl.BlockSpec((None, chunk, SSM_WIDTH), lambda b, c: (b, c, 0)),
            pl.BlockSpec((None, 1, STATE_W), lambda b, c: (b, 0, 0)),
            full(chunk, chunk), full(chunk, chunk),
            full(N_SLABS, SLAB_CH, 2 * SLAB_ST),
            full(N_SLABS, 2 * SLAB_ST, SLAB_CH),
            full(N_SLABS, 2, SLAB_ST), full(N_SLABS, 2, SLAB_ST),
            full(N_SLABS, 2, seg, SLAB_ST),
            pl.BlockSpec((None, 1, SSM_WIDTH), lsel),
            pl.BlockSpec((None, SSM_WIDTH, SSM_WIDTH), lsel),
            pl.BlockSpec((None, 1, SSM_WIDTH), lsel),
        ],
        out_specs=[
            pl.BlockSpec((None, chunk, SSM_WIDTH), lambda b, c: (b, c, 0)),
            pl.BlockSpec((None, 1, STATE_W), lambda b, c: (b, 0, 0)),
        ],
        out_shape=[jax.ShapeDtypeStruct((b_sz, seq, SSM_WIDTH), BF16),
                   jax.ShapeDtypeStruct((b_sz, 1, STATE_W), F32)],
        scratch_shapes=[
            pltpu.VMEM((chunk, 2 * SLAB_ST), F32),
            pltpu.VMEM((chunk, 2 * SLAB_ST), BF16),
            pltpu.VMEM((SUBLANES, 2 * SLAB_ST), F32),
            pltpu.VMEM((1, STATE_W), F32),
        ],
        compiler_params=_cparams(("arbitrary", "arbitrary"), 40),
        name="ssm_prompt",
    )(zs, s0, pm, pmt, tabs["b_bf"], tabs["c_bf"], tabs["lam"], tabs["lam_seg"], tabs["pw"],
      ssm_d, w_glu, b_glu)


def _ssm_sample_body(u_ref, s0_ref, b_ref, c_ref, lam_ref, d_ref, wglu_ref, bglu_ref, y_ref, st_ref):
    u = u_ref[...]
    y_parts = []
    for m in range(N_SLABS):
        base = m * 2 * SLAB_ST
        v = _mm(u[:, m * SLAB_CH:(m + 1) * SLAB_CH], b_ref[m], True)
        lr = lam_ref[m, 0:1, :]
        li = lam_ref[m, 1:2, :]
        sr0 = s0_ref[:, base:base + SLAB_ST]
        si0 = s0_ref[:, base + SLAB_ST:base + 2 * SLAB_ST]
        sr = lr * sr0 - li * si0 + v[:, 0:SLAB_ST]
        si = lr * si0 + li * sr0 + v[:, SLAB_ST:2 * SLAB_ST]
        st_ref[:, base:base + SLAB_ST] = sr
        st_ref[:, base + SLAB_ST:base + 2 * SLAB_ST] = si
        y_parts.append(_mm(jnp.concatenate([sr, si], axis=1), c_ref[m], True))
    y_raw = jnp.concatenate(y_parts, axis=1)
    y_ref[...] = _ssm_tail(y_raw, u, d_ref, wglu_ref, bglu_ref, True)


def _ssm_sample(zs, s0, l, tabs, ssm_d, w_glu, b_glu):
    n = zs.shape[0]
    full = lambda *shape: pl.BlockSpec(shape, lambda i: (0,) * len(shape))
    lsel = lambda i: (l, 0, 0)
    return pl.pallas_call(
        _ssm_sample_body,
        grid=(1,),
        in_specs=[
            full(n, SSM_WIDTH), full(n, STATE_W),
            full(N_SLABS, SLAB_CH, 2 * SLAB_ST), full(N_SLABS, 2 * SLAB_ST, SLAB_CH),
            full(N_SLABS, 2, SLAB_ST),
            pl.BlockSpec((None, 1, SSM_WIDTH), lsel),
            pl.BlockSpec((None, SSM_WIDTH, SSM_WIDTH), lsel),
            pl.BlockSpec((None, 1, SSM_WIDTH), lsel),
        ],
        out_specs=[full(n, SSM_WIDTH), full(n, STATE_W)],
        out_shape=[jax.ShapeDtypeStruct((n, SSM_WIDTH), F32), jax.ShapeDtypeStruct((n, STATE_W), F32)],
        compiler_params=_cparams(("arbitrary",), 32),
        name="ssm_sample",
    )(zs, s0, tabs["b_f32"], tabs["c_f32"], tabs["lam"], ssm_d, w_glu, b_glu)


def _ssm_tables(a_re, a_im, log_dt, b_re, b_im, c_re, c_im, seg):
    dt = jnp.exp(log_dt)[:, None]
    xr, xi = a_re * dt, a_im * dt
    mag = jnp.exp(xr)
    lr, li = mag * jnp.cos(xi), mag * jnp.sin(xi)
    den = a_re * a_re + a_im * a_im
    nr, ni = lr - 1.0, li
    cr = (nr * a_re + ni * a_im) / den
    cim = (ni * a_re - nr * a_im) / den
    bbr = cr[..., None] * b_re - cim[..., None] * b_im
    bbi = cr[..., None] * b_im + cim[..., None] * b_re
    eye = jnp.eye(SLAB_GROUPS, dtype=F32)

    def b_slab(t):
        t = t.reshape(N_SLABS, SLAB_GROUPS, SSM_N, SSM_GROUP)
        return jnp.einsum('mgnc,gh->mgchn', t, eye).reshape(N_SLABS, SLAB_CH, SLAB_ST)

    def c_slab(t):
        t = t.reshape(N_SLABS, SLAB_GROUPS, SSM_GROUP, SSM_N)
        return jnp.einsum('mgcn,gh->mgnhc', t, eye).reshape(N_SLABS, SLAB_ST, SLAB_CH)

    b_mat = jnp.concatenate([b_slab(bbr), b_slab(bbi)], axis=2)
    c_mat = jnp.concatenate([c_slab(c_re), c_slab(-c_im)], axis=1)

    def power(k):
        mk = jnp.exp(k * xr)
        return jnp.stack([(mk * jnp.cos(k * xi)).reshape(N_SLABS, SLAB_ST),
                          (mk * jnp.sin(k * xi)).reshape(N_SLABS, SLAB_ST)], axis=1)

    pw = jnp.stack([power(float(k)) for k in range(1, seg + 1)], axis=2)
    return dict(b_f32=b_mat, c_f32=c_mat, b_bf=b_mat.astype(BF16), c_bf=c_mat.astype(BF16),
                lam=power(1.0), lam_seg=power(float(seg)), pw=pw)


def _state_to_slab(s):
    b = s.shape[0]
    return s.reshape(b, N_SLABS, SLAB_GROUPS, SSM_N, 2).transpose(0, 1, 4, 2, 3).reshape(b, STATE_W)


def _slab_to_state(x):
    b = x.shape[0]
    return x.reshape(b, N_SLABS, 2, SLAB_GROUPS, SSM_N).transpose(0, 1, 3, 4, 2).reshape(b, SSM_GROUPS, SSM_N, 2)


def _ln_silu(y, g, b):
    mu = jnp.mean(y, axis=-1, keepdims=True)
    var = jnp.mean(jnp.square(y - mu), axis=-1, keepdims=True)
    t = (y - mu) * lax.rsqrt(var + EPS) * g + b
    return t * _sigmoid(t)


def _conv_prompt_body(zc_ref, w_ref, b_ref, lg_ref, lb_ref, y_ref, tail_ref, abuf_ref):
    i = pl.program_id(1)
    rows = zc_ref.shape[0]

    @pl.when(i == 0)
    def _():
        abuf_ref[0:CONV_HALO, :] = jnp.zeros((CONV_HALO, CONV_WIDTH), F32)

    z = zc_ref[...]
    a = z[:, 0:CONV_WIDTH] * _sigmoid(z[:, CONV_WIDTH:2 * CONV_WIDTH])
    abuf_ref[CONV_HALO:CONV_HALO + rows, :] = a
    acc = jnp.zeros((rows, CONV_WIDTH), F32) + b_ref[...]
    first = CONV_HALO - (CONV_K - 1)
    for j in range(CONV_K):
        acc = acc + w_ref[j:j + 1, :] * abuf_ref[first + j:first + j + rows, :]
    y_ref[...] = _ln_silu(acc, lg_ref[...], lb_ref[...]).astype(BF16)
    last = a[rows - CONV_HALO:rows]
    tail_ref[...] = last
    abuf_ref[0:CONV_HALO, :] = last


def _conv_prompt(zc, l, conv_w, conv_b, ln_g, ln_b):
    b_sz, seq, _ = zc.shape
    rows = T_CONV
    lsel = lambda b, i: (l, 0, 0)
    return pl.pallas_call(
        _conv_prompt_body,
        grid=(b_sz, seq // rows),
        in_specs=[
            pl.BlockSpec((None, rows, 2 * CONV_WIDTH), lambda b, i: (b, i, 0)),
            pl.BlockSpec((None, CONV_K, CONV_WIDTH), lsel),
            pl.BlockSpec((None, 1, CONV_WIDTH), lsel),
            pl.BlockSpec((None, 1, CONV_WIDTH), lsel),
            pl.BlockSpec((None, 1, CONV_WIDTH), lsel),
        ],
        out_specs=[
            pl.BlockSpec((None, rows, CONV_WIDTH), lambda b, i: (b, i, 0)),
            pl.BlockSpec((None, CONV_HALO, CONV_WIDTH), lambda b, i: (b, 0, 0)),
        ],
        out_shape=[jax.ShapeDtypeStruct((b_sz, seq, CONV_WIDTH), BF16),
                   jax.ShapeDtypeStruct((b_sz, CONV_HALO, CONV_WIDTH), F32)],
        scratch_shapes=[pltpu.VMEM((rows + CONV_HALO, CONV_WIDTH), F32)],
        compiler_params=_cparams(("arbitrary", "arbitrary"), 32),
        name="conv_prompt",
    )(zc, conv_w, conv_b, ln_g, ln_b)


def _conv_sample_body(zc_ref, hist_ref, w_ref, b_ref, lg_ref, lb_ref, y_ref, a_ref):
    z = zc_ref[...]
    a = z[:, 0:CONV_WIDTH] * _sigmoid(z[:, CONV_WIDTH:2 * CONV_WIDTH])
    a_ref[...] = a
    acc = b_ref[...] + w_ref[CONV_K - 1] * a
    for j in range(CONV_K - 1):
        acc = acc + w_ref[j] * hist_ref[j]
    y_ref[...] = _ln_silu(acc, lg_ref[...], lb_ref[...])


def _conv_sample(zc, hist_t, l, conv_w4, conv_b, ln_g, ln_b):
    n = zc.shape[0]
    full = lambda *shape: pl.BlockSpec(shape, lambda i: (0,) * len(shape))
    lsel = lambda i: (l, 0, 0)
    return pl.pallas_call(
        _conv_sample_body,
        grid=(1,),
        in_specs=[
            full(n, 2 * CONV_WIDTH), full(CONV_K - 1, n, CONV_WIDTH),
            pl.BlockSpec((None, CONV_K, 1, CONV_WIDTH), lambda i: (l, 0, 0, 0)),
            pl.BlockSpec((None, 1, CONV_WIDTH), lsel),
            pl.BlockSpec((None, 1, CONV_WIDTH), lsel),
            pl.BlockSpec((None, 1, CONV_WIDTH), lsel),
        ],
        out_specs=[full(n, CONV_WIDTH), full(n, CONV_WIDTH)],
        out_shape=[jax.ShapeDtypeStruct((n, CONV_WIDTH), F32)] * 2,
        compiler_params=_cparams(("arbitrary",), 32),
        name="conv_sample",
    )(zc, hist_t, conv_w4, conv_b, ln_g, ln_b)


def _attn_prompt_body(q_ref, kc_ref, kp_ref, vc_ref, vp_ref, oa_ref, ob_ref, la_ref, lb_ref, *, dil):
    j = pl.program_id(1)
    r = pl.program_id(2)
    blk = ATTN_BLOCK
    q = q_ref[...]
    k = jnp.concatenate([kp_ref[...], kc_ref[...]], axis=0)
    v = jnp.concatenate([vp_ref[...], vc_ref[...]], axis=0)
    qi = lax.broadcasted_iota(jnp.int32, (blk, 2 * blk), 0)
    ki = lax.broadcasted_iota(jnp.int32, (blk, 2 * blk), 1)
    dist = qi + blk - ki
    band = jnp.where(dist >= 0, jnp.where(dist <= blk, 1, 0), 0)
    first_ok = jnp.where(ki >= blk, 1, jnp.where(j > 0, 1, 0))
    valid = (band * first_ok) > 0
    lane = lax.broadcasted_iota(jnp.int32, (1, LANES), 1)
    lo = lane < HEAD_DIM
    o_parts, lse_parts = [], []
    for hp in range(HEADS_PER_GROUP // 2):
        sl = slice(hp * LANES, (hp + 1) * LANES)
        q2, k2, v2 = q[:, sl], k[:, sl], v[:, sl]
        res = []
        for half in range(2):
            keep = lo if half == 0 else jnp.logical_not(lo)
            qm = jnp.where(keep, q2, jnp.zeros_like(q2))
            s = lax.dot_general(qm, k2, (((1,), (1,)), ((), ())), preferred_element_type=F32)
            s = jnp.where(valid, s, NEG_BIG)
            m = jnp.max(s, axis=-1, keepdims=True)
            p = jnp.exp(s - m)
            den = jnp.sum(p, axis=-1, keepdims=True)
            o = jnp.dot(p.astype(BF16), v2, preferred_element_type=F32)
            res.append((o / den, m + jnp.log(den)))
        o_parts.append(jnp.where(lo, res[0][0], res[1][0]))
        lse_parts.append(jnp.where(lo, res[0][1], res[1][1]))
    for val, ref in zip(o_parts + lse_parts, (oa_ref, ob_ref, la_ref, lb_ref)):
        if dil == 1:
            ref[...] = val
        else:
            ref[pl.ds(r, blk, stride=dil), :] = val


def _attn_prompt(q, k, v, dil):
    b_sz, _, m_len, gw = q.shape
    blk = ATTN_BLOCK
    seq = m_len * dil
    cur = lambda b, j, r: (b, r, j, 0)
    prev = lambda b, j, r: (b, r, jnp.maximum(j - 1, 0), 0)
    bs = lambda f: pl.BlockSpec((None, None, blk, gw), f)
    span = blk * dil
    return pl.pallas_call(
        functools.partial(_attn_prompt_body, dil=dil),
        grid=(b_sz, m_len // blk, dil),
        in_specs=[bs(cur), bs(cur), bs(prev), bs(cur), bs(prev)],
        out_specs=[pl.BlockSpec((None, span, LANES), lambda b, j, r: (b, j, 0))] * 4,
        out_shape=[jax.ShapeDtypeStruct((b_sz, seq, LANES), F32)] * 4,
        compiler_params=_cparams(("arbitrary", "arbitrary", "arbitrary"), 40),
        name=f"attn_prompt_d{dil}",
    )(q, k, k, v, v)


def _attn_roll_body(*refs):
    ng = len(ATTN_GROUPS)
    q_ref, kn_ref, vn_ref = refs[:3]
    c_refs = refs[3:3 + ng]
    y_ref = refs[-1 - ng]
    o_refs = refs[-ng:]
    scale = HEAD_DIM ** -0.5
    outs = [[None] * ng for _ in range(HEADS_PER_GROUP)]
    lses = [[None] * ng for _ in range(HEADS_PER_GROUP)]
    for g, ((_, dil), c_ref, o_ref) in enumerate(zip(ATTN_GROUPS, c_refs, o_refs)):
        buf_len = c_ref.shape[-1]
        lane = lax.broadcasted_iota(jnp.int32, (1, buf_len), 1)
        used = (lane % dil) == 0
        last = lane == buf_len - 1
        for h in range(HEADS_PER_GROUP):
            kk = c_ref[0, h]
            vv = c_ref[1, h]
            qc, knc, vnc = q_ref[g, h], kn_ref[g, h], vn_ref[g, h]
            s = jnp.where(used, jnp.sum(kk * qc, axis=0, keepdims=True) * scale, NEG_BIG)
            sn = jnp.sum(knc * qc, axis=0, keepdims=True) * scale
            m = jnp.maximum(jnp.max(s, axis=1, keepdims=True), sn)
            p = jnp.exp(s - m)
            pn = jnp.exp(sn - m)
            den = jnp.sum(p, axis=1, keepdims=True) + pn
            outs[h][g] = (jnp.sum(vv * p, axis=1, keepdims=True) + pn * vnc) / den
            lses[h][g] = m + jnp.log(den)
            o_ref[0, h] = jnp.where(last, knc, pltpu.roll(kk, buf_len - 1, 1))
            o_ref[1, h] = jnp.where(last, vnc, pltpu.roll(vv, buf_len - 1, 1))
    for h in range(HEADS_PER_GROUP):
        top = functools.reduce(jnp.maximum, lses[h])
        ws = [jnp.exp(t - top) for t in lses[h]]
        y_ref[h] = sum(w * o for w, o in zip(ws, outs[h])) / sum(ws)


def _attn_roll(q, kn, vn, caches_t, prev, l):
    n = q.shape[0]
    ng = len(ATTN_GROUPS)
    small = pl.BlockSpec((None, ng, HEADS_PER_GROUP, HEAD_DIM, 1), lambda b: (b, 0, 0, 0, 0))
    c_specs = []
    for (win, dil), c in zip(ATTN_GROUPS, caches_t):
        assert c.shape[-1] == win and win % dil == 0
        c_specs.append(pl.BlockSpec((None, None) + c.shape[2:], lambda b: (l, b, 0, 0, 0, 0)))
    in_specs = [small, small, small] + c_specs
    args = [q, kn, vn, *caches_t]
    aliases = {}
    if prev is not None:
        in_specs += [pl.BlockSpec(memory_space=pl.ANY)] * ng
        aliases = {len(args) + g: 1 + g for g in range(ng)}
        args += list(prev)
    return pl.pallas_call(
        _attn_roll_body,
        grid=(n,),
        in_specs=in_specs,
        out_specs=[pl.BlockSpec((None, HEADS_PER_GROUP, HEAD_DIM, 1), lambda b: (b, 0, 0, 0))] + c_specs,
        out_shape=[jax.ShapeDtypeStruct((n, HEADS_PER_GROUP, HEAD_DIM, 1), F32)]
        + [jax.ShapeDtypeStruct(c.shape, c.dtype) for c in caches_t],
        input_output_aliases=aliases,
        compiler_params=_cparams(("arbitrary",), 48),
        name="attn_roll_sample",
    )(*args)


def _merge_body(*refs, sample):
    if sample:
        (x_ref, mod_ref, g_ref, wg_ref, bg_ref, ys_ref, yc_ref, ya_ref,
         ws_ref, wc_ref, wa_ref, wo_ref, o_ref) = refs
    else:
        (x_ref, mod_ref, g_ref, wg_ref, bg_ref, ys_ref, yc_ref,
         *attn_refs, ws_ref, wc_ref, wa_ref, wo_ref, o_ref) = refs
    precise = sample
    d = D_MODEL
    x = x_ref[...]
    h = _norm_mod(x, g_ref[...], mod_ref[1], mod_ref[0])
    hm = h if precise else h.astype(BF16)
    if sample:
        ya = ya_ref[...]
    else:
        halves = []
        for hp in range(2):
            o0, l0, o1, l1, o2, l2 = [attn_refs[4 * g + s_ + hp][...] for g in range(3) for s_ in (0, 2)]
            top = jnp.maximum(jnp.maximum(l0, l1), l2)
            e0, e1, e2 = jnp.exp(l0 - top), jnp.exp(l1 - top), jnp.exp(l2 - top)
            halves.append((e0 * o0 + e1 * o1 + e2 * o2) / (e0 + e1 + e2))
        ya = jnp.concatenate(halves, axis=1)
    merged = None
    for bi, (y, w_ref) in enumerate(((ys_ref[...], ws_ref), (yc_ref[...], wc_ref), (ya, wa_ref))):
        gate = _sigmoid(_mm(hm, wg_ref[:, bi * d:(bi + 1) * d], precise) + bg_ref[:, bi * d:(bi + 1) * d])
        part = gate * _mm(y, w_ref[...], precise)
        merged = part if merged is None else merged + part
    o_ref[...] = x + mod_ref[2] * _mm(merged, wo_ref[...], precise)


def _merge_prompt(x, modp, l, gnorm, w_gate, b_gate, y_ssm, y_conv, attn, w_bs, w_bc, w_ba, w_out):
    b_sz, seq, d = x.shape
    tm = TM_MERGE
    gw = ATTN_OUT
    row = lambda w: pl.BlockSpec((None, tm, w), lambda b, i: (b, i, 0))
    lsel = lambda b, i: (l, 0, 0)
    wsp = lambda r, c: pl.BlockSpec((None, r, c), lsel)
    in_specs = [
        row(d),
        pl.BlockSpec((None, None, 6, 1, d), lambda b, i: (l, b, 0, 0, 0)),
        wsp(1, d), wsp(d, 3 * d), wsp(1, 3 * d),
        row(SSM_WIDTH), row(CONV_WIDTH),
    ] + [row(LANES)] * 12 + [wsp(SSM_WIDTH, d), wsp(CONV_WIDTH, d), wsp(ATTN_OUT, d), wsp(d, d)]
    flat = [t for group in attn for t in group]
    return pl.pallas_call(
        functools.partial(_merge_body, sample=False),
        grid=(b_sz, seq // tm),
        in_specs=in_specs,
        out_specs=row(d),
        out_shape=jax.ShapeDtypeStruct((b_sz, seq, d), F32),
        compiler_params=_cparams(("arbitrary", "arbitrary"), 56),
        name="merge_prompt",
    )(x, modp, gnorm, w_gate, b_gate, y_ssm, y_conv, *flat, w_bs, w_bc, w_ba, w_out)


def _merge_sample(x, mods, l, gnorm, w_gate, b_gate, y_ssm, y_conv, y_attn, w_bs, w_bc, w_ba, w_out):
    n, d = x.shape
    full = lambda *shape: pl.BlockSpec(shape, lambda i: (0,) * len(shape))
    lsel = lambda i: (l, 0, 0)
    wsp = lambda r, c: pl.BlockSpec((None, r, c), lsel)
    in_specs = [
        full(n, d),
        pl.BlockSpec((None, 6, n, d), lambda i: (l, 0, 0, 0)),
        wsp(1, d), wsp(d, 3 * d), wsp(1, 3 * d),
        full(n, SSM_WIDTH), full(n, CONV_WIDTH), full(n, ATTN_OUT),
        wsp(SSM_WIDTH, d), wsp(CONV_WIDTH, d), wsp(ATTN_OUT, d), wsp(d, d),
    ]
    return pl.pallas_call(
        functools.partial(_merge_body, sample=True),
        grid=(1,),
        in_specs=in_specs,
        out_specs=full(n, d),
        out_shape=jax.ShapeDtypeStruct((n, d), F32),
        compiler_params=_cparams(("arbitrary",), 56),
        name="merge_sample",
    )(x, mods, gnorm, w_gate, b_gate, y_ssm, y_conv, y_attn, w_bs, w_bc, w_ba, w_out)


def _ffn_prompt_body(x_ref, mod_ref, g_ref, wup_ref, cw_ref, cb_ref, wdn_ref, o_ref, tail_ref,
                     hist_ref, ua_ref, ub_ref):
    i = pl.program_id(1)
    rows = x_ref.shape[0]
    f2 = 2 * FFN_HIDDEN

    @pl.when(i == 0)
    def _():
        hist_ref[...] = jnp.zeros((HIST, f2), F32)

    x = x_ref[...]
    hb = _norm_mod(x, g_ref[...], mod_ref[4], mod_ref[3]).astype(BF16)
    acc = jnp.zeros((rows, D_MODEL), F32)
    cw = FFN_CHUNK
    for c in range(FFN_HIDDEN // cw):
        halves = []
        for base, u_ref in ((c * cw, ua_ref), (FFN_HIDDEN + c * cw, ub_ref)):
            cols = slice(base, base + cw)
            up = jnp.dot(hb, wup_ref[:, cols], preferred_element_type=F32)
            u_ref[0:HIST, :] = hist_ref[:, cols]
            u_ref[HIST:HIST + rows, :] = up
            hist_ref[:, cols] = up[rows - HIST:rows]
            halves.append(cw_ref[0:1, cols] * u_ref[HIST - 2:HIST - 2 + rows, :]
                          + cw_ref[1:2, cols] * u_ref[HIST - 1:HIST - 1 + rows, :]
                          + cw_ref[2:3, cols] * up + cb_ref[:, cols])
        act = _gelu_tanh(halves[0]) * halves[1]
        acc = acc + jnp.dot(act.astype(BF16), wdn_ref[c * cw:(c + 1) * cw, :], preferred_element_type=F32)
    o_ref[...] = x + mod_ref[5] * acc
    tail_ref[...] = hist_ref[...]


def _ffn_prompt(x, modp, l, gnorm, w_up, conv_w, conv_b, w_down):
    b_sz, seq, d = x.shape
    tm = TM_FFN
    f2 = 2 * FFN_HIDDEN
    lsel = lambda b, i: (l, 0, 0)
    once = pl.Buffered(1)
    return pl.pallas_call(
        _ffn_prompt_body,
        grid=(b_sz, seq // tm),
        in_specs=[
            pl.BlockSpec((None, tm, d), lambda b, i: (b, i, 0)),
            pl.BlockSpec((None, None, 6, 1, d), lambda b, i: (l, b, 0, 0, 0)),
            pl.BlockSpec((None, 1, d), lsel),
            pl.BlockSpec((None, d, f2), lsel, pipeline_mode=once),
            pl.BlockSpec((None, FFN_CONV_K, f2), lsel),
            pl.BlockSpec((None, 1, f2), lsel),
            pl.BlockSpec((None, FFN_HIDDEN, d), lsel, pipeline_mode=once),
        ],
        out_specs=[
            pl.BlockSpec((None, tm, d), lambda b, i: (b, i, 0)),
            pl.BlockSpec((None, HIST, f2), lambda b, i: (b, 0, 0)),
        ],
        out_shape=[jax.ShapeDtypeStruct((b_sz, seq, d), F32),
                   jax.ShapeDtypeStruct((b_sz, HIST, f2), F32)],
        scratch_shapes=[
            pltpu.VMEM((HIST, f2), F32),
            pltpu.VMEM((tm + HIST, FFN_CHUNK), F32),
            pltpu.VMEM((tm + HIST, FFN_CHUNK), F32),
        ],
        compiler_params=_cparams(("arbitrary", "arbitrary"), 56),
        name="ffn_prompt",
    )(x, modp, gnorm, w_up, conv_w, conv_b, w_down)


def _ffn_sample_body(x_ref, mod_ref, g_ref, wup_ref, c0_ref, c1_ref, cw_ref, cb_ref, wdn_ref, o_ref, up_ref):
    x = x_ref[...]
    h2 = _norm_mod(x, g_ref[...], mod_ref[4], mod_ref[3])
    up = _mm(h2, wup_ref[...], True)
    up_ref[...] = up
    cv = cw_ref[0:1, :] * c0_ref[...] + cw_ref[1:2, :] * c1_ref[...] + cw_ref[2:3, :] * up + cb_ref[...]
    act = _gelu_tanh(cv[:, 0:FFN_HIDDEN]) * cv[:, FFN_HIDDEN:2 * FFN_HIDDEN]
    o_ref[...] = x + mod_ref[5] * _mm(act, wdn_ref[...], True)


def _ffn_sample(x, mods, l, gnorm, w_up, c0, c1, conv_w, conv_b, w_down):
    n, d = x.shape
    f2 = 2 * FFN_HIDDEN
    full = lambda *shape: pl.BlockSpec(shape, lambda i: (0,) * len(shape))
    lsel = lambda i: (l, 0, 0)
    once = pl.Buffered(1)
    return pl.pallas_call(
        _ffn_sample_body,
        grid=(1,),
        in_specs=[
            full(n, d),
            pl.BlockSpec((None, 6, n, d), lambda i: (l, 0, 0, 0)),
            pl.BlockSpec((None, 1, d), lsel),
            pl.BlockSpec((None, d, f2), lsel, pipeline_mode=once),
            full(n, f2), full(n, f2),
            pl.BlockSpec((None, FFN_CONV_K, f2), lsel),
            pl.BlockSpec((None, 1, f2), lsel),
            pl.BlockSpec((None, FFN_HIDDEN, d), lsel, pipeline_mode=once),
        ],
        out_specs=[full(n, d), full(n, f2)],
        out_shape=[jax.ShapeDtypeStruct((n, d), F32), jax.ShapeDtypeStruct((n, f2), F32)],
        compiler_params=_cparams(("arbitrary",), 56),
        name="ffn_sample",
    )(x, mods, gnorm, w_up, c0, c1, conv_w, conv_b, w_down)


def _rope_tables(pos):
    half = HEAD_DIM // 2
    inv = ROPE_THETA ** (-jnp.arange(half, dtype=F32) / half)
    ang = pos.astype(F32)[:, None] * inv[None, :]
    cos, sin = jnp.cos(ang), jnp.sin(ang)
    cos_h = jnp.concatenate([cos, cos], axis=1)
    sin_h = jnp.concatenate([-sin, sin], axis=1)
    reps = LANES // HEAD_DIM
    return jnp.tile(cos_h, (1, reps)), jnp.tile(sin_h, (1, reps))


def _residue_perm(rows, dil):
    i = jnp.arange(rows)
    src = (i % (rows // dil)) * dil + i // (rows // dil)
    return (src[:, None] == jnp.arange(rows)[None, :]).astype(BF16)


def _block_ones(dtype):
    i = jnp.arange(MXU_DIM) // HEAD_DIM
    return (i[:, None] == i[None, :]).astype(dtype)


def kernel(x_prompt, x_sample, state_ssm, cache_conv, cache_kv_w128, cache_kv_w512, cache_kv_w2048, cache_ffn, c_prompt, c_sample, w_ada, b_ada, g_norm_mix, w_in, ssm_a_re, ssm_a_im, ssm_log_dt, ssm_b_re, ssm_b_im, ssm_c_re, ssm_c_im, ssm_d, ssm_w_glu, ssm_b_glu, conv_w, conv_b, conv_ln_g, conv_ln_b, attn_gq, attn_gk, w_gate, b_gate, w_br_ssm, w_br_conv, w_br_attn, w_out, g_norm_ffn, ffn_w_up, ffn_conv_w, ffn_conv_b, ffn_w_down):
    bp, seq, d = x_prompt.shape
    ns = x_sample.shape[0]
    depth = w_ada.shape[0]
    f2 = 2 * FFN_HIDDEN
    assert x_sample.shape[1] == 1 and seq % (ATTN_BLOCK * ATTN_GROUPS[-1][1]) == 0
    caches = (cache_kv_w128, cache_kv_w512, cache_kv_w2048)

    pad = (-(bp + ns)) % SUBLANES
    c_all = jnp.concatenate([c_prompt, c_sample, jnp.zeros((pad, d), F32)], axis=0)
    mod = _ada_call(c_all, w_ada, b_ada)
    modp = mod[:, :bp].reshape(depth, bp, 6, 1, d)
    mods = mod[:, bp:bp + ns].reshape(depth, ns, 6, d).transpose(0, 2, 1, 3)

    bf = lambda w: w.astype(BF16)
    w_in_b, w_gate_b, w_out_b = bf(w_in), bf(w_gate), bf(w_out)
    w_bs_b, w_bc_b, w_ba_b = bf(w_br_ssm), bf(w_br_conv), bf(w_br_attn)
    w_up_b, w_dn_b, w_glu_b = bf(ffn_w_up), bf(ffn_w_down), bf(ssm_w_glu)

    row3 = lambda t: t.reshape(depth, 1, t.shape[-1])
    g_mix, g_ffn = row3(g_norm_mix), row3(g_norm_ffn)
    gq = row3(jnp.tile(attn_gq, (1, ATTN_HEADS)))
    gk = row3(jnp.tile(attn_gk, (1, ATTN_HEADS)))
    b_gate3, ssm_d3, b_glu3 = row3(b_gate), row3(ssm_d), row3(ssm_b_glu)
    conv_b3, ln_g3, ln_b3, ffn_cb3 = row3(conv_b), row3(conv_ln_g), row3(conv_ln_b), row3(ffn_conv_b)
    conv_w4 = conv_w.reshape(depth, CONV_K, 1, CONV_WIDTH)

    cos_p, sin_p = _rope_tables(jnp.arange(seq, dtype=jnp.int32))
    cos_s, sin_s = _rope_tables(jnp.full((ns,), PAST_LEN, dtype=jnp.int32))
    ones_b, ones_f = _block_ones(BF16), _block_ones(F32)
    p4 = _residue_perm(TM_INPROJ, ATTN_GROUPS[1][1])
    p16 = _residue_perm(TM_INPROJ, ATTN_GROUPS[2][1])
    seg = L_SSM // SUBLANES
    pm = _residue_perm(L_SSM, seg)
    pmt = pm.T
    zero_state = jnp.zeros((bp, 1, STATE_W), F32)

    yp = x_prompt
    ys = x_sample.reshape(ns, d)
    ssm_p, ssm_s, conv_p, conv_s, ffn_p, ffn_s = [], [], [], [], [], []
    kv_p = [[] for _ in ATTN_GROUPS]
    caches_t = [c.transpose(0, 1, 3, 4, 5, 2) for c in caches]
    rolled = None
    gw = HEADS_PER_GROUP * HEAD_DIM
    for l in range(depth):
        tabs = _ssm_tables(ssm_a_re[l], ssm_a_im[l], ssm_log_dt[l], ssm_b_re[l], ssm_b_im[l],
                           ssm_c_re[l], ssm_c_im[l], seg)

        (zs, zc, k_tail, v_tail, q0, k0, v0, q1, k1, v1, q2, k2, v2) = _inproj_prompt(
            yp, modp, l, g_mix, w_in_b, gq, gk, cos_p, sin_p, ones_b, p4, p16)
        y_ssm, st = _ssm_prompt(zs, zero_state, l, pm, pmt, tabs, ssm_d3, w_glu_b, b_glu3)
        y_conv, a_tail = _conv_prompt(zc, l, conv_w, conv_b3, ln_g3, ln_b3)
        attn = [_attn_prompt(q0, k0, v0, 1), _attn_prompt(q1, k1, v1, ATTN_GROUPS[1][1]),
                _attn_prompt(q2, k2, v2, ATTN_GROUPS[2][1])]
        x_mid = _merge_prompt(yp, modp, l, g_mix, w_gate_b, b_gate3, y_ssm, y_conv, attn,
                              w_bs_b, w_bc_b, w_ba_b, w_out_b)
        yp, up_tail = _ffn_prompt(x_mid, modp, l, g_ffn, w_up_b, ffn_conv_w, ffn_cb3, w_dn_b)
        ssm_p.append(_slab_to_state(st.reshape(bp, STATE_W)))
        conv_p.append(a_tail[:, CONV_HALO - (CONV_K - 1):])
        ffn_p.append(up_tail[:, HIST - (FFN_CONV_K - 1):])
        for gi, (win, _) in enumerate(ATTN_GROUPS):
            keep = min(win, seq)
            cols = slice(gi * gw, (gi + 1) * gw)
            kk = k_tail[:, -keep:, cols].reshape(bp, keep, HEADS_PER_GROUP, HEAD_DIM)
            vv = v_tail[:, -keep:, cols].reshape(bp, keep, HEADS_PER_GROUP, HEAD_DIM)
            kv_p[gi].append(jnp.stack([kk, vv], axis=2))

        zs_s, zc_s, q_s, k_s, v_s = _inproj_sample(ys, mods, l, g_mix, w_in, gq, gk, cos_s, sin_s, ones_f)
        y_ssm_s, st_s = _ssm_sample(zs_s, _state_to_slab(state_ssm[l]), l, tabs, ssm_d3, ssm_w_glu, b_glu3)
        hist_t = cache_conv[l].transpose(1, 0, 2)
        y_conv_s, a_s = _conv_sample(zc_s, hist_t, l, conv_w4, conv_b3, ln_g3, ln_b3)
        cols = lambda t: t.reshape(ns, len(ATTN_GROUPS), HEADS_PER_GROUP, HEAD_DIM, 1)
        y_attn_s, *rolled = _attn_roll(cols(q_s), cols(k_s), cols(v_s), caches_t, rolled, l)
        y_attn_s = y_attn_s.reshape(ns, ATTN_OUT)
        x_mid_s = _merge_sample(ys, mods, l, g_mix, w_gate, b_gate3, y_ssm_s, y_conv_s, y_attn_s,
                                w_br_ssm, w_br_conv, w_br_attn, w_out)
        c_old, c_new = cache_ffn[l, :, 0], cache_ffn[l, :, 1]
        ys, up_s = _ffn_sample(x_mid_s, mods, l, g_ffn, ffn_w_up, c_old, c_new, ffn_conv_w, ffn_cb3, ffn_w_down)
        ssm_s.append(_slab_to_state(st_s))
        conv_s.append(jnp.concatenate([cache_conv[l][:, 1:], a_s[:, None, :]], axis=1))
        ffn_s.append(jnp.stack([c_new, up_s], axis=1))

    kv_s = [t.transpose(0, 1, 5, 2, 3, 4) for t in rolled]
    st = lambda t: jnp.stack(t, axis=0)
    return (yp, ys.reshape(ns, 1, d), st(ssm_p), st(ssm_s), st(conv_p), st(conv_s),
            st(kv_p[0]), kv_s[0], st(kv_p[1]), kv_s[1], st(kv_p[2]), kv_s[2], st(ffn_p), st(ffn_s))
```

```python
import functools
import math

import jax
import jax.numpy as jnp
from jax import lax
from jax.experimental import pallas as pl
from jax.experimental.pallas import tpu as pltpu

F32 = jnp.float32
BF16 = jnp.bfloat16

D_MODEL = 1024
SSM_WIDTH = 512
SSM_GROUP = 16
SSM_GROUPS = 32
SSM_N = 64
CONV_WIDTH = 512
CONV_K = 31
HEAD_DIM = 64
HEADS_PER_GROUP = 4
ATTN_GROUPS = ((128, 1), (512, 4), (2048, 16))
ATTN_HEADS = 12
ATTN_WIDTH = 768
ATTN_OUT = 256
ATTN_BLOCK = 128
ROPE_THETA = 10000.0
FFN_HIDDEN = 2816
FFN_CONV_K = 3
IN_WIDTH = SSM_WIDTH + 2 * CONV_WIDTH + 3 * ATTN_WIDTH
EPS = 1e-6
PAST_LEN = 8192

SUBLANES = 8
LANES = 128
MXU_DIM = 256
SLAB_GROUPS = 8
N_SLABS = SSM_GROUPS // SLAB_GROUPS
SLAB_CH = SLAB_GROUPS * SSM_GROUP
SLAB_ST = SLAB_GROUPS * SSM_N
STATE_W = 2 * SSM_GROUPS * SSM_N
NEG_BIG = -1e30
GELU_C = math.sqrt(2.0 / math.pi)

TM_INPROJ = 512
L_SSM = 256
T_CONV = 256
TM_MERGE = 256
TM_FFN = 256
FFN_CHUNK = 256
HIST = 8
CONV_HALO = 32
KV_TAIL = 2048
ATTN_STEP_BLOCKS = {1: 4, 4: 4, 16: 2}


def _cparams(sem, vmem_mb):
    return pltpu.CompilerParams(dimension_semantics=sem, vmem_limit_bytes=vmem_mb * 1024 * 1024)


def _sigmoid(x):
    return 1.0 / (1.0 + jnp.exp(-x))


def _gelu_tanh(x):
    neg_2u = x * ((-2.0 * GELU_C * 0.044715) * (x * x) - 2.0 * GELU_C)
    return x / (1.0 + jnp.exp(neg_2u))


def _mm(a, w, precise):
    if precise:
        return jnp.dot(a, w, precision=lax.Precision.HIGHEST, preferred_element_type=F32)
    return jnp.dot(a.astype(BF16), w, preferred_element_type=F32)


def _norm_mod(x, g, scale, shift):
    ms = jnp.mean(x * x, axis=-1, keepdims=True)
    return (x * lax.rsqrt(ms + EPS) * g) * (1.0 + scale) + shift


def _ada_body(c_ref, w_ref, b_ref, o_ref):
    c = c_ref[...]
    s = c * _sigmoid(c)
    o_ref[...] = jnp.dot(s, w_ref[...], precision=lax.Precision.HIGHEST,
                         preferred_element_type=F32) + b_ref[...]


def _ada_call(c_all, w_ada, b_ada):
    depth, d, n6 = w_ada.shape
    rows = c_all.shape[0]
    tn = 1536
    return pl.pallas_call(
        _ada_body,
        grid=(depth, n6 // tn),
        in_specs=[
            pl.BlockSpec((rows, d), lambda l, j: (0, 0)),
            pl.BlockSpec((None, d, tn), lambda l, j: (l, 0, j)),
            pl.BlockSpec((None, 1, tn), lambda l, j: (l, 0, j)),
        ],
        out_specs=pl.BlockSpec((None, rows, tn), lambda l, j: (l, 0, j)),
        out_shape=jax.ShapeDtypeStruct((depth, rows, n6), F32),
        compiler_params=_cparams(("arbitrary", "arbitrary"), 40),
        name="ada_mod",
    )(c_all, w_ada, b_ada.reshape(depth, 1, n6))


def _head_norm_rope(z, gvec, ones, cos, sins, precise):
    sq = z * z
    parts = []
    for c in range(ATTN_WIDTH // MXU_DIM):
        parts.append(_mm(sq[:, c * MXU_DIM:(c + 1) * MXU_DIM], ones, precise))
    ms = jnp.concatenate(parts, axis=1) * (1.0 / HEAD_DIM)
    y = z * lax.rsqrt(ms + EPS) * gvec
    lane = lax.broadcasted_iota(jnp.int32, (1, LANES), 1)
    first = (lane % HEAD_DIM) < (HEAD_DIM // 2)
    outs = []
    for c in range(ATTN_WIDTH // LANES):
        yc = y[:, c * LANES:(c + 1) * LANES]
        partner = jnp.where(first, pltpu.roll(yc, LANES - HEAD_DIM // 2, 1),
                            pltpu.roll(yc, HEAD_DIM // 2, 1))
        outs.append(yc * cos + partner * sins)
    return jnp.concatenate(outs, axis=1)


def _inproj_body(*refs, sample):
    if sample:
        (x_ref, mod_ref, g_ref, w_ref, gq_ref, gk_ref, cos_ref, sin_ref, ones_ref,
         zs_ref, zc_ref, q_ref, k_ref, v_ref) = refs
    else:
        (x_ref, mod_ref, g_ref, w_ref, gq_ref, gk_ref, cos_ref, sin_ref, ones_ref, p4_ref, p16_ref,
         zs_ref, zc_ref, kt_ref, vt_ref,
         q0_ref, k0_ref, v0_ref, q1_ref, k1_ref, v1_ref, q2_ref, k2_ref, v2_ref) = refs
    precise = sample
    x = x_ref[...]
    h = _norm_mod(x, g_ref[...], mod_ref[1], mod_ref[0])
    hm = h if precise else h.astype(BF16)
    c1 = SSM_WIDTH
    c2 = c1 + 2 * CONV_WIDTH
    c3 = c2 + ATTN_WIDTH
    c4 = c3 + ATTN_WIDTH
    zs_ref[...] = _mm(hm, w_ref[:, 0:c1], precise)
    zc_ref[...] = _mm(hm, w_ref[:, c1:c2], precise)
    zq = _mm(hm, w_ref[:, c2:c3], precise)
    zk = _mm(hm, w_ref[:, c3:c4], precise)
    zv = _mm(hm, w_ref[:, c4:IN_WIDTH], precise)
    cos = cos_ref[...]
    sins = sin_ref[...]
    ones = ones_ref[...]
    q = _head_norm_rope(zq, gq_ref[...], ones, cos, sins, precise)
    k = _head_norm_rope(zk, gk_ref[...], ones, cos, sins, precise)
    if sample:
        q_ref[...] = q
        k_ref[...] = k
        v_ref[...] = zv
        return
    kt_ref[...] = k
    vt_ref[...] = zv
    qb = (q * (HEAD_DIM ** -0.5)).astype(BF16)
    kb = k.astype(BF16)
    vb = zv.astype(BF16)
    gw = HEADS_PER_GROUP * HEAD_DIM
    q0_ref[0] = qb[:, 0:gw]
    k0_ref[0] = kb[:, 0:gw]
    v0_ref[0] = vb[:, 0:gw]
    for gi, (p_ref, outs) in enumerate(((p4_ref, (q1_ref, k1_ref, v1_ref)),
                                        (p16_ref, (q2_ref, k2_ref, v2_ref))), start=1):
        pm = p_ref[...]
        for src, o_ref in zip((qb, kb, vb), outs):
            dil, rows = o_ref.shape[0], o_ref.shape[1]
            perm = jnp.dot(pm, src[:, gi * gw:(gi + 1) * gw], preferred_element_type=F32).astype(BF16)
            for r in range(dil):
                o_ref[r] = perm[r * rows:(r + 1) * rows]


def _inproj_prompt(x, modp, l, gnorm, w_in, gq, gk, cos, sins, ones, p4, p16):
    b_sz, seq, d = x.shape
    tm = TM_INPROJ
    nt = seq // tm
    tail = min(KV_TAIL, seq)
    ft = nt - tail // tm
    gw = HEADS_PER_GROUP * HEAD_DIM
    tmap = lambda b, i: (b, jnp.maximum(i - ft, 0), 0)
    in_specs = [
        pl.BlockSpec((None, tm, d), lambda b, i: (b, i, 0)),
        pl.BlockSpec((None, None, 6, 1, d), lambda b, i: (l, b, 0, 0, 0)),
        pl.BlockSpec((None, 1, d), lambda b, i: (l, 0, 0)),
        pl.BlockSpec((None, d, IN_WIDTH), lambda b, i: (l, 0, 0)),
        pl.BlockSpec((None, 1, ATTN_WIDTH), lambda b, i: (l, 0, 0)),
        pl.BlockSpec((None, 1, ATTN_WIDTH), lambda b, i: (l, 0, 0)),
        pl.BlockSpec((tm, LANES), lambda b, i: (i, 0)),
        pl.BlockSpec((tm, LANES), lambda b, i: (i, 0)),
        pl.BlockSpec((MXU_DIM, MXU_DIM), lambda b, i: (0, 0)),
        pl.BlockSpec((tm, tm), lambda b, i: (0, 0)),
        pl.BlockSpec((tm, tm), lambda b, i: (0, 0)),
    ]
    out_shape = [
        jax.ShapeDtypeStruct((b_sz, seq, SSM_WIDTH), F32),
        jax.ShapeDtypeStruct((b_sz, seq, 2 * CONV_WIDTH), F32),
        jax.ShapeDtypeStruct((b_sz, tail, ATTN_WIDTH), F32),
        jax.ShapeDtypeStruct((b_sz, tail, ATTN_WIDTH), F32),
    ]
    out_specs = [
        pl.BlockSpec((None, tm, SSM_WIDTH), lambda b, i: (b, i, 0)),
        pl.BlockSpec((None, tm, 2 * CONV_WIDTH), lambda b, i: (b, i, 0)),
        pl.BlockSpec((None, tm, ATTN_WIDTH), tmap),
        pl.BlockSpec((None, tm, ATTN_WIDTH), tmap),
    ]
    for _, dil in ATTN_GROUPS:
        for _ in range(3):
            out_shape.append(jax.ShapeDtypeStruct((b_sz, dil, seq // dil, gw), BF16))
            out_specs.append(pl.BlockSpec((None, dil, tm // dil, gw), lambda b, i: (b, 0, i, 0)))
    return pl.pallas_call(
        functools.partial(_inproj_body, sample=False),
        grid=(b_sz, nt),
        in_specs=in_specs,
        out_specs=out_specs,
        out_shape=out_shape,
        compiler_params=_cparams(("arbitrary", "arbitrary"), 56),
        name="inproj_prompt",
    )(x, modp, gnorm, w_in, gq, gk, cos, sins, ones, p4, p16)


def _inproj_sample(x, mods, l, gnorm, w_in, gq, gk, cos, sins, ones):
    n, d = x.shape
    lsel = lambda i: (l, 0, 0)
    in_specs = [
        pl.BlockSpec((n, d), lambda i: (0, 0)),
        pl.BlockSpec((None, 6, n, d), lambda i: (l, 0, 0, 0)),
        pl.BlockSpec((None, 1, d), lsel),
        pl.BlockSpec((None, d, IN_WIDTH), lsel),
        pl.BlockSpec((None, 1, ATTN_WIDTH), lsel),
        pl.BlockSpec((None, 1, ATTN_WIDTH), lsel),
        pl.BlockSpec((n, LANES), lambda i: (0, 0)),
        pl.BlockSpec((n, LANES), lambda i: (0, 0)),
        pl.BlockSpec((MXU_DIM, MXU_DIM), lambda i: (0, 0)),
    ]
    widths = (SSM_WIDTH, 2 * CONV_WIDTH, ATTN_WIDTH, ATTN_WIDTH, ATTN_WIDTH)
    return pl.pallas_call(
        functools.partial(_inproj_body, sample=True),
        grid=(1,),
        in_specs=in_specs,
        out_specs=[pl.BlockSpec((n, w), lambda i: (0, 0)) for w in widths],
        out_shape=[jax.ShapeDtypeStruct((n, w), F32) for w in widths],
        compiler_params=_cparams(("arbitrary",), 48),
        name="inproj_sample",
    )(x, mods, gnorm, w_in, gq, gk, cos, sins, ones)


def _ssm_tail(y_raw, u, d_ref, wglu_ref, bglu_ref, precise):
    y = _gelu_tanh(y_raw + d_ref[...] * u)
    return y * _sigmoid(_mm(y, wglu_ref[...], precise) + bglu_ref[...])


def _ssm_prompt_body(zs_ref, s0_ref, pm_ref, pmt_ref, b_ref, c_ref, lam_ref, lamp_ref, pw_ref,
                     d_ref, wglu_ref, bglu_ref, y_ref, st_ref,
                     v_ref, sb_ref, cs_ref, carry_ref):
    ci = pl.program_id(1)
    chunk = zs_ref.shape[0]
    seg = chunk // SUBLANES

    @pl.when(ci == 0)
    def _():
        carry_ref[...] = s0_ref[...]

    u = zs_ref[...]
    up = jnp.dot(pm_ref[...], u.astype(BF16), preferred_element_type=F32).astype(BF16)
    re, im = slice(0, SLAB_ST), slice(SLAB_ST, 2 * SLAB_ST)
    for m in range(N_SLABS):
        v_ref[m] = jnp.dot(up[:, m * SLAB_CH:(m + 1) * SLAB_CH], b_ref[m], preferred_element_type=F32)
    y_parts = [None] * N_SLABS
    for pair in range(0, N_SLABS, 2):
        slabs = (pair, pair + 1)
        zero = jnp.zeros((SUBLANES, SLAB_ST), F32)
        state = {m: (zero, zero) for m in slabs}
        for i in range(seg):
            rows = slice(i * SUBLANES, (i + 1) * SUBLANES)
            for m in slabs:
                lr = lam_ref[m, 0]
                li = lam_ref[m, 1]
                sr, si = state[m]
                nsr = lr * sr - li * si + v_ref[m, rows, re]
                nsi = lr * si + li * sr + v_ref[m, rows, im]
                v_ref[m, rows, re] = nsr
                v_ref[m, rows, im] = nsi
                state[m] = (nsr, nsi)
        for m in slabs:
            base = m * 2 * SLAB_ST
            sr, si = state[m]
            er = carry_ref[0:1, base:base + SLAB_ST]
            ei = carry_ref[0:1, base + SLAB_ST:base + 2 * SLAB_ST]
            pr = lamp_ref[m, 0:1, :]
            pi = lamp_ref[m, 1:2, :]
            for r in range(SUBLANES):
                cs_ref[m, r:r + 1, re] = er
                cs_ref[m, r:r + 1, im] = ei
                ner = sr[r:r + 1] + pr * er - pi * ei
                nei = si[r:r + 1] + pr * ei + pi * er
                er, ei = ner, nei
            carry_ref[0:1, base:base + SLAB_ST] = er
            carry_ref[0:1, base + SLAB_ST:base + 2 * SLAB_ST] = ei
        for m in slabs:
            csr = cs_ref[m, :, re]
            csi = cs_ref[m, :, im]
            for i2 in range(seg // 2):
                rows16 = slice(i2 * 2 * SUBLANES, (i2 + 1) * 2 * SUBLANES)
                rows_r, rows_i = [], []
                for h in range(2):
                    i = i2 * 2 + h
                    rows = slice(i * SUBLANES, (i + 1) * SUBLANES)
                    qr = pw_ref[m, 0, rows, :]
                    qi = pw_ref[m, 1, rows, :]
                    rows_r.append(v_ref[m, rows, re] + qr * csr - qi * csi)
                    rows_i.append(v_ref[m, rows, im] + qr * csi + qi * csr)
                sb_ref[m, rows16, re] = jnp.concatenate(rows_r, axis=0).astype(BF16)
                sb_ref[m, rows16, im] = jnp.concatenate(rows_i, axis=0).astype(BF16)
            y_parts[m] = jnp.dot(sb_ref[m], c_ref[m], preferred_element_type=F32)
    y_perm = jnp.concatenate(y_parts, axis=1)
    hi = y_perm.astype(BF16)
    lo = (y_perm - hi.astype(F32)).astype(BF16)
    pmt = pmt_ref[...]
    y_nat = jnp.dot(pmt, hi, preferred_element_type=F32) + jnp.dot(pmt, lo, preferred_element_type=F32)
    y_ref[...] = _ssm_tail(y_nat, u, d_ref, wglu_ref, bglu_ref, False).astype(BF16)
    st_ref[...] = carry_ref[...]


def _ssm_prompt(zs, s0, l, pm, pmt, tabs, ssm_d, w_glu, b_glu):
    b_sz, seq, _ = zs.shape
    chunk = L_SSM
    seg = chunk // SUBLANES
    full = lambda *shape: pl.BlockSpec(shape, lambda b, c: (0,) * len(shape))
    lsel = lambda b, c: (l, 0, 0)
    return pl.pallas_call(
        _ssm_prompt_body,
        grid=(b_sz, seq // chunk),
        in_specs=[
            pl.BlockSpec((None, chunk, SSM_WIDTH), lambda b, c: (b, c, 0)),
            pl.BlockSpec((None, 1, STATE_W), lambda b, c: (b, 0, 0)),
            full(chunk, chunk), full(chunk, chunk),
            full(N_SLABS, SLAB_CH, 2 * SLAB_ST),
            full(N_SLABS, 2 * SLAB_ST, SLAB_CH),
            full(N_SLABS, 2, SUBLANES, SLAB_ST), full(N_SLABS, 2, SLAB_ST),
            full(N_SLABS, 2, chunk, SLAB_ST),
            pl.BlockSpec((None, 1, SSM_WIDTH), lsel),
            pl.BlockSpec((None, SSM_WIDTH, SSM_WIDTH), lsel),
            pl.BlockSpec((None, 1, SSM_WIDTH), lsel),
        ],
        out_specs=[
            pl.BlockSpec((None, chunk, SSM_WIDTH), lambda b, c: (b, c, 0)),
            pl.BlockSpec((None, 1, STATE_W), lambda b, c: (b, 0, 0)),
        ],
        out_shape=[jax.ShapeDtypeStruct((b_sz, seq, SSM_WIDTH), BF16),
                   jax.ShapeDtypeStruct((b_sz, 1, STATE_W), F32)],
        scratch_shapes=[
            pltpu.VMEM((N_SLABS, chunk, 2 * SLAB_ST), F32),
            pltpu.VMEM((N_SLABS, chunk, 2 * SLAB_ST), BF16),
            pltpu.VMEM((N_SLABS, SUBLANES, 2 * SLAB_ST), F32),
            pltpu.VMEM((1, STATE_W), F32),
        ],
        compiler_params=_cparams(("arbitrary", "arbitrary"), 40),
        name="ssm_prompt",
    )(zs, s0, pm, pmt, tabs["b_bf"], tabs["c_bf"], tabs["lam_rows"], tabs["lam_seg"], tabs["pw_rows"],
      ssm_d, w_glu, b_glu)


def _ssm_sample_body(u_ref, s0_ref, b_ref, c_ref, lam_ref, d_ref, wglu_ref, bglu_ref, y_ref, st_ref):
    u = u_ref[...]
    y_parts = []
    for m in range(N_SLABS):
        base = m * 2 * SLAB_ST
        v = _mm(u[:, m * SLAB_CH:(m + 1) * SLAB_CH], b_ref[m], True)
        lr = lam_ref[m, 0:1, :]
        li = lam_ref[m, 1:2, :]
        sr0 = s0_ref[:, base:base + SLAB_ST]
        si0 = s0_ref[:, base + SLAB_ST:base + 2 * SLAB_ST]
        sr = lr * sr0 - li * si0 + v[:, 0:SLAB_ST]
        si = lr * si0 + li * sr0 + v[:, SLAB_ST:2 * SLAB_ST]
        st_ref[:, base:base + SLAB_ST] = sr
        st_ref[:, base + SLAB_ST:base + 2 * SLAB_ST] = si
        y_parts.append(_mm(jnp.concatenate([sr, si], axis=1), c_ref[m], True))
    y_raw = jnp.concatenate(y_parts, axis=1)
    y_ref[...] = _ssm_tail(y_raw, u, d_ref, wglu_ref, bglu_ref, True)


def _ssm_sample(zs, s0, l, tabs, ssm_d, w_glu, b_glu):
    n = zs.shape[0]
    full = lambda *shape: pl.BlockSpec(shape, lambda i: (0,) * len(shape))
    lsel = lambda i: (l, 0, 0)
    return pl.pallas_call(
        _ssm_sample_body,
        grid=(1,),
        in_specs=[
            full(n, SSM_WIDTH), full(n, STATE_W),
            full(N_SLABS, SLAB_CH, 2 * SLAB_ST), full(N_SLABS, 2 * SLAB_ST, SLAB_CH),
            full(N_SLABS, 2, SLAB_ST),
            pl.BlockSpec((None, 1, SSM_WIDTH), lsel),
            pl.BlockSpec((None, SSM_WIDTH, SSM_WIDTH), lsel),
            pl.BlockSpec((None, 1, SSM_WIDTH), lsel),
        ],
        out_specs=[full(n, SSM_WIDTH), full(n, STATE_W)],
        out_shape=[jax.ShapeDtypeStruct((n, SSM_WIDTH), F32), jax.ShapeDtypeStruct((n, STATE_W), F32)],
        compiler_params=_cparams(("arbitrary",), 32),
        name="ssm_sample",
    )(zs, s0, tabs["b_f32"], tabs["c_f32"], tabs["lam"], ssm_d, w_glu, b_glu)


def _ssm_tables(a_re, a_im, log_dt, b_re, b_im, c_re, c_im, seg):
    dt = jnp.exp(log_dt)[:, None]
    xr, xi = a_re * dt, a_im * dt
    mag = jnp.exp(xr)
    lr, li = mag * jnp.cos(xi), mag * jnp.sin(xi)
    den = a_re * a_re + a_im * a_im
    nr, ni = lr - 1.0, li
    cr = (nr * a_re + ni * a_im) / den
    cim = (ni * a_re - nr * a_im) / den
    bbr = cr[..., None] * b_re - cim[..., None] * b_im
    bbi = cr[..., None] * b_im + cim[..., None] * b_re
    eye = jnp.eye(SLAB_GROUPS, dtype=F32)

    def b_slab(t):
        t = t.reshape(N_SLABS, SLAB_GROUPS, SSM_N, SSM_GROUP)
        return jnp.einsum('mgnc,gh->mgchn', t, eye).reshape(N_SLABS, SLAB_CH, SLAB_ST)

    def c_slab(t):
        t = t.reshape(N_SLABS, SLAB_GROUPS, SSM_GROUP, SSM_N)
        return jnp.einsum('mgcn,gh->mgnhc', t, eye).reshape(N_SLABS, SLAB_ST, SLAB_CH)

    b_mat = jnp.concatenate([b_slab(bbr), b_slab(bbi)], axis=2)
    c_mat = jnp.concatenate([c_slab(c_re), c_slab(-c_im)], axis=1)

    def power(k):
        mk = jnp.exp(k * xr)
        return jnp.stack([(mk * jnp.cos(k * xi)).reshape(N_SLABS, SLAB_ST),
                          (mk * jnp.sin(k * xi)).reshape(N_SLABS, SLAB_ST)], axis=1)

    pw = jnp.stack([power(float(k)) for k in range(1, seg + 1)], axis=2)
    lam = power(1.0)
    rep = lambda t: jnp.repeat(t, SUBLANES, axis=2)
    return dict(b_f32=b_mat, c_f32=c_mat, b_bf=b_mat.astype(BF16), c_bf=c_mat.astype(BF16),
                lam=lam, lam_rows=rep(lam[:, :, None, :]), lam_seg=power(float(seg)), pw_rows=rep(pw))


def _state_to_slab(s):
    b = s.shape[0]
    return s.reshape(b, N_SLABS, SLAB_GROUPS, SSM_N, 2).transpose(0, 1, 4, 2, 3).reshape(b, STATE_W)


def _slab_to_state(x):
    b = x.shape[0]
    return x.reshape(b, N_SLABS, 2, SLAB_GROUPS, SSM_N).transpose(0, 1, 3, 4, 2).reshape(b, SSM_GROUPS, SSM_N, 2)


def _ln_silu(y, g, b):
    mu = jnp.mean(y, axis=-1, keepdims=True)
    var = jnp.mean(jnp.square(y - mu), axis=-1, keepdims=True)
    t = (y - mu) * lax.rsqrt(var + EPS) * g + b
    return t * _sigmoid(t)


def _conv_prompt_body(zc_ref, w_ref, b_ref, lg_ref, lb_ref, y_ref, tail_ref, abuf_ref, shift_ref):
    i = pl.program_id(1)
    rows = zc_ref.shape[0]

    @pl.when(i == 0)
    def _():
        abuf_ref[0:CONV_HALO, :] = jnp.zeros((CONV_HALO, CONV_WIDTH), F32)

    z = zc_ref[...]
    a = z[:, 0:CONV_WIDTH] * _sigmoid(z[:, CONV_WIDTH:2 * CONV_WIDTH])
    abuf_ref[CONV_HALO:CONV_HALO + rows, :] = a
    span = shift_ref.shape[1]
    for c in range(1, SUBLANES):
        shift_ref[c - 1] = abuf_ref[c:c + span, :]
    acc = jnp.zeros((rows, CONV_WIDTH), F32) + b_ref[...]
    first = CONV_HALO - (CONV_K - 1)
    for j in range(CONV_K):
        c = (first + j) % SUBLANES
        base = first + j - c
        src = abuf_ref[base:base + rows, :] if c == 0 else shift_ref[c - 1, base:base + rows, :]
        acc = acc + w_ref[j:j + 1, :] * src
    y_ref[...] = _ln_silu(acc, lg_ref[...], lb_ref[...]).astype(BF16)
    last = a[rows - CONV_HALO:rows]
    tail_ref[...] = last
    abuf_ref[0:CONV_HALO, :] = last


def _conv_prompt(zc, l, conv_w, conv_b, ln_g, ln_b):
    b_sz, seq, _ = zc.shape
    rows = T_CONV
    lsel = lambda b, i: (l, 0, 0)
    return pl.pallas_call(
        _conv_prompt_body,
        grid=(b_sz, seq // rows),
        in_specs=[
            pl.BlockSpec((None, rows, 2 * CONV_WIDTH), lambda b, i: (b, i, 0)),
            pl.BlockSpec((None, CONV_K, CONV_WIDTH), lsel),
            pl.BlockSpec((None, 1, CONV_WIDTH), lsel),
            pl.BlockSpec((None, 1, CONV_WIDTH), lsel),
            pl.BlockSpec((None, 1, CONV_WIDTH), lsel),
        ],
        out_specs=[
            pl.BlockSpec((None, rows, CONV_WIDTH), lambda b, i: (b, i, 0)),
            pl.BlockSpec((None, CONV_HALO, CONV_WIDTH), lambda b, i: (b, 0, 0)),
        ],
        out_shape=[jax.ShapeDtypeStruct((b_sz, seq, CONV_WIDTH), BF16),
                   jax.ShapeDtypeStruct((b_sz, CONV_HALO, CONV_WIDTH), F32)],
        scratch_shapes=[pltpu.VMEM((rows + CONV_HALO, CONV_WIDTH), F32),
                        pltpu.VMEM((SUBLANES - 1, rows + CONV_HALO - SUBLANES, CONV_WIDTH), F32)],
        compiler_params=_cparams(("arbitrary", "arbitrary"), 32),
        name="conv_prompt",
    )(zc, conv_w, conv_b, ln_g, ln_b)


def _conv_sample_body(zc_ref, hist_ref, w_ref, b_ref, lg_ref, lb_ref, y_ref, a_ref):
    z = zc_ref[...]
    a = z[:, 0:CONV_WIDTH] * _sigmoid(z[:, CONV_WIDTH:2 * CONV_WIDTH])
    a_ref[...] = a
    acc = b_ref[...] + w_ref[CONV_K - 1] * a
    for j in range(CONV_K - 1):
        acc = acc + w_ref[j] * hist_ref[j]
    y_ref[...] = _ln_silu(acc, lg_ref[...], lb_ref[...])


def _conv_sample(zc, hist_t, l, conv_w4, conv_b, ln_g, ln_b):
    n = zc.shape[0]
    full = lambda *shape: pl.BlockSpec(shape, lambda i: (0,) * len(shape))
    lsel = lambda i: (l, 0, 0)
    return pl.pallas_call(
        _conv_sample_body,
        grid=(1,),
        in_specs=[
            full(n, 2 * CONV_WIDTH), full(CONV_K - 1, n, CONV_WIDTH),
            pl.BlockSpec((None, CONV_K, 1, CONV_WIDTH), lambda i: (l, 0, 0, 0)),
            pl.BlockSpec((None, 1, CONV_WIDTH), lsel),
            pl.BlockSpec((None, 1, CONV_WIDTH), lsel),
            pl.BlockSpec((None, 1, CONV_WIDTH), lsel),
        ],
        out_specs=[full(n, CONV_WIDTH), full(n, CONV_WIDTH)],
        out_shape=[jax.ShapeDtypeStruct((n, CONV_WIDTH), F32)] * 2,
        compiler_params=_cparams(("arbitrary",), 32),
        name="conv_sample",
    )(zc, hist_t, conv_w4, conv_b, ln_g, ln_b)


def _attn_prompt_body(q_ref, kc_ref, kp_ref, vc_ref, vp_ref, oa_ref, ob_ref, la_ref, lb_ref, *, dil):
    j = pl.program_id(1)
    r = pl.program_id(2)
    blk = ATTN_BLOCK
    nblk = q_ref.shape[0] // blk
    q_all = q_ref[...]
    k_all = jnp.concatenate([kp_ref[...], kc_ref[...]], axis=0)
    v_all = jnp.concatenate([vp_ref[...], vc_ref[...]], axis=0)
    qi = lax.broadcasted_iota(jnp.int32, (blk, 2 * blk), 0)
    ki = lax.broadcasted_iota(jnp.int32, (blk, 2 * blk), 1)
    dist = qi + blk - ki
    band = jnp.where(dist >= 0, jnp.where(dist <= blk, 1, 0), 0)
    first_ok = jnp.where(ki >= blk, 1, jnp.where(j > 0, 1, 0))
    lane = lax.broadcasted_iota(jnp.int32, (1, LANES), 1)
    lo = lane < HEAD_DIM
    for s_blk in range(nblk):
        q = q_all[s_blk * blk:(s_blk + 1) * blk]
        k = k_all[s_blk * blk:(s_blk + 2) * blk]
        v = v_all[s_blk * blk:(s_blk + 2) * blk]
        valid = ((band * first_ok) if s_blk == 0 else band) > 0
        o_parts, lse_parts = [], []
        for hp in range(HEADS_PER_GROUP // 2):
            sl = slice(hp * LANES, (hp + 1) * LANES)
            q2, k2, v2 = q[:, sl], k[:, sl], v[:, sl]
            res = []
            for half in range(2):
                keep = lo if half == 0 else jnp.logical_not(lo)
                qm = jnp.where(keep, q2, jnp.zeros_like(q2))
                s = lax.dot_general(qm, k2, (((1,), (1,)), ((), ())), preferred_element_type=F32)
                s = jnp.where(valid, s, NEG_BIG)
                m = jnp.max(s, axis=-1, keepdims=True)
                p = jnp.exp(s - m)
                den = jnp.sum(p, axis=-1, keepdims=True)
                o = jnp.dot(p.astype(BF16), v2, preferred_element_type=F32)
                res.append((o / den, m + jnp.log(den)))
            o_parts.append(jnp.where(lo, res[0][0], res[1][0]))
            lse_parts.append(jnp.where(lo, res[0][1], res[1][1]))
        for val, ref in zip(o_parts + lse_parts, (oa_ref, ob_ref, la_ref, lb_ref)):
            if dil == 1:
                ref[s_blk * blk:(s_blk + 1) * blk, :] = val
            else:
                ref[pl.ds(r + s_blk * blk * dil, blk, stride=dil), :] = val


def _attn_prompt(q, k, v, dil):
    b_sz, _, m_len, gw = q.shape
    blk = ATTN_BLOCK
    seq = m_len * dil
    nblk = ATTN_STEP_BLOCKS[dil]
    cur = lambda b, j, r: (b, r, j, 0)
    prev = lambda b, j, r: (b, r, jnp.maximum(j * nblk - 1, 0), 0)
    bs = lambda f: pl.BlockSpec((None, None, nblk * blk, gw), f)
    bp = lambda f: pl.BlockSpec((None, None, blk, gw), f)
    span = nblk * blk * dil
    return pl.pallas_call(
        functools.partial(_attn_prompt_body, dil=dil),
        grid=(b_sz, m_len // (nblk * blk), dil),
        in_specs=[bs(cur), bs(cur), bp(prev), bs(cur), bp(prev)],
        out_specs=[pl.BlockSpec((None, span, LANES), lambda b, j, r: (b, j, 0))] * 4,
        out_shape=[jax.ShapeDtypeStruct((b_sz, seq, LANES), F32)] * 4,
        compiler_params=_cparams(("arbitrary", "arbitrary", "arbitrary"), 40),
        name=f"attn_prompt_d{dil}",
    )(q, k, k, v, v)


def _attn_roll_body(*refs):
    ng = len(ATTN_GROUPS)
    q_ref, kn_ref, vn_ref = refs[:3]
    c_refs = refs[3:3 + ng]
    y_ref = refs[-1 - ng]
    o_refs = refs[-ng:]
    scale = HEAD_DIM ** -0.5
    outs = [[None] * ng for _ in range(HEADS_PER_GROUP)]
    lses = [[None] * ng for _ in range(HEADS_PER_GROUP)]
    for g, ((_, dil), c_ref, o_ref) in enumerate(zip(ATTN_GROUPS, c_refs, o_refs)):
        buf_len = c_ref.shape[-1]
        lane = lax.broadcasted_iota(jnp.int32, (1, buf_len), 1)
        used = (lane % dil) == 0
        last = lane == buf_len - 1
        for h in range(HEADS_PER_GROUP):
            kk = c_ref[0, h]
            vv = c_ref[1, h]
            qc, knc, vnc = q_ref[g, h], kn_ref[g, h], vn_ref[g, h]
            s = jnp.where(used, jnp.sum(kk * qc, axis=0, keepdims=True) * scale, NEG_BIG)
            sn = jnp.sum(knc * qc, axis=0, keepdims=True) * scale
            m = jnp.maximum(jnp.max(s, axis=1, keepdims=True), sn)
            p = jnp.exp(s - m)
            pn = jnp.exp(sn - m)
            den = jnp.sum(p, axis=1, keepdims=True) + pn
            outs[h][g] = (jnp.sum(vv * p, axis=1, keepdims=True) + pn * vnc) / den
            lses[h][g] = m + jnp.log(den)
            o_ref[0, h] = jnp.where(last, knc, pltpu.roll(kk, buf_len - 1, 1))
            o_ref[1, h] = jnp.where(last, vnc, pltpu.roll(vv, buf_len - 1, 1))
    for h in range(HEADS_PER_GROUP):
        top = functools.reduce(jnp.maximum, lses[h])
        ws = [jnp.exp(t - top) for t in lses[h]]
        y_ref[h] = sum(w * o for w, o in zip(ws, outs[h])) / sum(ws)


def _attn_roll(q, kn, vn, caches_t, prev, l):
    n = q.shape[0]
    ng = len(ATTN_GROUPS)
    small = pl.BlockSpec((None, ng, HEADS_PER_GROUP, HEAD_DIM, 1), lambda b: (b, 0, 0, 0, 0))
    c_specs = []
    for (win, dil), c in zip(ATTN_GROUPS, caches_t):
        assert c.shape[-1] == win and win % dil == 0
        c_specs.append(pl.BlockSpec((None, None) + c.shape[2:], lambda b: (l, b, 0, 0, 0, 0)))
    in_specs = [small, small, small] + c_specs
    args = [q, kn, vn, *caches_t]
    aliases = {}
    if prev is not None:
        in_specs += [pl.BlockSpec(memory_space=pl.ANY)] * ng
        aliases = {len(args) + g: 1 + g for g in range(ng)}
        args += list(prev)
    return pl.pallas_call(
        _attn_roll_body,
        grid=(n,),
        in_specs=in_specs,
        out_specs=[pl.BlockSpec((None, HEADS_PER_GROUP, HEAD_DIM, 1), lambda b: (b, 0, 0, 0))] + c_specs,
        out_shape=[jax.ShapeDtypeStruct((n, HEADS_PER_GROUP, HEAD_DIM, 1), F32)]
        + [jax.ShapeDtypeStruct(c.shape, c.dtype) for c in caches_t],
        input_output_aliases=aliases,
        compiler_params=_cparams(("arbitrary",), 48),
        name="attn_roll_sample",
    )(*args)


def _merge_body(*refs, sample):
    if sample:
        (x_ref, mod_ref, g_ref, wg_ref, bg_ref, ys_ref, yc_ref, ya_ref,
         ws_ref, wc_ref, wa_ref, wo_ref, o_ref) = refs
    else:
        (x_ref, mod_ref, g_ref, wg_ref, bg_ref, ys_ref, yc_ref,
         *attn_refs, ws_ref, wc_ref, wa_ref, wo_ref, o_ref) = refs
    precise = sample
    d = D_MODEL
    x = x_ref[...]
    h = _norm_mod(x, g_ref[...], mod_ref[1], mod_ref[0])
    hm = h if precise else h.astype(BF16)
    if sample:
        ya = ya_ref[...]
    else:
        halves = []
        for hp in range(2):
            o0, l0, o1, l1, o2, l2 = [attn_refs[4 * g + s_ + hp][...] for g in range(3) for s_ in (0, 2)]
            top = jnp.maximum(jnp.maximum(l0, l1), l2)
            e0, e1, e2 = jnp.exp(l0 - top), jnp.exp(l1 - top), jnp.exp(l2 - top)
            halves.append((e0 * o0 + e1 * o1 + e2 * o2) / (e0 + e1 + e2))
        ya = jnp.concatenate(halves, axis=1)
    merged = None
    for bi, (y, w_ref) in enumerate(((ys_ref[...], ws_ref), (yc_ref[...], wc_ref), (ya, wa_ref))):
        gate = _sigmoid(_mm(hm, wg_ref[:, bi * d:(bi + 1) * d], precise) + bg_ref[:, bi * d:(bi + 1) * d])
        part = gate * _mm(y, w_ref[...], precise)
        merged = part if merged is None else merged + part
    o_ref[...] = x + mod_ref[2] * _mm(merged, wo_ref[...], precise)


def _merge_prompt(x, modp, l, gnorm, w_gate, b_gate, y_ssm, y_conv, attn, w_bs, w_bc, w_ba, w_out):
    b_sz, seq, d = x.shape
    tm = TM_MERGE
    gw = ATTN_OUT
    row = lambda w: pl.BlockSpec((None, tm, w), lambda b, i: (b, i, 0))
    lsel = lambda b, i: (l, 0, 0)
    wsp = lambda r, c: pl.BlockSpec((None, r, c), lsel)
    in_specs = [
        row(d),
        pl.BlockSpec((None, None, 6, 1, d), lambda b, i: (l, b, 0, 0, 0)),
        wsp(1, d), wsp(d, 3 * d), wsp(1, 3 * d),
        row(SSM_WIDTH), row(CONV_WIDTH),
    ] + [row(LANES)] * 12 + [wsp(SSM_WIDTH, d), wsp(CONV_WIDTH, d), wsp(ATTN_OUT, d), wsp(d, d)]
    flat = [t for group in attn for t in group]
    return pl.pallas_call(
        functools.partial(_merge_body, sample=False),
        grid=(b_sz, seq // tm),
        in_specs=in_specs,
        out_specs=row(d),
        out_shape=jax.ShapeDtypeStruct((b_sz, seq, d), F32),
        compiler_params=_cparams(("arbitrary", "arbitrary"), 56),
        name="merge_prompt",
    )(x, modp, gnorm, w_gate, b_gate, y_ssm, y_conv, *flat, w_bs, w_bc, w_ba, w_out)


def _merge_sample(x, mods, l, gnorm, w_gate, b_gate, y_ssm, y_conv, y_attn, w_bs, w_bc, w_ba, w_out):
    n, d = x.shape
    full = lambda *shape: pl.BlockSpec(shape, lambda i: (0,) * len(shape))
    lsel = lambda i: (l, 0, 0)
    wsp = lambda r, c: pl.BlockSpec((None, r, c), lsel)
    in_specs = [
        full(n, d),
        pl.BlockSpec((None, 6, n, d), lambda i: (l, 0, 0, 0)),
        wsp(1, d), wsp(d, 3 * d), wsp(1, 3 * d),
        full(n, SSM_WIDTH), full(n, CONV_WIDTH), full(n, ATTN_OUT),
        wsp(SSM_WIDTH, d), wsp(CONV_WIDTH, d), wsp(ATTN_OUT, d), wsp(d, d),
    ]
    return pl.pallas_call(
        functools.partial(_merge_body, sample=True),
        grid=(1,),
        in_specs=in_specs,
        out_specs=full(n, d),
        out_shape=jax.ShapeDtypeStruct((n, d), F32),
        compiler_params=_cparams(("arbitrary",), 56),
        name="merge_sample",
    )(x, mods, gnorm, w_gate, b_gate, y_ssm, y_conv, y_attn, w_bs, w_bc, w_ba, w_out)


def _shift_rows(x, hist, k):
    r = pltpu.roll(x, k, 0)
    row = lax.broadcasted_iota(jnp.int32, hist.shape, 0)
    head = jnp.where(row < k, pltpu.roll(hist, k, 0), r[0:SUBLANES])
    return jnp.concatenate([head, r[SUBLANES:]], axis=0)


def _ffn_prompt_body(x_ref, mod_ref, g_ref, wup_ref, cw_ref, cb_ref, wdn_ref, o_ref, tail_ref,
                     hist_ref, hb_ref, up_ref, act_ref):
    i = pl.program_id(1)
    rows = x_ref.shape[0]
    f2 = 2 * FFN_HIDDEN

    @pl.when(i == 0)
    def _():
        hist_ref[...] = jnp.zeros((HIST, f2), F32)

    x = x_ref[...]
    hb_ref[...] = _norm_mod(x, g_ref[...], mod_ref[4], mod_ref[3]).astype(BF16)
    acc = jnp.zeros((rows, D_MODEL), F32)
    cw = FFN_CHUNK
    n_chunks = FFN_HIDDEN // cw

    def project(c):
        for half, base in enumerate((c * cw, FFN_HIDDEN + c * cw)):
            up_ref[c % 2, half] = jnp.dot(hb_ref[...], wup_ref[:, base:base + cw],
                                          preferred_element_type=F32)

    def contract(c, acc):
        return acc + jnp.dot(act_ref[c % 2], wdn_ref[c * cw:(c + 1) * cw, :], preferred_element_type=F32)

    project(0)
    for c in range(n_chunks):
        if c + 1 < n_chunks:
            project(c + 1)
        if c > 0:
            acc = contract(c - 1, acc)
        halves = []
        for half, base in enumerate((c * cw, FFN_HIDDEN + c * cw)):
            cols = slice(base, base + cw)
            up = up_ref[c % 2, half]
            hist = hist_ref[:, cols]
            hist_ref[:, cols] = up[rows - HIST:rows]
            halves.append(cw_ref[0:1, cols] * _shift_rows(up, hist, 2)
                          + cw_ref[1:2, cols] * _shift_rows(up, hist, 1)
                          + cw_ref[2:3, cols] * up + cb_ref[:, cols])
        act_ref[c % 2] = (_gelu_tanh(halves[0]) * halves[1]).astype(BF16)
    acc = contract(n_chunks - 1, acc)
    o_ref[...] = x_ref[...] + mod_ref[5] * acc
    tail_ref[...] = hist_ref[...]


def _ffn_prompt(x, modp, l, gnorm, w_up, conv_w, conv_b, w_down):
    b_sz, seq, d = x.shape
    tm = TM_FFN
    f2 = 2 * FFN_HIDDEN
    lsel = lambda b, i: (l, 0, 0)
    once = pl.Buffered(1)
    return pl.pallas_call(
        _ffn_prompt_body,
        grid=(b_sz, seq // tm),
        in_specs=[
            pl.BlockSpec((None, tm, d), lambda b, i: (b, i, 0)),
            pl.BlockSpec((None, None, 6, 1, d), lambda b, i: (l, b, 0, 0, 0)),
            pl.BlockSpec((None, 1, d), lsel),
            pl.BlockSpec((None, d, f2), lsel, pipeline_mode=once),
            pl.BlockSpec((None, FFN_CONV_K, f2), lsel),
            pl.BlockSpec((None, 1, f2), lsel),
            pl.BlockSpec((None, FFN_HIDDEN, d), lsel, pipeline_mode=once),
        ],
        out_specs=[
            pl.BlockSpec((None, tm, d), lambda b, i: (b, i, 0)),
            pl.BlockSpec((None, HIST, f2), lambda b, i: (b, 0, 0)),
        ],
        out_shape=[jax.ShapeDtypeStruct((b_sz, seq, d), F32),
                   jax.ShapeDtypeStruct((b_sz, HIST, f2), F32)],
        scratch_shapes=[pltpu.VMEM((HIST, f2), F32),
                        pltpu.VMEM((tm, d), BF16),
                        pltpu.VMEM((2, 2, tm, FFN_CHUNK), F32),
                        pltpu.VMEM((2, tm, FFN_CHUNK), BF16)],
        compiler_params=_cparams(("arbitrary", "arbitrary"), 56),
        name="ffn_prompt",
    )(x, modp, gnorm, w_up, conv_w, conv_b, w_down)


def _ffn_sample_body(x_ref, mod_ref, g_ref, wup_ref, c0_ref, c1_ref, cw_ref, cb_ref, wdn_ref, o_ref, up_ref):
    x = x_ref[...]
    h2 = _norm_mod(x, g_ref[...], mod_ref[4], mod_ref[3])
    up = _mm(h2, wup_ref[...], True)
    up_ref[...] = up
    cv = cw_ref[0:1, :] * c0_ref[...] + cw_ref[1:2, :] * c1_ref[...] + cw_ref[2:3, :] * up + cb_ref[...]
    act = _gelu_tanh(cv[:, 0:FFN_HIDDEN]) * cv[:, FFN_HIDDEN:2 * FFN_HIDDEN]
    o_ref[...] = x + mod_ref[5] * _mm(act, wdn_ref[...], True)


def _ffn_sample(x, mods, l, gnorm, w_up, c0, c1, conv_w, conv_b, w_down):
    n, d = x.shape
    f2 = 2 * FFN_HIDDEN
    full = lambda *shape: pl.BlockSpec(shape, lambda i: (0,) * len(shape))
    lsel = lambda i: (l, 0, 0)
    once = pl.Buffered(1)
    return pl.pallas_call(
        _ffn_sample_body,
        grid=(1,),
        in_specs=[
            full(n, d),
            pl.BlockSpec((None, 6, n, d), lambda i: (l, 0, 0, 0)),
            pl.BlockSpec((None, 1, d), lsel),
            pl.BlockSpec((None, d, f2), lsel, pipeline_mode=once),
            full(n, f2), full(n, f2),
            pl.BlockSpec((None, FFN_CONV_K, f2), lsel),
            pl.BlockSpec((None, 1, f2), lsel),
            pl.BlockSpec((None, FFN_HIDDEN, d), lsel, pipeline_mode=once),
        ],
        out_specs=[full(n, d), full(n, f2)],
        out_shape=[jax.ShapeDtypeStruct((n, d), F32), jax.ShapeDtypeStruct((n, f2), F32)],
        compiler_params=_cparams(("arbitrary",), 56),
        name="ffn_sample",
    )(x, mods, gnorm, w_up, c0, c1, conv_w, conv_b, w_down)


def _rope_tables(pos):
    half = HEAD_DIM // 2
    inv = ROPE_THETA ** (-jnp.arange(half, dtype=F32) / half)
    ang = pos.astype(F32)[:, None] * inv[None, :]
    cos, sin = jnp.cos(ang), jnp.sin(ang)
    cos_h = jnp.concatenate([cos, cos], axis=1)
    sin_h = jnp.concatenate([-sin, sin], axis=1)
    reps = LANES // HEAD_DIM
    return jnp.tile(cos_h, (1, reps)), jnp.tile(sin_h, (1, reps))


def _residue_perm(rows, dil):
    i = jnp.arange(rows)
    src = (i % (rows // dil)) * dil + i // (rows // dil)
    return (src[:, None] == jnp.arange(rows)[None, :]).astype(BF16)


def _block_ones(dtype):
    i = jnp.arange(MXU_DIM) // HEAD_DIM
    return (i[:, None] == i[None, :]).astype(dtype)


def kernel(x_prompt, x_sample, state_ssm, cache_conv, cache_kv_w128, cache_kv_w512, cache_kv_w2048, cache_ffn, c_prompt, c_sample, w_ada, b_ada, g_norm_mix, w_in, ssm_a_re, ssm_a_im, ssm_log_dt, ssm_b_re, ssm_b_im, ssm_c_re, ssm_c_im, ssm_d, ssm_w_glu, ssm_b_glu, conv_w, conv_b, conv_ln_g, conv_ln_b, attn_gq, attn_gk, w_gate, b_gate, w_br_ssm, w_br_conv, w_br_attn, w_out, g_norm_ffn, ffn_w_up, ffn_conv_w, ffn_conv_b, ffn_w_down):
    bp, seq, d = x_prompt.shape
    ns = x_sample.shape[0]
    depth = w_ada.shape[0]
    f2 = 2 * FFN_HIDDEN
    assert x_sample.shape[1] == 1
    assert all(seq % (ATTN_BLOCK * dil * ATTN_STEP_BLOCKS[dil]) == 0 for _, dil in ATTN_GROUPS)
    caches = (cache_kv_w128, cache_kv_w512, cache_kv_w2048)

    pad = (-(bp + ns)) % SUBLANES
    c_all = jnp.concatenate([c_prompt, c_sample, jnp.zeros((pad, d), F32)], axis=0)
    mod = _ada_call(c_all, w_ada, b_ada)
    modp = mod[:, :bp].reshape(depth, bp, 6, 1, d)
    mods = mod[:, bp:bp + ns].reshape(depth, ns, 6, d).transpose(0, 2, 1, 3)

    bf = lambda w: w.astype(BF16)
    w_in_b, w_gate_b, w_out_b = bf(w_in), bf(w_gate), bf(w_out)
    w_bs_b, w_bc_b, w_ba_b = bf(w_br_ssm), bf(w_br_conv), bf(w_br_attn)
    w_up_b, w_dn_b, w_glu_b = bf(ffn_w_up), bf(ffn_w_down), bf(ssm_w_glu)

    row3 = lambda t: t.reshape(depth, 1, t.shape[-1])
    g_mix, g_ffn = row3(g_norm_mix), row3(g_norm_ffn)
    gq = row3(jnp.tile(attn_gq, (1, ATTN_HEADS)))
    gk = row3(jnp.tile(attn_gk, (1, ATTN_HEADS)))
    b_gate3, ssm_d3, b_glu3 = row3(b_gate), row3(ssm_d), row3(ssm_b_glu)
    conv_b3, ln_g3, ln_b3, ffn_cb3 = row3(conv_b), row3(conv_ln_g), row3(conv_ln_b), row3(ffn_conv_b)
    conv_w4 = conv_w.reshape(depth, CONV_K, 1, CONV_WIDTH)

    cos_p, sin_p = _rope_tables(jnp.arange(seq, dtype=jnp.int32))
    cos_s, sin_s = _rope_tables(jnp.full((ns,), PAST_LEN, dtype=jnp.int32))
    ones_b, ones_f = _block_ones(BF16), _block_ones(F32)
    p4 = _residue_perm(TM_INPROJ, ATTN_GROUPS[1][1])
    p16 = _residue_perm(TM_INPROJ, ATTN_GROUPS[2][1])
    seg = L_SSM // SUBLANES
    pm = _residue_perm(L_SSM, seg)
    pmt = pm.T
    zero_state = jnp.zeros((bp, 1, STATE_W), F32)

    yp = x_prompt
    ys = x_sample.reshape(ns, d)
    ssm_p, ssm_s, conv_p, conv_s, ffn_p, ffn_s = [], [], [], [], [], []
    kv_p = [[] for _ in ATTN_GROUPS]
    caches_t = [c.transpose(0, 1, 3, 4, 5, 2) for c in caches]
    rolled = None
    gw = HEADS_PER_GROUP * HEAD_DIM
    for l in range(depth):
        tabs = _ssm_tables(ssm_a_re[l], ssm_a_im[l], ssm_log_dt[l], ssm_b_re[l], ssm_b_im[l],
                           ssm_c_re[l], ssm_c_im[l], seg)

        (zs, zc, k_tail, v_tail, q0, k0, v0, q1, k1, v1, q2, k2, v2) = _inproj_prompt(
            yp, modp, l, g_mix, w_in_b, gq, gk, cos_p, sin_p, ones_b, p4, p16)
        y_ssm, st = _ssm_prompt(zs, zero_state, l, pm, pmt, tabs, ssm_d3, w_glu_b, b_glu3)
        y_conv, a_tail = _conv_prompt(zc, l, conv_w, conv_b3, ln_g3, ln_b3)
        attn = [_attn_prompt(q0, k0, v0, 1), _attn_prompt(q1, k1, v1, ATTN_GROUPS[1][1]),
                _attn_prompt(q2, k2, v2, ATTN_GROUPS[2][1])]
        x_mid = _merge_prompt(yp, modp, l, g_mix, w_gate_b, b_gate3, y_ssm, y_conv, attn,
                              w_bs_b, w_bc_b, w_ba_b, w_out_b)
        yp, up_tail = _ffn_prompt(x_mid, modp, l, g_ffn, w_up_b, ffn_conv_w, ffn_cb3, w_dn_b)
        ssm_p.append(_slab_to_state(st.reshape(bp, STATE_W)))
        conv_p.append(a_tail[:, CONV_HALO - (CONV_K - 1):])
        ffn_p.append(up_tail[:, HIST - (FFN_CONV_K - 1):])
        for gi, (win, _) in enumerate(ATTN_GROUPS):
            keep = min(win, seq)
            cols = slice(gi * gw, (gi + 1) * gw)
            kk = k_tail[:, -keep:, cols].reshape(bp, keep, HEADS_PER_GROUP, HEAD_DIM)
            vv = v_tail[:, -keep:, cols].reshape(bp, keep, HEADS_PER_GROUP, HEAD_DIM)
            kv_p[gi].append(jnp.stack([kk, vv], axis=2))

        zs_s, zc_s, q_s, k_s, v_s = _inproj_sample(ys, mods, l, g_mix, w_in, gq, gk, cos_s, sin_s, ones_f)
        y_ssm_s, st_s = _ssm_sample(zs_s, _state_to_slab(state_ssm[l]), l, tabs, ssm_d3, ssm_w_glu, b_glu3)
        hist_t = cache_conv[l].transpose(1, 0, 2)
        y_conv_s, a_s = _conv_sample(zc_s, hist_t, l, conv_w4, conv_b3, ln_g3, ln_b3)
        cols = lambda t: t.reshape(ns, len(ATTN_GROUPS), HEADS_PER_GROUP, HEAD_DIM, 1)
        y_attn_s, *rolled = _attn_roll(cols(q_s), cols(k_s), cols(v_s), caches_t, rolled, l)
        y_attn_s = y_attn_s.reshape(ns, ATTN_OUT)
        x_mid_s = _merge_sample(ys, mods, l, g_mix, w_gate, b_gate3, y_ssm_s, y_conv_s, y_attn_s,
                                w_br_ssm, w_br_conv, w_br_attn, w_out)
        c_old, c_new = cache_ffn[l, :, 0], cache_ffn[l, :, 1]
        ys, up_s = _ffn_sample(x_mid_s, mods, l, g_ffn, ffn_w_up, c_old, c_new, ffn_conv_w, ffn_cb3, ffn_w_down)
        ssm_s.append(_slab_to_state(st_s))
        conv_s.append(jnp.concatenate([cache_conv[l][:, 1:], a_s[:, None, :]], axis=1))
        ffn_s.append(jnp.stack([c_new, up_s], axis=1))

    kv_s = [t.transpose(0, 1, 5, 2, 3, 4) for t in rolled]
    st = lambda t: jnp.stack(t, axis=0)
    return (yp, ys.reshape(ns, 1, d), st(ssm_p), st(ssm_s), st(conv_p), st(conv_s),
            st(kv_p[0]), kv_s[0], st(kv_p[1]), kv_s[1], st(kv_p[2]), kv_s[2], st(ffn_p), st(ffn_s))
```

```python
import functools
import math

import jax
import jax.numpy as jnp
from jax import lax
from jax.experimental import pallas as pl
from jax.experimental.pallas import tpu as pltpu

F32 = jnp.float32
BF16 = jnp.bfloat16

D_MODEL = 1024
SSM_WIDTH = 512
SSM_GROUP = 16
SSM_GROUPS = 32
SSM_N = 64
CONV_WIDTH = 512
CONV_K = 31
HEAD_DIM = 64
HEADS_PER_GROUP = 4
ATTN_GROUPS = ((128, 1), (512, 4), (2048, 16))
ATTN_HEADS = 12
ATTN_WIDTH = 768
ATTN_OUT = 256
ATTN_BLOCK = 128
ROPE_THETA = 10000.0
FFN_HIDDEN = 2816
FFN_CONV_K = 3
IN_WIDTH = SSM_WIDTH + 2 * CONV_WIDTH + 3 * ATTN_WIDTH
EPS = 1e-6
PAST_LEN = 8192

SUBLANES = 8
LANES = 128
MXU_DIM = 256
SLAB_GROUPS = 8
N_SLABS = SSM_GROUPS // SLAB_GROUPS
SLAB_CH = SLAB_GROUPS * SSM_GROUP
SLAB_ST = SLAB_GROUPS * SSM_N
STATE_W = 2 * SSM_GROUPS * SSM_N
NEG_BIG = -1e30
GELU_C = math.sqrt(2.0 / math.pi)

TM_INPROJ = 512
L_SSM = 256
TM_MERGE = 256
TM_FFN = 256
FFN_CHUNK = 256
HIST = 8
CONV_HALO = 32
CONV_ROW_BLOCK = 32
KV_TAIL = 2048
ATTN_STEP_BLOCKS = {1: 4, 4: 4, 16: 4}


def _cparams(sem, vmem_mb):
    return pltpu.CompilerParams(dimension_semantics=sem, vmem_limit_bytes=vmem_mb * 1024 * 1024)


def _sigmoid(x):
    return 1.0 / (1.0 + jnp.exp(-x))


def _gelu_tanh(x):
    neg_2u = x * ((-2.0 * GELU_C * 0.044715) * (x * x) - 2.0 * GELU_C)
    return x / (1.0 + jnp.exp(neg_2u))


def _mm(a, w, precise):
    if precise:
        return jnp.dot(a, w, precision=lax.Precision.HIGHEST, preferred_element_type=F32)
    return jnp.dot(a.astype(BF16), w, preferred_element_type=F32)


def _norm_mod(x, g, scale, shift):
    ms = jnp.mean(x * x, axis=-1, keepdims=True)
    return (x * lax.rsqrt(ms + EPS) * g) * (1.0 + scale) + shift


def _ada_body(c_ref, w_ref, b_ref, o_ref):
    c = c_ref[...]
    s = c * _sigmoid(c)
    o_ref[...] = jnp.dot(s, w_ref[...], precision=lax.Precision.HIGHEST,
                         preferred_element_type=F32) + b_ref[...]


def _ada_call(c_all, w_ada, b_ada):
    depth, d, n6 = w_ada.shape
    rows = c_all.shape[0]
    tn = 1536
    return pl.pallas_call(
        _ada_body,
        grid=(depth, n6 // tn),
        in_specs=[
            pl.BlockSpec((rows, d), lambda l, j: (0, 0)),
            pl.BlockSpec((None, d, tn), lambda l, j: (l, 0, j)),
            pl.BlockSpec((None, 1, tn), lambda l, j: (l, 0, j)),
        ],
        out_specs=pl.BlockSpec((None, rows, tn), lambda l, j: (l, 0, j)),
        out_shape=jax.ShapeDtypeStruct((depth, rows, n6), F32),
        compiler_params=_cparams(("arbitrary", "arbitrary"), 40),
        name="ada_mod",
    )(c_all, w_ada, b_ada.reshape(depth, 1, n6))


def _head_norm_rope(z, gvec, ones, cos, sins, precise):
    sq = z * z
    parts = []
    for c in range(ATTN_WIDTH // MXU_DIM):
        parts.append(_mm(sq[:, c * MXU_DIM:(c + 1) * MXU_DIM], ones, precise))
    ms = jnp.concatenate(parts, axis=1) * (1.0 / HEAD_DIM)
    y = z * lax.rsqrt(ms + EPS) * gvec
    lane = lax.broadcasted_iota(jnp.int32, (1, LANES), 1)
    first = (lane % HEAD_DIM) < (HEAD_DIM // 2)
    outs = []
    for c in range(ATTN_WIDTH // LANES):
        yc = y[:, c * LANES:(c + 1) * LANES]
        partner = jnp.where(first, pltpu.roll(yc, LANES - HEAD_DIM // 2, 1),
                            pltpu.roll(yc, HEAD_DIM // 2, 1))
        outs.append(yc * cos + partner * sins)
    return jnp.concatenate(outs, axis=1)


def _inproj_body(*refs, sample, first_tail=0):
    if sample:
        (x_ref, mod_ref, g_ref, w_ref, gq_ref, gk_ref, cos_ref, sin_ref, ones_ref,
         zs_ref, zc_ref, q_ref, k_ref, v_ref) = refs
    else:
        (x_ref, mod_ref, g_ref, w_ref, gq_ref, gk_ref, cos_ref, sin_ref, ones_ref, p4_ref, p16_ref,
         zs_ref, zc_ref, kt_ref, vt_ref,
         q0_ref, k0_ref, v0_ref, q1_ref, k1_ref, v1_ref, q2_ref, k2_ref, v2_ref) = refs
    precise = sample
    x = x_ref[...]
    h = _norm_mod(x, g_ref[...], mod_ref[1], mod_ref[0])
    hm = h if precise else h.astype(BF16)
    c1 = SSM_WIDTH
    c2 = c1 + 2 * CONV_WIDTH
    c3 = c2 + ATTN_WIDTH
    c4 = c3 + ATTN_WIDTH
    zs_ref[...] = _mm(hm, w_ref[:, 0:c1], precise)
    zc_ref[...] = _mm(hm, w_ref[:, c1:c2], precise)
    zq = _mm(hm, w_ref[:, c2:c3], precise)
    zk = _mm(hm, w_ref[:, c3:c4], precise)
    zv = _mm(hm, w_ref[:, c4:IN_WIDTH], precise)
    cos = cos_ref[...]
    sins = sin_ref[...]
    ones = ones_ref[...]
    q = _head_norm_rope(zq, gq_ref[...], ones, cos, sins, precise)
    k = _head_norm_rope(zk, gk_ref[...], ones, cos, sins, precise)
    if sample:
        q_ref[...] = q
        k_ref[...] = k
        v_ref[...] = zv
        return
    @pl.when(pl.program_id(1) >= first_tail)
    def _():
        kt_ref[...] = k.T
        vt_ref[...] = zv.T

    qb = (q * (HEAD_DIM ** -0.5)).astype(BF16)
    kb = k.astype(BF16)
    vb = zv.astype(BF16)
    gw = HEADS_PER_GROUP * HEAD_DIM
    q0_ref[0] = qb[:, 0:gw]
    k0_ref[0] = kb[:, 0:gw]
    v0_ref[0] = vb[:, 0:gw]
    for gi, (p_ref, outs) in enumerate(((p4_ref, (q1_ref, k1_ref, v1_ref)),
                                        (p16_ref, (q2_ref, k2_ref, v2_ref))), start=1):
        pm = p_ref[...]
        for src, o_ref in zip((qb, kb, vb), outs):
            dil, rows = o_ref.shape[0], o_ref.shape[1]
            perm = jnp.dot(pm, src[:, gi * gw:(gi + 1) * gw], preferred_element_type=F32).astype(BF16)
            for r in range(dil):
                o_ref[r] = perm[r * rows:(r + 1) * rows]


def _inproj_prompt(x, modp, l, gnorm, w_in, gq, gk, cos, sins, ones, p4, p16):
    b_sz, seq, d = x.shape
    tm = TM_INPROJ
    nt = seq // tm
    tail = min(KV_TAIL, seq)
    ft = nt - tail // tm
    gw = HEADS_PER_GROUP * HEAD_DIM
    tmap = lambda b, i: (b, 0, jnp.maximum(i - ft, 0))
    in_specs = [
        pl.BlockSpec((None, tm, d), lambda b, i: (b, i, 0)),
        pl.BlockSpec((None, None, 6, 1, d), lambda b, i: (l, b, 0, 0, 0)),
        pl.BlockSpec((None, 1, d), lambda b, i: (l, 0, 0)),
        pl.BlockSpec((None, d, IN_WIDTH), lambda b, i: (l, 0, 0), pipeline_mode=pl.Buffered(1)),
        pl.BlockSpec((None, 1, ATTN_WIDTH), lambda b, i: (l, 0, 0)),
        pl.BlockSpec((None, 1, ATTN_WIDTH), lambda b, i: (l, 0, 0)),
        pl.BlockSpec((tm, LANES), lambda b, i: (i, 0)),
        pl.BlockSpec((tm, LANES), lambda b, i: (i, 0)),
        pl.BlockSpec((MXU_DIM, MXU_DIM), lambda b, i: (0, 0)),
        pl.BlockSpec((tm, tm), lambda b, i: (0, 0)),
        pl.BlockSpec((tm, tm), lambda b, i: (0, 0)),
    ]
    out_shape = [
        jax.ShapeDtypeStruct((b_sz, seq, SSM_WIDTH), F32),
        jax.ShapeDtypeStruct((b_sz, seq, 2 * CONV_WIDTH), F32),
        jax.ShapeDtypeStruct((b_sz, ATTN_WIDTH, tail), F32),
        jax.ShapeDtypeStruct((b_sz, ATTN_WIDTH, tail), F32),
    ]
    out_specs = [
        pl.BlockSpec((None, tm, SSM_WIDTH), lambda b, i: (b, i, 0)),
        pl.BlockSpec((None, tm, 2 * CONV_WIDTH), lambda b, i: (b, i, 0)),
        pl.BlockSpec((None, ATTN_WIDTH, tm), tmap),
        pl.BlockSpec((None, ATTN_WIDTH, tm), tmap),
    ]
    for _, dil in ATTN_GROUPS:
        for _ in range(3):
            out_shape.append(jax.ShapeDtypeStruct((b_sz, dil, seq // dil, gw), BF16))
            out_specs.append(pl.BlockSpec((None, dil, tm // dil, gw), lambda b, i: (b, 0, i, 0)))
    return pl.pallas_call(
        functools.partial(_inproj_body, sample=False, first_tail=ft),
        grid=(b_sz, nt),
        in_specs=in_specs,
        out_specs=out_specs,
        out_shape=out_shape,
        compiler_params=_cparams(("arbitrary", "arbitrary"), 56),
        name="inproj_prompt",
    )(x, modp, gnorm, w_in, gq, gk, cos, sins, ones, p4, p16)


def _inproj_sample(x, mods, l, gnorm, w_in, gq, gk, cos, sins, ones):
    n, d = x.shape
    lsel = lambda i: (l, 0, 0)
    in_specs = [
        pl.BlockSpec((n, d), lambda i: (0, 0)),
        pl.BlockSpec((None, 6, n, d), lambda i: (l, 0, 0, 0)),
        pl.BlockSpec((None, 1, d), lsel),
        pl.BlockSpec((None, d, IN_WIDTH), lsel),
        pl.BlockSpec((None, 1, ATTN_WIDTH), lsel),
        pl.BlockSpec((None, 1, ATTN_WIDTH), lsel),
        pl.BlockSpec((n, LANES), lambda i: (0, 0)),
        pl.BlockSpec((n, LANES), lambda i: (0, 0)),
        pl.BlockSpec((MXU_DIM, MXU_DIM), lambda i: (0, 0)),
    ]
    widths = (SSM_WIDTH, 2 * CONV_WIDTH, ATTN_WIDTH, ATTN_WIDTH, ATTN_WIDTH)
    return pl.pallas_call(
        functools.partial(_inproj_body, sample=True),
        grid=(1,),
        in_specs=in_specs,
        out_specs=[pl.BlockSpec((n, w), lambda i: (0, 0)) for w in widths],
        out_shape=[jax.ShapeDtypeStruct((n, w), F32) for w in widths],
        compiler_params=_cparams(("arbitrary",), 48),
        name="inproj_sample",
    )(x, mods, gnorm, w_in, gq, gk, cos, sins, ones)


def _ssm_tail(y_raw, u, d_ref, wglu_ref, bglu_ref, precise):
    y = _gelu_tanh(y_raw + d_ref[...] * u)
    return y * _sigmoid(_mm(y, wglu_ref[...], precise) + bglu_ref[...])


def _ssm_prompt_body(zs_ref, s0_ref, pm_ref, pmt_ref, b_ref, c_ref, lam_ref, lamp_ref, pw_ref,
                     d_ref, wglu_ref, bglu_ref, y_ref, st_ref,
                     v_ref, sb_ref, cs_ref, carry_ref):
    ci = pl.program_id(1)
    chunk = zs_ref.shape[0]
    seg = chunk // SUBLANES

    @pl.when(ci == 0)
    def _():
        carry_ref[...] = s0_ref[...]

    u = zs_ref[...]
    up = jnp.dot(pm_ref[...], u.astype(BF16), preferred_element_type=F32).astype(BF16)
    re, im = slice(0, SLAB_ST), slice(SLAB_ST, 2 * SLAB_ST)
    for m in range(N_SLABS):
        v_ref[m] = jnp.dot(up[:, m * SLAB_CH:(m + 1) * SLAB_CH], b_ref[m], preferred_element_type=F32)
    y_parts = [None] * N_SLABS
    for pair in range(0, N_SLABS, 2):
        slabs = (pair, pair + 1)
        zero = jnp.zeros((SUBLANES, SLAB_ST), F32)
        state = {m: (zero, zero) for m in slabs}
        for i in range(seg):
            rows = slice(i * SUBLANES, (i + 1) * SUBLANES)
            for m in slabs:
                lr = lam_ref[m, 0]
                li = lam_ref[m, 1]
                sr, si = state[m]
                nsr = lr * sr - li * si + v_ref[m, rows, re]
                nsi = lr * si + li * sr + v_ref[m, rows, im]
                v_ref[m, rows, re] = nsr
                v_ref[m, rows, im] = nsi
                state[m] = (nsr, nsi)
        for m in slabs:
            base = m * 2 * SLAB_ST
            sr, si = state[m]
            er = carry_ref[0:1, base:base + SLAB_ST]
            ei = carry_ref[0:1, base + SLAB_ST:base + 2 * SLAB_ST]
            pr = lamp_ref[m, 0:1, :]
            pi = lamp_ref[m, 1:2, :]
            for r in range(SUBLANES):
                cs_ref[m, r:r + 1, re] = er
                cs_ref[m, r:r + 1, im] = ei
                ner = sr[r:r + 1] + pr * er - pi * ei
                nei = si[r:r + 1] + pr * ei + pi * er
                er, ei = ner, nei
            carry_ref[0:1, base:base + SLAB_ST] = er
            carry_ref[0:1, base + SLAB_ST:base + 2 * SLAB_ST] = ei
        for m in slabs:
            csr = cs_ref[m, :, re]
            csi = cs_ref[m, :, im]
            for i2 in range(seg // 2):
                rows16 = slice(i2 * 2 * SUBLANES, (i2 + 1) * 2 * SUBLANES)
                rows_r, rows_i = [], []
                for h in range(2):
                    i = i2 * 2 + h
                    rows = slice(i * SUBLANES, (i + 1) * SUBLANES)
                    qr = pw_ref[m, 0, rows, :]
                    qi = pw_ref[m, 1, rows, :]
                    rows_r.append(v_ref[m, rows, re] + qr * csr - qi * csi)
                    rows_i.append(v_ref[m, rows, im] + qr * csi + qi * csr)
                sb_ref[m, rows16, re] = jnp.concatenate(rows_r, axis=0).astype(BF16)
                sb_ref[m, rows16, im] = jnp.concatenate(rows_i, axis=0).astype(BF16)
            y_parts[m] = jnp.dot(sb_ref[m], c_ref[m], preferred_element_type=F32)
    y_perm = jnp.concatenate(y_parts, axis=1)
    hi = y_perm.astype(BF16)
    lo = (y_perm - hi.astype(F32)).astype(BF16)
    pmt = pmt_ref[...]
    y_nat = jnp.dot(pmt, hi, preferred_element_type=F32) + jnp.dot(pmt, lo, preferred_element_type=F32)
    y_ref[...] = _ssm_tail(y_nat, u, d_ref, wglu_ref, bglu_ref, False).astype(BF16)
    st_ref[...] = carry_ref[...]


def _ssm_prompt(zs, s0, l, pm, pmt, tabs, ssm_d, w_glu, b_glu):
    b_sz, seq, _ = zs.shape
    chunk = L_SSM
    full = lambda *shape: pl.BlockSpec(shape, lambda b, c: (0,) * len(shape))
    tab = lambda *shape: pl.BlockSpec((None,) + shape, lambda b, c: (l,) + (0,) * len(shape))
    lsel = lambda b, c: (l, 0, 0)
    return pl.pallas_call(
        _ssm_prompt_body,
        grid=(b_sz, seq // chunk),
        in_specs=[
            pl.BlockSpec((None, chunk, SSM_WIDTH), lambda b, c: (b, c, 0)),
            pl.BlockSpec((None, 1, STATE_W), lambda b, c: (b, 0, 0)),
            full(chunk, chunk), full(chunk, chunk),
            tab(N_SLABS, SLAB_CH, 2 * SLAB_ST),
            tab(N_SLABS, 2 * SLAB_ST, SLAB_CH),
            tab(N_SLABS, 2, SUBLANES, SLAB_ST), tab(N_SLABS, 2, SLAB_ST),
            tab(N_SLABS, 2, chunk, SLAB_ST),
            pl.BlockSpec((None, 1, SSM_WIDTH), lsel),
            pl.BlockSpec((None, SSM_WIDTH, SSM_WIDTH), lsel),
            pl.BlockSpec((None, 1, SSM_WIDTH), lsel),
        ],
        out_specs=[
            pl.BlockSpec((None, chunk, SSM_WIDTH), lambda b, c: (b, c, 0)),
            pl.BlockSpec((None, 1, STATE_W), lambda b, c: (b, 0, 0)),
        ],
        out_shape=[jax.ShapeDtypeStruct((b_sz, seq, SSM_WIDTH), BF16),
                   jax.ShapeDtypeStruct((b_sz, 1, STATE_W), F32)],
        scratch_shapes=[
            pltpu.VMEM((N_SLABS, chunk, 2 * SLAB_ST), F32),
            pltpu.VMEM((N_SLABS, chunk, 2 * SLAB_ST), BF16),
            pltpu.VMEM((N_SLABS, SUBLANES, 2 * SLAB_ST), F32),
            pltpu.VMEM((1, STATE_W), F32),
        ],
        compiler_params=_cparams(("arbitrary", "arbitrary"), 40),
        name="ssm_prompt",
    )(zs, s0, pm, pmt, tabs["b_bf"], tabs["c_bf"], tabs["lam_rows"], tabs["lam_seg"], tabs["pw_rows"],
      ssm_d, w_glu, b_glu)


def _ssm_sample_body(u_ref, s0_ref, b_ref, c_ref, lam_ref, d_ref, wglu_ref, bglu_ref, y_ref, st_ref):
    u = u_ref[...]
    y_parts = []
    for m in range(N_SLABS):
        base = m * 2 * SLAB_ST
        v = _mm(u[:, m * SLAB_CH:(m + 1) * SLAB_CH], b_ref[m], True)
        lr = lam_ref[m, 0:1, :]
        li = lam_ref[m, 1:2, :]
        sr0 = s0_ref[:, base:base + SLAB_ST]
        si0 = s0_ref[:, base + SLAB_ST:base + 2 * SLAB_ST]
        sr = lr * sr0 - li * si0 + v[:, 0:SLAB_ST]
        si = lr * si0 + li * sr0 + v[:, SLAB_ST:2 * SLAB_ST]
        st_ref[:, base:base + SLAB_ST] = sr
        st_ref[:, base + SLAB_ST:base + 2 * SLAB_ST] = si
        y_parts.append(_mm(jnp.concatenate([sr, si], axis=1), c_ref[m], True))
    y_raw = jnp.concatenate(y_parts, axis=1)
    y_ref[...] = _ssm_tail(y_raw, u, d_ref, wglu_ref, bglu_ref, True)


def _ssm_sample(zs, s0, l, tabs, ssm_d, w_glu, b_glu):
    n = zs.shape[0]
    full = lambda *shape: pl.BlockSpec(shape, lambda i: (0,) * len(shape))
    tab = lambda *shape: pl.BlockSpec((None,) + shape, lambda i: (l,) + (0,) * len(shape))
    lsel = lambda i: (l, 0, 0)
    return pl.pallas_call(
        _ssm_sample_body,
        grid=(1,),
        in_specs=[
            full(n, SSM_WIDTH), tab(n, STATE_W),
            tab(N_SLABS, SLAB_CH, 2 * SLAB_ST), tab(N_SLABS, 2 * SLAB_ST, SLAB_CH),
            tab(N_SLABS, 2, SLAB_ST),
            pl.BlockSpec((None, 1, SSM_WIDTH), lsel),
            pl.BlockSpec((None, SSM_WIDTH, SSM_WIDTH), lsel),
            pl.BlockSpec((None, 1, SSM_WIDTH), lsel),
        ],
        out_specs=[full(n, SSM_WIDTH), full(n, STATE_W)],
        out_shape=[jax.ShapeDtypeStruct((n, SSM_WIDTH), F32), jax.ShapeDtypeStruct((n, STATE_W), F32)],
        compiler_params=_cparams(("arbitrary",), 32),
        name="ssm_sample",
    )(zs, s0, tabs["b_f32"], tabs["c_f32"], tabs["lam"], ssm_d, w_glu, b_glu)


def _ssm_tables(a_re, a_im, log_dt, b_re, b_im, c_re, c_im, seg):
    depth = a_re.shape[0]
    dt = jnp.exp(log_dt)[..., None]
    xr, xi = a_re * dt, a_im * dt
    mag = jnp.exp(xr)
    lr, li = mag * jnp.cos(xi), mag * jnp.sin(xi)
    den = a_re * a_re + a_im * a_im
    nr, ni = lr - 1.0, li
    cr = (nr * a_re + ni * a_im) / den
    cim = (ni * a_re - nr * a_im) / den
    bbr = cr[..., None] * b_re - cim[..., None] * b_im
    bbi = cr[..., None] * b_im + cim[..., None] * b_re
    eye = jnp.eye(SLAB_GROUPS, dtype=F32)

    def b_slab(t):
        t = t.reshape(depth, N_SLABS, SLAB_GROUPS, SSM_N, SSM_GROUP)
        return jnp.einsum('lmgnc,gh->lmgchn', t, eye).reshape(depth, N_SLABS, SLAB_CH, SLAB_ST)

    def c_slab(t):
        t = t.reshape(depth, N_SLABS, SLAB_GROUPS, SSM_GROUP, SSM_N)
        return jnp.einsum('lmgcn,gh->lmgnhc', t, eye).reshape(depth, N_SLABS, SLAB_ST, SLAB_CH)

    b_mat = jnp.concatenate([b_slab(bbr), b_slab(bbi)], axis=3)
    c_mat = jnp.concatenate([c_slab(c_re), c_slab(-c_im)], axis=2)

    def powers(ks):
        k = jnp.asarray(ks, F32)[:, None, None, None]
        mk = jnp.exp(k * xr[None])
        both = jnp.stack([mk * jnp.cos(k * xi[None]), mk * jnp.sin(k * xi[None])], axis=0)
        return both.reshape(2, len(ks), depth, N_SLABS, SLAB_ST).transpose(2, 3, 0, 1, 4)

    pw = powers(range(1, seg + 1))
    lam = powers([1])
    rep = lambda t: jnp.repeat(t, SUBLANES, axis=3)
    return dict(b_f32=b_mat, c_f32=c_mat, b_bf=b_mat.astype(BF16), c_bf=c_mat.astype(BF16),
                lam=lam[:, :, :, 0], lam_rows=rep(lam), lam_seg=powers([seg])[:, :, :, 0], pw_rows=rep(pw))


def _state_to_slab(s):
    b = s.shape[0]
    return s.reshape(b, N_SLABS, SLAB_GROUPS, SSM_N, 2).transpose(0, 1, 4, 2, 3).reshape(b, STATE_W)


def _slab_to_state(x):
    b = x.shape[0]
    return x.reshape(b, N_SLABS, 2, SLAB_GROUPS, SSM_N).transpose(0, 1, 3, 4, 2).reshape(b, SSM_GROUPS, SSM_N, 2)


def _ln_silu(y, g, b):
    mu = jnp.mean(y, axis=-1, keepdims=True)
    var = jnp.mean(jnp.square(y - mu), axis=-1, keepdims=True)
    t = (y - mu) * lax.rsqrt(var + EPS) * g + b
    return t * _sigmoid(t)


def _conv_tile(zc_ref, w_ref, b_ref, lg_ref, lb_ref, tail_ref, abuf_ref, shift_ref, y_ref, between=()):
    rows = zc_ref.shape[0]

    @pl.when(pl.program_id(1) == 0)
    def _():
        abuf_ref[0:CONV_HALO, :] = jnp.zeros((CONV_HALO, CONV_WIDTH), F32)

    z = zc_ref[...]
    a = z[:, 0:CONV_WIDTH] * _sigmoid(z[:, CONV_WIDTH:2 * CONV_WIDTH])
    abuf_ref[CONV_HALO:CONV_HALO + rows, :] = a
    span = shift_ref.shape[1]
    for c in range(1, SUBLANES):
        shift_ref[c - 1] = abuf_ref[c:c + span, :]
    first = CONV_HALO - (CONV_K - 1)
    blk = CONV_ROW_BLOCK
    n_blk = rows // blk
    pending = list(between)
    every = -(-n_blk // (len(pending) + 1)) if pending else n_blk
    for r in range(n_blk):
        if pending and r % every == 0:
            pending.pop(0)()
        acc = jnp.zeros((blk, CONV_WIDTH), F32) + b_ref[...]
        for j in range(CONV_K):
            c = (first + j) % SUBLANES
            lo = first + j - c + r * blk
            src = abuf_ref[lo:lo + blk, :] if c == 0 else shift_ref[c - 1, lo:lo + blk, :]
            acc = acc + jnp.concatenate([w_ref[j]] * (blk // SUBLANES), axis=0) * src
        y_ref[r * blk:(r + 1) * blk, :] = _ln_silu(acc, lg_ref[...], lb_ref[...]).astype(y_ref.dtype)
    for thunk in pending:
        thunk()
    last = a[rows - CONV_HALO:rows]
    tail_ref[...] = last
    abuf_ref[0:CONV_HALO, :] = last


def _conv_sample_body(zc_ref, hist_ref, w_ref, b_ref, lg_ref, lb_ref, y_ref, a_ref):
    z = zc_ref[...]
    a = z[:, 0:CONV_WIDTH] * _sigmoid(z[:, CONV_WIDTH:2 * CONV_WIDTH])
    a_ref[...] = a
    acc = b_ref[...] + w_ref[CONV_K - 1] * a
    for j in range(CONV_K - 1):
        acc = acc + w_ref[j] * hist_ref[j]
    y_ref[...] = _ln_silu(acc, lg_ref[...], lb_ref[...])


def _conv_sample(zc, hist_t, l, conv_w4, conv_b, ln_g, ln_b):
    n = zc.shape[0]
    full = lambda *shape: pl.BlockSpec(shape, lambda i: (0,) * len(shape))
    lsel = lambda i: (l, 0, 0)
    return pl.pallas_call(
        _conv_sample_body,
        grid=(1,),
        in_specs=[
            full(n, 2 * CONV_WIDTH),
            pl.BlockSpec((None, CONV_K - 1, n, CONV_WIDTH), lambda i: (l, 0, 0, 0)),
            pl.BlockSpec((None, CONV_K, 1, CONV_WIDTH), lambda i: (l, 0, 0, 0)),
            pl.BlockSpec((None, 1, CONV_WIDTH), lsel),
            pl.BlockSpec((None, 1, CONV_WIDTH), lsel),
            pl.BlockSpec((None, 1, CONV_WIDTH), lsel),
        ],
        out_specs=[full(n, CONV_WIDTH), full(n, CONV_WIDTH)],
        out_shape=[jax.ShapeDtypeStruct((n, CONV_WIDTH), F32)] * 2,
        compiler_params=_cparams(("arbitrary",), 32),
        name="conv_sample",
    )(zc, hist_t, conv_w4, conv_b, ln_g, ln_b)


def _attn_prompt_body(q_ref, kc_ref, kp_ref, vc_ref, vp_ref, oa_ref, ob_ref, la_ref, lb_ref, *, dil):
    j = pl.program_id(1)
    r = pl.program_id(2)
    blk = ATTN_BLOCK
    nblk = q_ref.shape[0] // blk
    q_all = q_ref[...]
    k_all = jnp.concatenate([kp_ref[...], kc_ref[...]], axis=0)
    v_all = jnp.concatenate([vp_ref[...], vc_ref[...]], axis=0)
    qi = lax.broadcasted_iota(jnp.int32, (blk, 2 * blk), 0)
    ki = lax.broadcasted_iota(jnp.int32, (blk, 2 * blk), 1)
    dist = qi + blk - ki
    band = jnp.where(dist >= 0, jnp.where(dist <= blk, 1, 0), 0)
    first_ok = jnp.where(ki >= blk, 1, jnp.where(j > 0, 1, 0))
    lane = lax.broadcasted_iota(jnp.int32, (1, LANES), 1)
    lo = lane < HEAD_DIM
    for s_blk in range(nblk):
        q = q_all[s_blk * blk:(s_blk + 1) * blk]
        k = k_all[s_blk * blk:(s_blk + 2) * blk]
        v = v_all[s_blk * blk:(s_blk + 2) * blk]
        valid = ((band * first_ok) if s_blk == 0 else band) > 0
        o_parts, lse_parts = [], []
        for hp in range(HEADS_PER_GROUP // 2):
            sl = slice(hp * LANES, (hp + 1) * LANES)
            q2, k2, v2 = q[:, sl], k[:, sl], v[:, sl]
            res = []
            for half in range(2):
                keep = lo if half == 0 else jnp.logical_not(lo)
                qm = jnp.where(keep, q2, jnp.zeros_like(q2))
                s = lax.dot_general(qm, k2, (((1,), (1,)), ((), ())), preferred_element_type=F32)
                s = jnp.where(valid, s, NEG_BIG)
                m = jnp.max(s, axis=-1, keepdims=True)
                p = jnp.exp(s - m)
                den = jnp.sum(p, axis=-1, keepdims=True)
                o = jnp.dot(p.astype(BF16), v2, preferred_element_type=F32)
                res.append((o / den, m + jnp.log(den)))
            o_parts.append(jnp.where(lo, res[0][0], res[1][0]))
            lse_parts.append(jnp.where(lo, res[0][1], res[1][1]))
        for val, ref in zip(o_parts + lse_parts, (oa_ref, ob_ref, la_ref, lb_ref)):
            if dil == 1:
                ref[s_blk * blk:(s_blk + 1) * blk, :] = val
            else:
                ref[pl.ds(r + s_blk * blk * dil, blk, stride=dil), :] = val


def _attn_prompt(q, k, v, dil):
    b_sz, _, m_len, gw = q.shape
    blk = ATTN_BLOCK
    seq = m_len * dil
    nblk = ATTN_STEP_BLOCKS[dil]
    cur = lambda b, j, r: (b, r, j, 0)
    prev = lambda b, j, r: (b, r, jnp.maximum(j * nblk - 1, 0), 0)
    bs = lambda f: pl.BlockSpec((None, None, nblk * blk, gw), f)
    bp = lambda f: pl.BlockSpec((None, None, blk, gw), f)
    span = nblk * blk * dil
    return pl.pallas_call(
        functools.partial(_attn_prompt_body, dil=dil),
        grid=(b_sz, m_len // (nblk * blk), dil),
        in_specs=[bs(cur), bs(cur), bp(prev), bs(cur), bp(prev)],
        out_specs=[pl.BlockSpec((None, span, LANES), lambda b, j, r: (b, j, 0))] * 4,
        out_shape=[jax.ShapeDtypeStruct((b_sz, seq, LANES), F32)] * 4,
        compiler_params=_cparams(("arbitrary", "arbitrary", "arbitrary"), 48),
        name=f"attn_prompt_d{dil}",
    )(q, k, k, v, v)


def _attn_roll_body(*refs):
    ng = len(ATTN_GROUPS)
    q_ref, kn_ref, vn_ref = refs[:3]
    c_refs = refs[3:3 + ng]
    y_ref = refs[-1 - ng]
    o_refs = refs[-ng:]
    scale = HEAD_DIM ** -0.5
    outs = [[None] * ng for _ in range(HEADS_PER_GROUP)]
    lses = [[None] * ng for _ in range(HEADS_PER_GROUP)]
    for g, ((_, dil), c_ref, o_ref) in enumerate(zip(ATTN_GROUPS, c_refs, o_refs)):
        buf_len = c_ref.shape[-1]
        lane = lax.broadcasted_iota(jnp.int32, (1, buf_len), 1)
        used = (lane % dil) == 0
        last = lane == buf_len - 1
        for h in range(HEADS_PER_GROUP):
            kk = c_ref[0, h]
            vv = c_ref[1, h]
            qc, knc, vnc = q_ref[g, h], kn_ref[g, h], vn_ref[g, h]
            s = jnp.where(used, jnp.sum(kk * qc, axis=0, keepdims=True) * scale, NEG_BIG)
            sn = jnp.sum(knc * qc, axis=0, keepdims=True) * scale
            m = jnp.maximum(jnp.max(s, axis=1, keepdims=True), sn)
            p = jnp.exp(s - m)
            pn = jnp.exp(sn - m)
            den = jnp.sum(p, axis=1, keepdims=True) + pn
            outs[h][g] = (jnp.sum(vv * p, axis=1, keepdims=True) + pn * vnc) / den
            lses[h][g] = m + jnp.log(den)
            o_ref[0, h] = jnp.where(last, knc, pltpu.roll(kk, buf_len - 1, 1))
            o_ref[1, h] = jnp.where(last, vnc, pltpu.roll(vv, buf_len - 1, 1))
    for h in range(HEADS_PER_GROUP):
        top = functools.reduce(jnp.maximum, lses[h])
        ws = [jnp.exp(t - top) for t in lses[h]]
        y_ref[h] = sum(w * o for w, o in zip(ws, outs[h])) / sum(ws)


def _attn_roll(q, kn, vn, caches_t, prev, l):
    n = q.shape[0]
    ng = len(ATTN_GROUPS)
    small = pl.BlockSpec((None, ng, HEADS_PER_GROUP, HEAD_DIM, 1), lambda b: (b, 0, 0, 0, 0))
    c_specs = []
    for (win, dil), c in zip(ATTN_GROUPS, caches_t):
        assert c.shape[-1] == win and win % dil == 0
        c_specs.append(pl.BlockSpec((None, None) + c.shape[2:], lambda b: (l, b, 0, 0, 0, 0)))
    in_specs = [small, small, small] + c_specs
    args = [q, kn, vn, *caches_t]
    aliases = {}
    if prev is not None:
        in_specs += [pl.BlockSpec(memory_space=pl.ANY)] * ng
        aliases = {len(args) + g: 1 + g for g in range(ng)}
        args += list(prev)
    return pl.pallas_call(
        _attn_roll_body,
        grid=(n,),
        in_specs=in_specs,
        out_specs=[pl.BlockSpec((None, HEADS_PER_GROUP, HEAD_DIM, 1), lambda b: (b, 0, 0, 0))] + c_specs,
        out_shape=[jax.ShapeDtypeStruct((n, HEADS_PER_GROUP, HEAD_DIM, 1), F32)]
        + [jax.ShapeDtypeStruct(c.shape, c.dtype) for c in caches_t],
        input_output_aliases=aliases,
        compiler_params=_cparams(("arbitrary",), 48),
        name="attn_roll_sample",
    )(*args)


def _merge_body(*refs, sample):
    if sample:
        (x_ref, mod_ref, g_ref, wg_ref, bg_ref, ys_ref, yc_ref, ya_ref,
         ws_ref, wc_ref, wa_ref, wo_ref, o_ref) = refs
    else:
        (x_ref, mod_ref, g_ref, wg_ref, bg_ref, ys_ref, zc_ref, cw_ref, cb_ref, lg_ref, lb_ref,
         *attn_refs, ws_ref, wc_ref, wa_ref, wo_ref, o_ref, tail_ref,
         abuf_ref, shift_ref, ycs_ref, hm_ref, gate_ref) = refs
    precise = sample
    d = D_MODEL
    x = x_ref[...]
    h = _norm_mod(x, g_ref[...], mod_ref[1], mod_ref[0])
    hm = h if precise else h.astype(BF16)

    def gate_logits(bi):
        return _mm(hm, wg_ref[:, bi * d:(bi + 1) * d], precise) + bg_ref[:, bi * d:(bi + 1) * d]

    if sample:
        ya = ya_ref[...]
        yc = yc_ref[...]
        gates = [gate_logits(bi) for bi in range(3)]
    else:
        hm_ref[...] = hm

        def stage_gate(bi):
            def thunk():
                gate_ref[bi] = jnp.dot(hm_ref[...], wg_ref[:, bi * d:(bi + 1) * d],
                                       preferred_element_type=F32) + bg_ref[:, bi * d:(bi + 1) * d]
            return thunk

        _conv_tile(zc_ref, cw_ref, cb_ref, lg_ref, lb_ref, tail_ref, abuf_ref, shift_ref, ycs_ref,
                   between=[stage_gate(bi) for bi in range(3)])
        yc = ycs_ref[...]
        gates = [gate_ref[bi] for bi in range(3)]
        halves = []
        for hp in range(2):
            o0, l0, o1, l1, o2, l2 = [attn_refs[4 * g + s_ + hp][...] for g in range(3) for s_ in (0, 2)]
            top = jnp.maximum(jnp.maximum(l0, l1), l2)
            e0, e1, e2 = jnp.exp(l0 - top), jnp.exp(l1 - top), jnp.exp(l2 - top)
            halves.append((e0 * o0 + e1 * o1 + e2 * o2) / (e0 + e1 + e2))
        ya = jnp.concatenate(halves, axis=1)
    merged = None
    for bi, (y, w_ref) in enumerate(((ys_ref[...], ws_ref), (yc, wc_ref), (ya, wa_ref))):
        part = _sigmoid(gates[bi]) * _mm(y, w_ref[...], precise)
        merged = part if merged is None else merged + part
    o_ref[...] = x + mod_ref[2] * _mm(merged, wo_ref[...], precise)


def _merge_prompt(x, modp, l, gnorm, w_gate, b_gate, y_ssm, zc, conv_w, conv_b, ln_g, ln_b, attn,
                  w_bs, w_bc, w_ba, w_out):
    b_sz, seq, d = x.shape
    tm = TM_MERGE
    row = lambda w: pl.BlockSpec((None, tm, w), lambda b, i: (b, i, 0))
    lsel = lambda b, i: (l, 0, 0)
    wsp = lambda r, c: pl.BlockSpec((None, r, c), lsel)
    wsp1 = lambda r, c: pl.BlockSpec((None, r, c), lsel, pipeline_mode=pl.Buffered(1))
    in_specs = [
        row(d),
        pl.BlockSpec((None, None, 6, 1, d), lambda b, i: (l, b, 0, 0, 0)),
        wsp(1, d), wsp1(d, 3 * d), wsp(1, 3 * d),
        row(SSM_WIDTH), row(2 * CONV_WIDTH),
        pl.BlockSpec((None, CONV_K, SUBLANES, CONV_WIDTH), lambda b, i: (l, 0, 0, 0)),
        wsp(1, CONV_WIDTH), wsp(1, CONV_WIDTH), wsp(1, CONV_WIDTH),
    ] + [row(LANES)] * 12 + [wsp1(SSM_WIDTH, d), wsp1(CONV_WIDTH, d), wsp1(ATTN_OUT, d), wsp1(d, d)]
    flat = [t for group in attn for t in group]
    return pl.pallas_call(
        functools.partial(_merge_body, sample=False),
        grid=(b_sz, seq // tm),
        in_specs=in_specs,
        out_specs=[row(d), pl.BlockSpec((None, CONV_HALO, CONV_WIDTH), lambda b, i: (b, 0, 0))],
        out_shape=[jax.ShapeDtypeStruct((b_sz, seq, d), F32),
                   jax.ShapeDtypeStruct((b_sz, CONV_HALO, CONV_WIDTH), F32)],
        scratch_shapes=[pltpu.VMEM((tm + CONV_HALO, CONV_WIDTH), F32),
                        pltpu.VMEM((SUBLANES - 1, tm + CONV_HALO - SUBLANES, CONV_WIDTH), F32),
                        pltpu.VMEM((tm, CONV_WIDTH), BF16),
                        pltpu.VMEM((tm, d), BF16),
                        pltpu.VMEM((3, tm, d), F32)],
        compiler_params=_cparams(("arbitrary", "arbitrary"), 56),
        name="merge_prompt",
    )(x, modp, gnorm, w_gate, b_gate, y_ssm, zc, conv_w, conv_b, ln_g, ln_b, *flat, w_bs, w_bc, w_ba, w_out)


def _merge_sample(x, mods, l, gnorm, w_gate, b_gate, y_ssm, y_conv, y_attn, w_bs, w_bc, w_ba, w_out):
    n, d = x.shape
    full = lambda *shape: pl.BlockSpec(shape, lambda i: (0,) * len(shape))
    lsel = lambda i: (l, 0, 0)
    wsp = lambda r, c: pl.BlockSpec((None, r, c), lsel)
    in_specs = [
        full(n, d),
        pl.BlockSpec((None, 6, n, d), lambda i: (l, 0, 0, 0)),
        wsp(1, d), wsp(d, 3 * d), wsp(1, 3 * d),
        full(n, SSM_WIDTH), full(n, CONV_WIDTH), full(n, ATTN_OUT),
        wsp(SSM_WIDTH, d), wsp(CONV_WIDTH, d), wsp(ATTN_OUT, d), wsp(d, d),
    ]
    return pl.pallas_call(
        functools.partial(_merge_body, sample=True),
        grid=(1,),
        in_specs=in_specs,
        out_specs=full(n, d),
        out_shape=jax.ShapeDtypeStruct((n, d), F32),
        compiler_params=_cparams(("arbitrary",), 56),
        name="merge_sample",
    )(x, mods, gnorm, w_gate, b_gate, y_ssm, y_conv, y_attn, w_bs, w_bc, w_ba, w_out)


def _shift_rows(x, hist, k):
    r = pltpu.roll(x, k, 0)
    row = lax.broadcasted_iota(jnp.int32, hist.shape, 0)
    head = jnp.where(row < k, pltpu.roll(hist, k, 0), r[0:SUBLANES])
    return jnp.concatenate([head, r[SUBLANES:]], axis=0)


def _ffn_prompt_body(x_ref, mod_ref, g_ref, wup_ref, cw_ref, cb_ref, wdn_ref, o_ref, tail_ref,
                     hist_ref, hb_ref, up_ref, act_ref):
    i = pl.program_id(1)
    rows = x_ref.shape[0]
    f2 = 2 * FFN_HIDDEN

    @pl.when(i == 0)
    def _():
        hist_ref[...] = jnp.zeros((HIST, f2), F32)

    x = x_ref[...]
    hb_ref[...] = _norm_mod(x, g_ref[...], mod_ref[4], mod_ref[3]).astype(BF16)
    acc = jnp.zeros((rows, D_MODEL), F32)
    cw = FFN_CHUNK
    n_chunks = FFN_HIDDEN // cw

    def project(c):
        for half, base in enumerate((c * cw, FFN_HIDDEN + c * cw)):
            up_ref[c % 2, half] = jnp.dot(hb_ref[...], wup_ref[:, base:base + cw],
                                          preferred_element_type=F32)

    def contract(c, acc):
        return acc + jnp.dot(act_ref[c % 2], wdn_ref[c * cw:(c + 1) * cw, :], preferred_element_type=F32)

    project(0)
    for c in range(n_chunks):
        if c + 1 < n_chunks:
            project(c + 1)
        if c > 0:
            acc = contract(c - 1, acc)
        halves = []
        for half, base in enumerate((c * cw, FFN_HIDDEN + c * cw)):
            cols = slice(base, base + cw)
            up = up_ref[c % 2, half]
            hist = hist_ref[:, cols]
            hist_ref[:, cols] = up[rows - HIST:rows]
            halves.append(cw_ref[0:1, cols] * _shift_rows(up, hist, 2)
                          + cw_ref[1:2, cols] * _shift_rows(up, hist, 1)
                          + cw_ref[2:3, cols] * up + cb_ref[:, cols])
        act_ref[c % 2] = (_gelu_tanh(halves[0]) * halves[1]).astype(BF16)
    acc = contract(n_chunks - 1, acc)
    o_ref[...] = x_ref[...] + mod_ref[5] * acc
    tail_ref[...] = hist_ref[...]


def _ffn_prompt(x, modp, l, gnorm, w_up, conv_w, conv_b, w_down):
    b_sz, seq, d = x.shape
    tm = TM_FFN
    f2 = 2 * FFN_HIDDEN
    lsel = lambda b, i: (l, 0, 0)
    once = pl.Buffered(1)
    return pl.pallas_call(
        _ffn_prompt_body,
        grid=(b_sz, seq // tm),
        in_specs=[
            pl.BlockSpec((None, tm, d), lambda b, i: (b, i, 0)),
            pl.BlockSpec((None, None, 6, 1, d), lambda b, i: (l, b, 0, 0, 0)),
            pl.BlockSpec((None, 1, d), lsel),
            pl.BlockSpec((None, d, f2), lsel, pipeline_mode=once),
            pl.BlockSpec((None, FFN_CONV_K, f2), lsel),
            pl.BlockSpec((None, 1, f2), lsel),
            pl.BlockSpec((None, FFN_HIDDEN, d), lsel, pipeline_mode=once),
        ],
        out_specs=[
            pl.BlockSpec((None, tm, d), lambda b, i: (b, i, 0)),
            pl.BlockSpec((None, HIST, f2), lambda b, i: (b, 0, 0)),
        ],
        out_shape=[jax.ShapeDtypeStruct((b_sz, seq, d), F32),
                   jax.ShapeDtypeStruct((b_sz, HIST, f2), F32)],
        scratch_shapes=[pltpu.VMEM((HIST, f2), F32),
                        pltpu.VMEM((tm, d), BF16),
                        pltpu.VMEM((2, 2, tm, FFN_CHUNK), F32),
                        pltpu.VMEM((2, tm, FFN_CHUNK), BF16)],
        compiler_params=_cparams(("arbitrary", "arbitrary"), 56),
        name="ffn_prompt",
    )(x, modp, gnorm, w_up, conv_w, conv_b, w_down)


def _ffn_sample_body(x_ref, mod_ref, g_ref, wup_ref, c0_ref, c1_ref, cw_ref, cb_ref, wdn_ref, o_ref, up_ref):
    x = x_ref[...]
    h2 = _norm_mod(x, g_ref[...], mod_ref[4], mod_ref[3])
    up = _mm(h2, wup_ref[...], True)
    up_ref[...] = up
    cv = cw_ref[0:1, :] * c0_ref[...] + cw_ref[1:2, :] * c1_ref[...] + cw_ref[2:3, :] * up + cb_ref[...]
    act = _gelu_tanh(cv[:, 0:FFN_HIDDEN]) * cv[:, FFN_HIDDEN:2 * FFN_HIDDEN]
    o_ref[...] = x + mod_ref[5] * _mm(act, wdn_ref[...], True)


def _ffn_sample(x, mods, l, gnorm, w_up, c0, c1, conv_w, conv_b, w_down):
    n, d = x.shape
    f2 = 2 * FFN_HIDDEN
    full = lambda *shape: pl.BlockSpec(shape, lambda i: (0,) * len(shape))
    lsel = lambda i: (l, 0, 0)
    once = pl.Buffered(1)
    return pl.pallas_call(
        _ffn_sample_body,
        grid=(1,),
        in_specs=[
            full(n, d),
            pl.BlockSpec((None, 6, n, d), lambda i: (l, 0, 0, 0)),
            pl.BlockSpec((None, 1, d), lsel),
            pl.BlockSpec((None, d, f2), lsel, pipeline_mode=once),
            pl.BlockSpec((None, n, f2), lsel), pl.BlockSpec((None, n, f2), lsel),
            pl.BlockSpec((None, FFN_CONV_K, f2), lsel),
            pl.BlockSpec((None, 1, f2), lsel),
            pl.BlockSpec((None, FFN_HIDDEN, d), lsel, pipeline_mode=once),
        ],
        out_specs=[full(n, d), full(n, f2)],
        out_shape=[jax.ShapeDtypeStruct((n, d), F32), jax.ShapeDtypeStruct((n, f2), F32)],
        compiler_params=_cparams(("arbitrary",), 56),
        name="ffn_sample",
    )(x, mods, gnorm, w_up, c0, c1, conv_w, conv_b, w_down)


def _rope_tables(pos):
    half = HEAD_DIM // 2
    inv = ROPE_THETA ** (-jnp.arange(half, dtype=F32) / half)
    ang = pos.astype(F32)[:, None] * inv[None, :]
    cos, sin = jnp.cos(ang), jnp.sin(ang)
    cos_h = jnp.concatenate([cos, cos], axis=1)
    sin_h = jnp.concatenate([-sin, sin], axis=1)
    reps = LANES // HEAD_DIM
    return jnp.tile(cos_h, (1, reps)), jnp.tile(sin_h, (1, reps))


def _residue_perm(rows, dil):
    i = jnp.arange(rows)
    src = (i % (rows // dil)) * dil + i // (rows // dil)
    return (src[:, None] == jnp.arange(rows)[None, :]).astype(BF16)


def _block_ones(dtype):
    i = jnp.arange(MXU_DIM) // HEAD_DIM
    return (i[:, None] == i[None, :]).astype(dtype)


def kernel(x_prompt, x_sample, state_ssm, cache_conv, cache_kv_w128, cache_kv_w512, cache_kv_w2048, cache_ffn, c_prompt, c_sample, w_ada, b_ada, g_norm_mix, w_in, ssm_a_re, ssm_a_im, ssm_log_dt, ssm_b_re, ssm_b_im, ssm_c_re, ssm_c_im, ssm_d, ssm_w_glu, ssm_b_glu, conv_w, conv_b, conv_ln_g, conv_ln_b, attn_gq, attn_gk, w_gate, b_gate, w_br_ssm, w_br_conv, w_br_attn, w_out, g_norm_ffn, ffn_w_up, ffn_conv_w, ffn_conv_b, ffn_w_down):
    bp, seq, d = x_prompt.shape
    ns = x_sample.shape[0]
    depth = w_ada.shape[0]
    f2 = 2 * FFN_HIDDEN
    assert x_sample.shape[1] == 1
    assert all(seq % (ATTN_BLOCK * dil * ATTN_STEP_BLOCKS[dil]) == 0 for _, dil in ATTN_GROUPS)
    caches = (cache_kv_w128, cache_kv_w512, cache_kv_w2048)

    pad = (-(bp + ns)) % SUBLANES
    c_all = jnp.concatenate([c_prompt, c_sample, jnp.zeros((pad, d), F32)], axis=0)
    mod = _ada_call(c_all, w_ada, b_ada)
    modp = mod[:, :bp].reshape(depth, bp, 6, 1, d)
    mods = mod[:, bp:bp + ns].reshape(depth, ns, 6, d).transpose(0, 2, 1, 3)

    bf = lambda w: w.astype(BF16)
    w_in_b, w_gate_b, w_out_b = bf(w_in), bf(w_gate), bf(w_out)
    w_bs_b, w_bc_b, w_ba_b = bf(w_br_ssm), bf(w_br_conv), bf(w_br_attn)
    w_up_b, w_dn_b, w_glu_b = bf(ffn_w_up), bf(ffn_w_down), bf(ssm_w_glu)

    row3 = lambda t: t.reshape(depth, 1, t.shape[-1])
    g_mix, g_ffn = row3(g_norm_mix), row3(g_norm_ffn)
    gq = row3(jnp.tile(attn_gq, (1, ATTN_HEADS)))
    gk = row3(jnp.tile(attn_gk, (1, ATTN_HEADS)))
    b_gate3, ssm_d3, b_glu3 = row3(b_gate), row3(ssm_d), row3(ssm_b_glu)
    conv_b3, ln_g3, ln_b3, ffn_cb3 = row3(conv_b), row3(conv_ln_g), row3(conv_ln_b), row3(ffn_conv_b)
    conv_w4 = conv_w.reshape(depth, CONV_K, 1, CONV_WIDTH)
    conv_w8 = jnp.repeat(conv_w4, SUBLANES, axis=2)

    cos_p, sin_p = _rope_tables(jnp.arange(seq, dtype=jnp.int32))
    cos_s, sin_s = _rope_tables(jnp.full((ns,), PAST_LEN, dtype=jnp.int32))
    ones_b, ones_f = _block_ones(BF16), _block_ones(F32)
    p4 = _residue_perm(TM_INPROJ, ATTN_GROUPS[1][1])
    p16 = _residue_perm(TM_INPROJ, ATTN_GROUPS[2][1])
    seg = L_SSM // SUBLANES
    pm = _residue_perm(L_SSM, seg)
    pmt = pm.T
    zero_state = jnp.zeros((bp, 1, STATE_W), F32)

    tabs = _ssm_tables(ssm_a_re, ssm_a_im, ssm_log_dt, ssm_b_re, ssm_b_im, ssm_c_re, ssm_c_im, seg)
    state_slab = _state_to_slab(state_ssm.reshape((depth * ns,) + state_ssm.shape[2:])).reshape(depth, ns, STATE_W)
    conv_hist = cache_conv.transpose(0, 2, 1, 3)
    ffn_old, ffn_new = cache_ffn[:, :, 0], cache_ffn[:, :, 1]
    caches_t = [c.transpose(0, 1, 3, 4, 5, 2) for c in caches]

    yp = x_prompt
    ys = x_sample.reshape(ns, d)
    st_p, st_s, a_tails, a_news, up_tails, up_news, k_tails, v_tails = [], [], [], [], [], [], [], []
    rolled = None
    for l in range(depth):
        (zs, zc, k_tail, v_tail, q0, k0, v0, q1, k1, v1, q2, k2, v2) = _inproj_prompt(
            yp, modp, l, g_mix, w_in_b, gq, gk, cos_p, sin_p, ones_b, p4, p16)
        y_ssm, st = _ssm_prompt(zs, zero_state, l, pm, pmt, tabs, ssm_d3, w_glu_b, b_glu3)
        attn = [_attn_prompt(q0, k0, v0, 1), _attn_prompt(q1, k1, v1, ATTN_GROUPS[1][1]),
                _attn_prompt(q2, k2, v2, ATTN_GROUPS[2][1])]
        x_mid, a_tail = _merge_prompt(yp, modp, l, g_mix, w_gate_b, b_gate3, y_ssm, zc, conv_w8, conv_b3,
                                      ln_g3, ln_b3, attn, w_bs_b, w_bc_b, w_ba_b, w_out_b)
        yp, up_tail = _ffn_prompt(x_mid, modp, l, g_ffn, w_up_b, ffn_conv_w, ffn_cb3, w_dn_b)
        st_p.append(st)
        a_tails.append(a_tail)
        up_tails.append(up_tail)
        k_tails.append(k_tail)
        v_tails.append(v_tail)

        zs_s, zc_s, q_s, k_s, v_s = _inproj_sample(ys, mods, l, g_mix, w_in, gq, gk, cos_s, sin_s, ones_f)
        y_ssm_s, st_l = _ssm_sample(zs_s, state_slab, l, tabs, ssm_d3, ssm_w_glu, b_glu3)
        y_conv_s, a_s = _conv_sample(zc_s, conv_hist, l, conv_w4, conv_b3, ln_g3, ln_b3)
        cols = lambda t: t.reshape(ns, len(ATTN_GROUPS), HEADS_PER_GROUP, HEAD_DIM, 1)
        y_attn_s, *rolled = _attn_roll(cols(q_s), cols(k_s), cols(v_s), caches_t, rolled, l)
        y_attn_s = y_attn_s.reshape(ns, ATTN_OUT)
        x_mid_s = _merge_sample(ys, mods, l, g_mix, w_gate, b_gate3, y_ssm_s, y_conv_s, y_attn_s,
                                w_br_ssm, w_br_conv, w_br_attn, w_out)
        ys, up_s = _ffn_sample(x_mid_s, mods, l, g_ffn, ffn_w_up, ffn_old, ffn_new, ffn_conv_w, ffn_cb3,
                               ffn_w_down)
        st_s.append(st_l)
        a_news.append(a_s)
        up_news.append(up_s)

    stk = lambda t: jnp.stack(t, axis=0)
    unslab = lambda t, n: _slab_to_state(t.reshape(depth * n, STATE_W)).reshape(depth, n, SSM_GROUPS, SSM_N, 2)
    ssm_p = unslab(stk(st_p), bp)
    ssm_s = unslab(stk(st_s), ns)
    conv_p = stk(a_tails)[:, :, CONV_HALO - (CONV_K - 1):]
    conv_s = jnp.concatenate([cache_conv[:, :, 1:], stk(a_news)[:, :, None, :]], axis=2)
    ffn_p = stk(up_tails)[:, :, HIST - (FFN_CONV_K - 1):]
    ffn_s = jnp.stack([ffn_new, stk(up_news)], axis=2)
    kt, vt = stk(k_tails), stk(v_tails)
    gw = HEADS_PER_GROUP * HEAD_DIM
    kv_p = []
    for gi, (win, _) in enumerate(ATTN_GROUPS):
        keep = min(win, seq)
        pick = lambda t: t[:, :, gi * gw:(gi + 1) * gw, t.shape[-1] - keep:].reshape(
            depth, bp, HEADS_PER_GROUP, HEAD_DIM, keep)
        kv_p.append(jnp.stack([pick(kt), pick(vt)], axis=2).transpose(0, 1, 5, 2, 3, 4))
    kv_s = [t.transpose(0, 1, 5, 2, 3, 4) for t in rolled]
    return (yp, ys.reshape(ns, 1, d), ssm_p, ssm_s, conv_p, conv_s,
            kv_p[0], kv_s[0], kv_p[1], kv_s[1], kv_p[2], kv_s[2], ffn_p, ffn_s)
```

```python
import functools
import math

import jax
import jax.numpy as jnp
from jax import lax
from jax.experimental import pallas as pl
from jax.experimental.pallas import tpu as pltpu

F32 = jnp.float32
BF16 = jnp.bfloat16

D_MODEL = 1024
SSM_WIDTH = 512
SSM_GROUP = 16
SSM_GROUPS = 32
SSM_N = 64
CONV_WIDTH = 512
CONV_K = 31
HEAD_DIM = 64
HEADS_PER_GROUP = 4
ATTN_GROUPS = ((128, 1), (512, 4), (2048, 16))
ATTN_HEADS = 12
ATTN_WIDTH = 768
ATTN_OUT = 256
ATTN_BLOCK = 128
ROPE_THETA = 10000.0
FFN_HIDDEN = 2816
FFN_CONV_K = 3
IN_WIDTH = SSM_WIDTH + 2 * CONV_WIDTH + 3 * ATTN_WIDTH
EPS = 1e-6
PAST_LEN = 8192

SUBLANES = 8
LANES = 128
MXU_DIM = 256
SLAB_GROUPS = 8
N_SLABS = SSM_GROUPS // SLAB_GROUPS
SLAB_CH = SLAB_GROUPS * SSM_GROUP
SLAB_ST = SLAB_GROUPS * SSM_N
STATE_W = 2 * SSM_GROUPS * SSM_N
NEG_BIG = -1e30
GELU_C = math.sqrt(2.0 / math.pi)

TM_INPROJ = 512
L_SSM = 256
TM_MERGE = 256
TM_FFN = 256
FFN_CHUNK = 256
HIST = 8
CONV_HALO = 32
CONV_ROW_BLOCK = 32
KV_TAIL = 2048
ATTN_STEP_BLOCKS = {1: 4, 4: 4, 16: 4}


def _cparams(sem, vmem_mb):
    return pltpu.CompilerParams(dimension_semantics=sem, vmem_limit_bytes=vmem_mb * 1024 * 1024)


def _sigmoid(x):
    return 1.0 / (1.0 + jnp.exp(-x))


def _gelu_tanh(x):
    neg_2u = x * ((-2.0 * GELU_C * 0.044715) * (x * x) - 2.0 * GELU_C)
    return x / (1.0 + jnp.exp(neg_2u))


def _mm(a, w, precise):
    if precise:
        return jnp.dot(a, w, precision=lax.Precision.HIGHEST, preferred_element_type=F32)
    return jnp.dot(a.astype(BF16), w, preferred_element_type=F32)


def _norm_mod(x, g, scale, shift):
    ms = jnp.mean(x * x, axis=-1, keepdims=True)
    return (x * lax.rsqrt(ms + EPS) * g) * (1.0 + scale) + shift


def _ada_body(c_ref, w_ref, b_ref, o_ref):
    c = c_ref[...]
    s = c * _sigmoid(c)
    o_ref[...] = jnp.dot(s, w_ref[...], precision=lax.Precision.HIGHEST,
                         preferred_element_type=F32) + b_ref[...]


def _ada_call(c_all, w_ada, b_ada):
    depth, d, n6 = w_ada.shape
    rows = c_all.shape[0]
    tn = 1536
    return pl.pallas_call(
        _ada_body,
        grid=(depth, n6 // tn),
        in_specs=[
            pl.BlockSpec((rows, d), lambda l, j: (0, 0)),
            pl.BlockSpec((None, d, tn), lambda l, j: (l, 0, j)),
            pl.BlockSpec((None, 1, tn), lambda l, j: (l, 0, j)),
        ],
        out_specs=pl.BlockSpec((None, rows, tn), lambda l, j: (l, 0, j)),
        out_shape=jax.ShapeDtypeStruct((depth, rows, n6), F32),
        compiler_params=_cparams(("arbitrary", "arbitrary"), 40),
        name="ada_mod",
    )(c_all, w_ada, b_ada.reshape(depth, 1, n6))


def _head_norm_rope(z, gvec, ones, cos, sins, precise):
    sq = z * z
    parts = []
    for c in range(ATTN_WIDTH // MXU_DIM):
        parts.append(_mm(sq[:, c * MXU_DIM:(c + 1) * MXU_DIM], ones, precise))
    ms = jnp.concatenate(parts, axis=1) * (1.0 / HEAD_DIM)
    y = z * lax.rsqrt(ms + EPS) * gvec
    lane = lax.broadcasted_iota(jnp.int32, (1, LANES), 1)
    first = (lane % HEAD_DIM) < (HEAD_DIM // 2)
    outs = []
    for c in range(ATTN_WIDTH // LANES):
        yc = y[:, c * LANES:(c + 1) * LANES]
        partner = jnp.where(first, pltpu.roll(yc, LANES - HEAD_DIM // 2, 1),
                            pltpu.roll(yc, HEAD_DIM // 2, 1))
        outs.append(yc * cos + partner * sins)
    return jnp.concatenate(outs, axis=1)


def _inproj_body(*refs, sample, first_tail=0):
    if sample:
        (x_ref, mod_ref, g_ref, w_ref, gq_ref, gk_ref, cos_ref, sin_ref, ones_ref,
         zs_ref, zc_ref, q_ref, k_ref, v_ref) = refs
    else:
        (x_ref, mod_ref, g_ref, w_ref, gq_ref, gk_ref, cos_ref, sin_ref, ones_ref, p4_ref, p16_ref,
         zs_ref, zc_ref, kt_ref, vt_ref,
         q0_ref, k0_ref, v0_ref, q1_ref, k1_ref, v1_ref, q2_ref, k2_ref, v2_ref, kf_ref, vf_ref) = refs
    precise = sample
    x = x_ref[...]
    h = _norm_mod(x, g_ref[...], mod_ref[1], mod_ref[0])
    hm = h if precise else h.astype(BF16)
    c1 = SSM_WIDTH
    c2 = c1 + 2 * CONV_WIDTH
    c3 = c2 + ATTN_WIDTH
    c4 = c3 + ATTN_WIDTH
    zs_ref[...] = _mm(hm, w_ref[:, 0:c1], precise)
    zc_ref[...] = _mm(hm, w_ref[:, c1:c2], precise)
    zq = _mm(hm, w_ref[:, c2:c3], precise)
    zk = _mm(hm, w_ref[:, c3:c4], precise)
    zv = _mm(hm, w_ref[:, c4:IN_WIDTH], precise)
    cos = cos_ref[...]
    sins = sin_ref[...]
    ones = ones_ref[...]
    q = _head_norm_rope(zq, gq_ref[...], ones, cos, sins, precise)
    k = _head_norm_rope(zk, gk_ref[...], ones, cos, sins, precise)
    if sample:
        q_ref[...] = q
        k_ref[...] = k
        v_ref[...] = zv
        return
    kf_ref[...] = k
    vf_ref[...] = zv
    qb = (q * (HEAD_DIM ** -0.5)).astype(BF16)
    kb = k.astype(BF16)
    vb = zv.astype(BF16)
    gw = HEADS_PER_GROUP * HEAD_DIM
    q0_ref[0] = qb[:, 0:gw]
    k0_ref[0] = kb[:, 0:gw]
    v0_ref[0] = vb[:, 0:gw]
    for gi, (p_ref, outs) in enumerate(((p4_ref, (q1_ref, k1_ref, v1_ref)),
                                        (p16_ref, (q2_ref, k2_ref, v2_ref))), start=1):
        pm = p_ref[...]
        for src, o_ref in zip((qb, kb, vb), outs):
            dil, rows = o_ref.shape[0], o_ref.shape[1]
            perm = jnp.dot(pm, src[:, gi * gw:(gi + 1) * gw], preferred_element_type=F32).astype(BF16)
            for r in range(dil):
                o_ref[r] = perm[r * rows:(r + 1) * rows]

    @pl.when(pl.program_id(1) >= first_tail)
    def _():
        kt_ref[...] = kf_ref[...].T
        vt_ref[...] = vf_ref[...].T


def _inproj_prompt(x, modp, l, gnorm, w_in, gq, gk, cos, sins, ones, p4, p16):
    b_sz, seq, d = x.shape
    tm = TM_INPROJ
    nt = seq // tm
    tail = min(KV_TAIL, seq)
    ft = nt - tail // tm
    gw = HEADS_PER_GROUP * HEAD_DIM
    tmap = lambda b, i: (b, 0, jnp.maximum(i - ft, 0))
    in_specs = [
        pl.BlockSpec((None, tm, d), lambda b, i: (b, i, 0)),
        pl.BlockSpec((None, None, 6, 1, d), lambda b, i: (l, b, 0, 0, 0)),
        pl.BlockSpec((None, 1, d), lambda b, i: (l, 0, 0)),
        pl.BlockSpec((None, d, IN_WIDTH), lambda b, i: (l, 0, 0), pipeline_mode=pl.Buffered(1)),
        pl.BlockSpec((None, 1, ATTN_WIDTH), lambda b, i: (l, 0, 0)),
        pl.BlockSpec((None, 1, ATTN_WIDTH), lambda b, i: (l, 0, 0)),
        pl.BlockSpec((tm, LANES), lambda b, i: (i, 0)),
        pl.BlockSpec((tm, LANES), lambda b, i: (i, 0)),
        pl.BlockSpec((MXU_DIM, MXU_DIM), lambda b, i: (0, 0)),
        pl.BlockSpec((tm, tm), lambda b, i: (0, 0)),
        pl.BlockSpec((tm, tm), lambda b, i: (0, 0)),
    ]
    out_shape = [
        jax.ShapeDtypeStruct((b_sz, seq, SSM_WIDTH), F32),
        jax.ShapeDtypeStruct((b_sz, seq, 2 * CONV_WIDTH), F32),
        jax.ShapeDtypeStruct((b_sz, ATTN_WIDTH, tail), F32),
        jax.ShapeDtypeStruct((b_sz, ATTN_WIDTH, tail), F32),
    ]
    out_specs = [
        pl.BlockSpec((None, tm, SSM_WIDTH), lambda b, i: (b, i, 0)),
        pl.BlockSpec((None, tm, 2 * CONV_WIDTH), lambda b, i: (b, i, 0)),
        pl.BlockSpec((None, ATTN_WIDTH, tm), tmap),
        pl.BlockSpec((None, ATTN_WIDTH, tm), tmap),
    ]
    for _, dil in ATTN_GROUPS:
        for _ in range(3):
            out_shape.append(jax.ShapeDtypeStruct((b_sz, dil, seq // dil, gw), BF16))
            out_specs.append(pl.BlockSpec((None, dil, tm // dil, gw), lambda b, i: (b, 0, i, 0)))
    return pl.pallas_call(
        functools.partial(_inproj_body, sample=False, first_tail=ft),
        grid=(b_sz, nt),
        in_specs=in_specs,
        out_specs=out_specs,
        out_shape=out_shape,
        scratch_shapes=[pltpu.VMEM((tm, ATTN_WIDTH), F32), pltpu.VMEM((tm, ATTN_WIDTH), F32)],
        compiler_params=_cparams(("arbitrary", "arbitrary"), 56),
        name="inproj_prompt",
    )(x, modp, gnorm, w_in, gq, gk, cos, sins, ones, p4, p16)


def _inproj_sample(x, mods, l, gnorm, w_in, gq, gk, cos, sins, ones):
    n, d = x.shape
    lsel = lambda i: (l, 0, 0)
    in_specs = [
        pl.BlockSpec((n, d), lambda i: (0, 0)),
        pl.BlockSpec((None, 6, n, d), lambda i: (l, 0, 0, 0)),
        pl.BlockSpec((None, 1, d), lsel),
        pl.BlockSpec((None, d, IN_WIDTH), lsel),
        pl.BlockSpec((None, 1, ATTN_WIDTH), lsel),
        pl.BlockSpec((None, 1, ATTN_WIDTH), lsel),
        pl.BlockSpec((n, LANES), lambda i: (0, 0)),
        pl.BlockSpec((n, LANES), lambda i: (0, 0)),
        pl.BlockSpec((MXU_DIM, MXU_DIM), lambda i: (0, 0)),
    ]
    widths = (SSM_WIDTH, 2 * CONV_WIDTH, ATTN_WIDTH, ATTN_WIDTH, ATTN_WIDTH)
    return pl.pallas_call(
        functools.partial(_inproj_body, sample=True),
        grid=(1,),
        in_specs=in_specs,
        out_specs=[pl.BlockSpec((n, w), lambda i: (0, 0)) for w in widths],
        out_shape=[jax.ShapeDtypeStruct((n, w), F32) for w in widths],
        compiler_params=_cparams(("arbitrary",), 48),
        name="inproj_sample",
    )(x, mods, gnorm, w_in, gq, gk, cos, sins, ones)


def _ssm_tail(y_raw, u, d_ref, wglu_ref, bglu_ref, precise):
    y = _gelu_tanh(y_raw + d_ref[...] * u)
    return y * _sigmoid(_mm(y, wglu_ref[...], precise) + bglu_ref[...])


def _ssm_prompt_body(zs_ref, s0_ref, pm_ref, pmt_ref, b_ref, c_ref, lam_ref, lamp_ref, pw_ref,
                     d_ref, wglu_ref, bglu_ref, y_ref, st_ref,
                     v_ref, sb_ref, cs_ref, carry_ref):
    ci = pl.program_id(1)
    chunk = zs_ref.shape[0]
    seg = chunk // SUBLANES

    @pl.when(ci == 0)
    def _():
        carry_ref[...] = s0_ref[...]

    u = zs_ref[...]
    up = jnp.dot(pm_ref[...], u.astype(BF16), preferred_element_type=F32).astype(BF16)
    re, im = slice(0, SLAB_ST), slice(SLAB_ST, 2 * SLAB_ST)
    for m in range(N_SLABS):
        v_ref[m] = jnp.dot(up[:, m * SLAB_CH:(m + 1) * SLAB_CH], b_ref[m], preferred_element_type=F32)
    y_parts = [None] * N_SLABS
    for pair in range(0, N_SLABS, 2):
        slabs = (pair, pair + 1)
        zero = jnp.zeros((SUBLANES, SLAB_ST), F32)
        state = {m: (zero, zero) for m in slabs}
        for i in range(seg):
            rows = slice(i * SUBLANES, (i + 1) * SUBLANES)
            for m in slabs:
                lr = lam_ref[m, 0]
                li = lam_ref[m, 1]
                sr, si = state[m]
                nsr = lr * sr - li * si + v_ref[m, rows, re]
                nsi = lr * si + li * sr + v_ref[m, rows, im]
                v_ref[m, rows, re] = nsr
                v_ref[m, rows, im] = nsi
                state[m] = (nsr, nsi)
        for m in slabs:
            base = m * 2 * SLAB_ST
            sr, si = state[m]
            er = carry_ref[0:1, base:base + SLAB_ST]
            ei = carry_ref[0:1, base + SLAB_ST:base + 2 * SLAB_ST]
            pr = lamp_ref[m, 0:1, :]
            pi = lamp_ref[m, 1:2, :]
            for r in range(SUBLANES):
                cs_ref[m, r:r + 1, re] = er
                cs_ref[m, r:r + 1, im] = ei
                ner = sr[r:r + 1] + pr * er - pi * ei
                nei = si[r:r + 1] + pr * ei + pi * er
                er, ei = ner, nei
            carry_ref[0:1, base:base + SLAB_ST] = er
            carry_ref[0:1, base + SLAB_ST:base + 2 * SLAB_ST] = ei
        for m in slabs:
            csr = cs_ref[m, :, re]
            csi = cs_ref[m, :, im]
            for i2 in range(seg // 2):
                rows16 = slice(i2 * 2 * SUBLANES, (i2 + 1) * 2 * SUBLANES)
                rows_r, rows_i = [], []
                for h in range(2):
                    i = i2 * 2 + h
                    rows = slice(i * SUBLANES, (i + 1) * SUBLANES)
                    qr = pw_ref[m, 0, rows, :]
                    qi = pw_ref[m, 1, rows, :]
                    rows_r.append(v_ref[m, rows, re] + qr * csr - qi * csi)
                    rows_i.append(v_ref[m, rows, im] + qr * csi + qi * csr)
                sb_ref[m, rows16, re] = jnp.concatenate(rows_r, axis=0).astype(BF16)
                sb_ref[m, rows16, im] = jnp.concatenate(rows_i, axis=0).astype(BF16)
            y_parts[m] = jnp.dot(sb_ref[m], c_ref[m], preferred_element_type=F32)
    y_perm = jnp.concatenate(y_parts, axis=1)
    hi = y_perm.astype(BF16)
    lo = (y_perm - hi.astype(F32)).astype(BF16)
    pmt = pmt_ref[...]
    y_nat = jnp.dot(pmt, hi, preferred_element_type=F32) + jnp.dot(pmt, lo, preferred_element_type=F32)
    y_ref[...] = _ssm_tail(y_nat, u, d_ref, wglu_ref, bglu_ref, False).astype(BF16)
    st_ref[...] = carry_ref[...]


def _ssm_prompt(zs, s0, l, pm, pmt, tabs, ssm_d, w_glu, b_glu):
    b_sz, seq, _ = zs.shape
    chunk = L_SSM
    full = lambda *shape: pl.BlockSpec(shape, lambda b, c: (0,) * len(shape))
    tab = lambda *shape: pl.BlockSpec((None,) + shape, lambda b, c: (l,) + (0,) * len(shape))
    lsel = lambda b, c: (l, 0, 0)
    return pl.pallas_call(
        _ssm_prompt_body,
        grid=(b_sz, seq // chunk),
        in_specs=[
            pl.BlockSpec((None, chunk, SSM_WIDTH), lambda b, c: (b, c, 0)),
            pl.BlockSpec((None, 1, STATE_W), lambda b, c: (b, 0, 0)),
            full(chunk, chunk), full(chunk, chunk),
            tab(N_SLABS, SLAB_CH, 2 * SLAB_ST),
            tab(N_SLABS, 2 * SLAB_ST, SLAB_CH),
            tab(N_SLABS, 2, SUBLANES, SLAB_ST), tab(N_SLABS, 2, SLAB_ST),
            tab(N_SLABS, 2, chunk, SLAB_ST),
            pl.BlockSpec((None, 1, SSM_WIDTH), lsel),
            pl.BlockSpec((None, SSM_WIDTH, SSM_WIDTH), lsel),
            pl.BlockSpec((None, 1, SSM_WIDTH), lsel),
        ],
        out_specs=[
            pl.BlockSpec((None, chunk, SSM_WIDTH), lambda b, c: (b, c, 0)),
            pl.BlockSpec((None, 1, STATE_W), lambda b, c: (b, 0, 0)),
        ],
        out_shape=[jax.ShapeDtypeStruct((b_sz, seq, SSM_WIDTH), BF16),
                   jax.ShapeDtypeStruct((b_sz, 1, STATE_W), F32)],
        scratch_shapes=[
            pltpu.VMEM((N_SLABS, chunk, 2 * SLAB_ST), F32),
            pltpu.VMEM((N_SLABS, chunk, 2 * SLAB_ST), BF16),
            pltpu.VMEM((N_SLABS, SUBLANES, 2 * SLAB_ST), F32),
            pltpu.VMEM((1, STATE_W), F32),
        ],
        compiler_params=_cparams(("arbitrary", "arbitrary"), 40),
        name="ssm_prompt",
    )(zs, s0, pm, pmt, tabs["b_bf"], tabs["c_bf"], tabs["lam_rows"], tabs["lam_seg"], tabs["pw_rows"],
      ssm_d, w_glu, b_glu)


def _ssm_sample_body(u_ref, s0_ref, b_ref, c_ref, lam_ref, d_ref, wglu_ref, bglu_ref, y_ref, st_ref):
    u = u_ref[...]
    y_parts = []
    for m in range(N_SLABS):
        base = m * 2 * SLAB_ST
        v = _mm(u[:, m * SLAB_CH:(m + 1) * SLAB_CH], b_ref[m], True)
        lr = lam_ref[m, 0:1, :]
        li = lam_ref[m, 1:2, :]
        sr0 = s0_ref[:, base:base + SLAB_ST]
        si0 = s0_ref[:, base + SLAB_ST:base + 2 * SLAB_ST]
        sr = lr * sr0 - li * si0 + v[:, 0:SLAB_ST]
        si = lr * si0 + li * sr0 + v[:, SLAB_ST:2 * SLAB_ST]
        st_ref[:, base:base + SLAB_ST] = sr
        st_ref[:, base + SLAB_ST:base + 2 * SLAB_ST] = si
        y_parts.append(_mm(jnp.concatenate([sr, si], axis=1), c_ref[m], True))
    y_raw = jnp.concatenate(y_parts, axis=1)
    y_ref[...] = _ssm_tail(y_raw, u, d_ref, wglu_ref, bglu_ref, True)


def _ssm_sample(zs, s0, l, tabs, ssm_d, w_glu, b_glu):
    n = zs.shape[0]
    full = lambda *shape: pl.BlockSpec(shape, lambda i: (0,) * len(shape))
    tab = lambda *shape: pl.BlockSpec((None,) + shape, lambda i: (l,) + (0,) * len(shape))
    lsel = lambda i: (l, 0, 0)
    return pl.pallas_call(
        _ssm_sample_body,
        grid=(1,),
        in_specs=[
            full(n, SSM_WIDTH), tab(n, STATE_W),
            tab(N_SLABS, SLAB_CH, 2 * SLAB_ST), tab(N_SLABS, 2 * SLAB_ST, SLAB_CH),
            tab(N_SLABS, 2, SLAB_ST),
            pl.BlockSpec((None, 1, SSM_WIDTH), lsel),
            pl.BlockSpec((None, SSM_WIDTH, SSM_WIDTH), lsel),
            pl.BlockSpec((None, 1, SSM_WIDTH), lsel),
        ],
        out_specs=[full(n, SSM_WIDTH), full(n, STATE_W)],
        out_shape=[jax.ShapeDtypeStruct((n, SSM_WIDTH), F32), jax.ShapeDtypeStruct((n, STATE_W), F32)],
        compiler_params=_cparams(("arbitrary",), 32),
        name="ssm_sample",
    )(zs, s0, tabs["b_f32"], tabs["c_f32"], tabs["lam"], ssm_d, w_glu, b_glu)


def _ssm_tables(a_re, a_im, log_dt, b_re, b_im, c_re, c_im, seg):
    depth = a_re.shape[0]
    dt = jnp.exp(log_dt)[..., None]
    xr, xi = a_re * dt, a_im * dt
    mag = jnp.exp(xr)
    lr, li = mag * jnp.cos(xi), mag * jnp.sin(xi)
    den = a_re * a_re + a_im * a_im
    nr, ni = lr - 1.0, li
    cr = (nr * a_re + ni * a_im) / den
    cim = (ni * a_re - nr * a_im) / den
    bbr = cr[..., None] * b_re - cim[..., None] * b_im
    bbi = cr[..., None] * b_im + cim[..., None] * b_re
    eye = jnp.eye(SLAB_GROUPS, dtype=F32)

    def b_slab(t):
        t = t.reshape(depth, N_SLABS, SLAB_GROUPS, SSM_N, SSM_GROUP)
        return jnp.einsum('lmgnc,gh->lmgchn', t, eye).reshape(depth, N_SLABS, SLAB_CH, SLAB_ST)

    def c_slab(t):
        t = t.reshape(depth, N_SLABS, SLAB_GROUPS, SSM_GROUP, SSM_N)
        return jnp.einsum('lmgcn,gh->lmgnhc', t, eye).reshape(depth, N_SLABS, SLAB_ST, SLAB_CH)

    b_mat = jnp.concatenate([b_slab(bbr), b_slab(bbi)], axis=3)
    c_mat = jnp.concatenate([c_slab(c_re), c_slab(-c_im)], axis=2)

    def powers(ks):
        k = jnp.asarray(ks, F32)[:, None, None, None]
        mk = jnp.exp(k * xr[None])
        both = jnp.stack([mk * jnp.cos(k * xi[None]), mk * jnp.sin(k * xi[None])], axis=0)
        return both.reshape(2, len(ks), depth, N_SLABS, SLAB_ST).transpose(2, 3, 0, 1, 4)

    pw = powers(range(1, seg + 1))
    lam = powers([1])
    rep = lambda t: jnp.repeat(t, SUBLANES, axis=3)
    return dict(b_f32=b_mat, c_f32=c_mat, b_bf=b_mat.astype(BF16), c_bf=c_mat.astype(BF16),
                lam=lam[:, :, :, 0], lam_rows=rep(lam), lam_seg=powers([seg])[:, :, :, 0], pw_rows=rep(pw))


def _state_to_slab(s):
    b = s.shape[0]
    return s.reshape(b, N_SLABS, SLAB_GROUPS, SSM_N, 2).transpose(0, 1, 4, 2, 3).reshape(b, STATE_W)


def _slab_to_state(x):
    b = x.shape[0]
    return x.reshape(b, N_SLABS, 2, SLAB_GROUPS, SSM_N).transpose(0, 1, 3, 4, 2).reshape(b, SSM_GROUPS, SSM_N, 2)


def _ln_silu(y, g, b):
    mu = jnp.mean(y, axis=-1, keepdims=True)
    var = jnp.mean(jnp.square(y - mu), axis=-1, keepdims=True)
    t = (y - mu) * lax.rsqrt(var + EPS) * g + b
    return t * _sigmoid(t)


def _conv_tile(zc_ref, w_ref, b_ref, lg_ref, lb_ref, tail_ref, abuf_ref, shift_ref, y_ref, between=()):
    rows = zc_ref.shape[0]

    @pl.when(pl.program_id(1) == 0)
    def _():
        abuf_ref[0:CONV_HALO, :] = jnp.zeros((CONV_HALO, CONV_WIDTH), F32)

    z = zc_ref[...]
    a = z[:, 0:CONV_WIDTH] * _sigmoid(z[:, CONV_WIDTH:2 * CONV_WIDTH])
    abuf_ref[CONV_HALO:CONV_HALO + rows, :] = a
    span = shift_ref.shape[1]
    for c in range(1, SUBLANES):
        shift_ref[c - 1] = abuf_ref[c:c + span, :]
    first = CONV_HALO - (CONV_K - 1)
    blk = CONV_ROW_BLOCK
    n_blk = rows // blk
    pending = list(between)
    every = -(-n_blk // (len(pending) + 1)) if pending else n_blk
    for r in range(n_blk):
        if pending and r % every == 0:
            pending.pop(0)()
        acc = jnp.zeros((blk, CONV_WIDTH), F32) + b_ref[...]
        for j in range(CONV_K):
            c = (first + j) % SUBLANES
            lo = first + j - c + r * blk
            src = abuf_ref[lo:lo + blk, :] if c == 0 else shift_ref[c - 1, lo:lo + blk, :]
            acc = acc + jnp.concatenate([w_ref[j]] * (blk // SUBLANES), axis=0) * src
        y_ref[r * blk:(r + 1) * blk, :] = _ln_silu(acc, lg_ref[...], lb_ref[...]).astype(y_ref.dtype)
    for thunk in pending:
        thunk()
    last = a[rows - CONV_HALO:rows]
    tail_ref[...] = last
    abuf_ref[0:CONV_HALO, :] = last


def _conv_sample_body(zc_ref, hist_ref, w_ref, b_ref, lg_ref, lb_ref, y_ref, a_ref):
    z = zc_ref[...]
    a = z[:, 0:CONV_WIDTH] * _sigmoid(z[:, CONV_WIDTH:2 * CONV_WIDTH])
    a_ref[...] = a
    acc = b_ref[...] + w_ref[CONV_K - 1] * a
    for j in range(CONV_K - 1):
        acc = acc + w_ref[j] * hist_ref[j]
    y_ref[...] = _ln_silu(acc, lg_ref[...], lb_ref[...])


def _conv_sample(zc, hist_t, l, conv_w4, conv_b, ln_g, ln_b):
    n = zc.shape[0]
    full = lambda *shape: pl.BlockSpec(shape, lambda i: (0,) * len(shape))
    lsel = lambda i: (l, 0, 0)
    return pl.pallas_call(
        _conv_sample_body,
        grid=(1,),
        in_specs=[
            full(n, 2 * CONV_WIDTH),
            pl.BlockSpec((None, CONV_K - 1, n, CONV_WIDTH), lambda i: (l, 0, 0, 0)),
            pl.BlockSpec((None, CONV_K, 1, CONV_WIDTH), lambda i: (l, 0, 0, 0)),
            pl.BlockSpec((None, 1, CONV_WIDTH), lsel),
            pl.BlockSpec((None, 1, CONV_WIDTH), lsel),
            pl.BlockSpec((None, 1, CONV_WIDTH), lsel),
        ],
        out_specs=[full(n, CONV_WIDTH), full(n, CONV_WIDTH)],
        out_shape=[jax.ShapeDtypeStruct((n, CONV_WIDTH), F32)] * 2,
        compiler_params=_cparams(("arbitrary",), 32),
        name="conv_sample",
    )(zc, hist_t, conv_w4, conv_b, ln_g, ln_b)


def _attn_prompt_body(q_ref, kc_ref, kp_ref, vc_ref, vp_ref, oa_ref, ob_ref, la_ref, lb_ref, *, dil):
    j = pl.program_id(1)
    r = pl.program_id(2)
    blk = ATTN_BLOCK
    nblk = q_ref.shape[0] // blk
    q_all = q_ref[...]
    k_all = jnp.concatenate([kp_ref[...], kc_ref[...]], axis=0)
    v_all = jnp.concatenate([vp_ref[...], vc_ref[...]], axis=0)
    qi = lax.broadcasted_iota(jnp.int32, (blk, 2 * blk), 0)
    ki = lax.broadcasted_iota(jnp.int32, (blk, 2 * blk), 1)
    dist = qi + blk - ki
    band = jnp.where(dist >= 0, jnp.where(dist <= blk, 1, 0), 0)
    first_ok = jnp.where(ki >= blk, 1, jnp.where(j > 0, 1, 0))
    lane = lax.broadcasted_iota(jnp.int32, (1, LANES), 1)
    lo = lane < HEAD_DIM
    for s_blk in range(nblk):
        q = q_all[s_blk * blk:(s_blk + 1) * blk]
        k = k_all[s_blk * blk:(s_blk + 2) * blk]
        v = v_all[s_blk * blk:(s_blk + 2) * blk]
        valid = ((band * first_ok) if s_blk == 0 else band) > 0
        o_parts, lse_parts = [], []
        for hp in range(HEADS_PER_GROUP // 2):
            sl = slice(hp * LANES, (hp + 1) * LANES)
            q2, k2, v2 = q[:, sl], k[:, sl], v[:, sl]
            res = []
            for half in range(2):
                keep = lo if half == 0 else jnp.logical_not(lo)
                qm = jnp.where(keep, q2, jnp.zeros_like(q2))
                s = lax.dot_general(qm, k2, (((1,), (1,)), ((), ())), preferred_element_type=F32)
                s = jnp.where(valid, s, NEG_BIG)
                m = jnp.max(s, axis=-1, keepdims=True)
                p = jnp.exp(s - m)
                den = jnp.sum(p, axis=-1, keepdims=True)
                o = jnp.dot(p.astype(BF16), v2, preferred_element_type=F32)
                res.append((o / den, m + jnp.log(den)))
            o_parts.append(jnp.where(lo, res[0][0], res[1][0]))
            lse_parts.append(jnp.where(lo, res[0][1], res[1][1]))
        for val, ref in zip(o_parts + lse_parts, (oa_ref, ob_ref, la_ref, lb_ref)):
            if dil == 1:
                ref[s_blk * blk:(s_blk + 1) * blk, :] = val
            else:
                ref[pl.ds(r + s_blk * blk * dil, blk, stride=dil), :] = val


def _attn_prompt(q, k, v, dil):
    b_sz, _, m_len, gw = q.shape
    blk = ATTN_BLOCK
    seq = m_len * dil
    nblk = ATTN_STEP_BLOCKS[dil]
    cur = lambda b, j, r: (b, r, j, 0)
    prev = lambda b, j, r: (b, r, jnp.maximum(j * nblk - 1, 0), 0)
    bs = lambda f: pl.BlockSpec((None, None, nblk * blk, gw), f)
    bp = lambda f: pl.BlockSpec((None, None, blk, gw), f)
    span = nblk * blk * dil
    return pl.pallas_call(
        functools.partial(_attn_prompt_body, dil=dil),
        grid=(b_sz, m_len // (nblk * blk), dil),
        in_specs=[bs(cur), bs(cur), bp(prev), bs(cur), bp(prev)],
        out_specs=[pl.BlockSpec((None, span, LANES), lambda b, j, r: (b, j, 0))] * 4,
        out_shape=[jax.ShapeDtypeStruct((b_sz, seq, LANES), F32)] * 4,
        compiler_params=_cparams(("arbitrary", "arbitrary", "arbitrary"), 48),
        name=f"attn_prompt_d{dil}",
    )(q, k, k, v, v)


def _attn_roll_body(*refs):
    ng = len(ATTN_GROUPS)
    q_ref, kn_ref, vn_ref = refs[:3]
    c_refs = refs[3:3 + ng]
    y_ref = refs[-1 - ng]
    o_refs = refs[-ng:]
    scale = HEAD_DIM ** -0.5
    outs = [[None] * ng for _ in range(HEADS_PER_GROUP)]
    lses = [[None] * ng for _ in range(HEADS_PER_GROUP)]

    def columns(x_ref):
        x = jnp.concatenate([x_ref[...], jnp.zeros((LANES - SUBLANES, LANES), F32)], axis=0)
        return x.T

    def head_col(t, head):
        r, half = head // 2, head % 2
        return t[half * HEAD_DIM:(half + 1) * HEAD_DIM, r:r + 1]

    q_t, kn_t, vn_t = columns(q_ref), columns(kn_ref), columns(vn_ref)
    for g, ((_, dil), c_ref, o_ref) in enumerate(zip(ATTN_GROUPS, c_refs, o_refs)):
        buf_len = c_ref.shape[-1]
        lane = lax.broadcasted_iota(jnp.int32, (1, buf_len), 1)
        used = (lane % dil) == 0
        last = lane == buf_len - 1
        for h in range(HEADS_PER_GROUP):
            kk = c_ref[0, h]
            vv = c_ref[1, h]
            head = g * HEADS_PER_GROUP + h
            qc, knc, vnc = head_col(q_t, head), head_col(kn_t, head), head_col(vn_t, head)
            s = jnp.where(used, jnp.sum(kk * qc, axis=0, keepdims=True) * scale, NEG_BIG)
            sn = jnp.sum(knc * qc, axis=0, keepdims=True) * scale
            m = jnp.maximum(jnp.max(s, axis=1, keepdims=True), sn)
            p = jnp.exp(s - m)
            pn = jnp.exp(sn - m)
            den = jnp.sum(p, axis=1, keepdims=True) + pn
            outs[h][g] = (jnp.sum(vv * p, axis=1, keepdims=True) + pn * vnc) / den
            lses[h][g] = m + jnp.log(den)
            o_ref[0, h] = jnp.where(last, knc, pltpu.roll(kk, buf_len - 1, 1))
            o_ref[1, h] = jnp.where(last, vnc, pltpu.roll(vv, buf_len - 1, 1))
    ys = []
    for h in range(HEADS_PER_GROUP):
        top = functools.reduce(jnp.maximum, lses[h])
        ws = [jnp.exp(t - top) for t in lses[h]]
        ys.append(sum(w * o for w, o in zip(ws, outs[h])) / sum(ws))
    lane = lax.broadcasted_iota(jnp.int32, (1, LANES), 1)
    square = jnp.zeros((LANES, LANES), F32)
    for r in range(HEADS_PER_GROUP // 2):
        square = jnp.where(lane == r, jnp.concatenate([ys[2 * r], ys[2 * r + 1]], axis=0), square)
    y_ref[...] = square.T[0:SUBLANES]


def _attn_roll(q, kn, vn, caches_t, prev, l):
    n = q.shape[0]
    ng = len(ATTN_GROUPS)
    small = pl.BlockSpec((None, SUBLANES, LANES), lambda b: (b, 0, 0))
    c_specs = []
    for (win, dil), c in zip(ATTN_GROUPS, caches_t):
        assert c.shape[-1] == win and win % dil == 0
        c_specs.append(pl.BlockSpec((None, None) + c.shape[2:], lambda b: (l, b, 0, 0, 0, 0)))
    in_specs = [small, small, small] + c_specs
    args = [q, kn, vn, *caches_t]
    aliases = {}
    if prev is not None:
        in_specs += [pl.BlockSpec(memory_space=pl.ANY)] * ng
        aliases = {len(args) + g: 1 + g for g in range(ng)}
        args += list(prev)
    return pl.pallas_call(
        _attn_roll_body,
        grid=(n,),
        in_specs=in_specs,
        out_specs=[small] + c_specs,
        out_shape=[jax.ShapeDtypeStruct((n, SUBLANES, LANES), F32)]
        + [jax.ShapeDtypeStruct(c.shape, c.dtype) for c in caches_t],
        input_output_aliases=aliases,
        compiler_params=_cparams(("arbitrary",), 48),
        name="attn_roll_sample",
    )(*args)


def _merge_body(*refs, sample):
    if sample:
        (x_ref, mod_ref, g_ref, wg_ref, bg_ref, ys_ref, yc_ref, ya_ref,
         ws_ref, wc_ref, wa_ref, wo_ref, o_ref) = refs
    else:
        (x_ref, mod_ref, g_ref, wg_ref, bg_ref, ys_ref, zc_ref, cw_ref, cb_ref, lg_ref, lb_ref,
         *attn_refs, ws_ref, wc_ref, wa_ref, wo_ref, o_ref, tail_ref,
         abuf_ref, shift_ref, ycs_ref, hm_ref, gate_ref) = refs
    precise = sample
    d = D_MODEL
    x = x_ref[...]
    h = _norm_mod(x, g_ref[...], mod_ref[1], mod_ref[0])
    hm = h if precise else h.astype(BF16)

    def gate_logits(bi):
        return _mm(hm, wg_ref[:, bi * d:(bi + 1) * d], precise) + bg_ref[:, bi * d:(bi + 1) * d]

    if sample:
        ya = ya_ref[...]
        yc = yc_ref[...]
        gates = [gate_logits(bi) for bi in range(3)]
    else:
        hm_ref[...] = hm

        def stage_gate(bi):
            def thunk():
                gate_ref[bi] = jnp.dot(hm_ref[...], wg_ref[:, bi * d:(bi + 1) * d],
                                       preferred_element_type=F32)
            return thunk

        _conv_tile(zc_ref, cw_ref, cb_ref, lg_ref, lb_ref, tail_ref, abuf_ref, shift_ref, ycs_ref,
                   between=[stage_gate(bi) for bi in range(3)])
        yc = ycs_ref[...]
        gates = [gate_ref[bi] + bg_ref[:, bi * d:(bi + 1) * d] for bi in range(3)]
        halves = []
        for hp in range(2):
            o0, l0, o1, l1, o2, l2 = [attn_refs[4 * g + s_ + hp][...] for g in range(3) for s_ in (0, 2)]
            top = jnp.maximum(jnp.maximum(l0, l1), l2)
            e0, e1, e2 = jnp.exp(l0 - top), jnp.exp(l1 - top), jnp.exp(l2 - top)
            halves.append((e0 * o0 + e1 * o1 + e2 * o2) / (e0 + e1 + e2))
        ya = jnp.concatenate(halves, axis=1)
    merged = None
    for bi, (y, w_ref) in enumerate(((ys_ref[...], ws_ref), (yc, wc_ref), (ya, wa_ref))):
        part = _sigmoid(gates[bi]) * _mm(y, w_ref[...], precise)
        merged = part if merged is None else merged + part
    o_ref[...] = x + mod_ref[2] * _mm(merged, wo_ref[...], precise)


def _merge_prompt(x, modp, l, gnorm, w_gate, b_gate, y_ssm, zc, conv_w, conv_b, ln_g, ln_b, attn,
                  w_bs, w_bc, w_ba, w_out):
    b_sz, seq, d = x.shape
    tm = TM_MERGE
    row = lambda w: pl.BlockSpec((None, tm, w), lambda b, i: (b, i, 0))
    lsel = lambda b, i: (l, 0, 0)
    wsp = lambda r, c: pl.BlockSpec((None, r, c), lsel)
    wsp1 = lambda r, c: pl.BlockSpec((None, r, c), lsel, pipeline_mode=pl.Buffered(1))
    in_specs = [
        row(d),
        pl.BlockSpec((None, None, 6, 1, d), lambda b, i: (l, b, 0, 0, 0)),
        wsp(1, d), wsp1(d, 3 * d), wsp(1, 3 * d),
        row(SSM_WIDTH), row(2 * CONV_WIDTH),
        pl.BlockSpec((None, CONV_K, SUBLANES, CONV_WIDTH), lambda b, i: (l, 0, 0, 0)),
        wsp(1, CONV_WIDTH), wsp(1, CONV_WIDTH), wsp(1, CONV_WIDTH),
    ] + [row(LANES)] * 12 + [wsp1(SSM_WIDTH, d), wsp1(CONV_WIDTH, d), wsp1(ATTN_OUT, d), wsp1(d, d)]
    flat = [t for group in attn for t in group]
    return pl.pallas_call(
        functools.partial(_merge_body, sample=False),
        grid=(b_sz, seq // tm),
        in_specs=in_specs,
        out_specs=[row(d), pl.BlockSpec((None, CONV_HALO, CONV_WIDTH), lambda b, i: (b, 0, 0))],
        out_shape=[jax.ShapeDtypeStruct((b_sz, seq, d), F32),
                   jax.ShapeDtypeStruct((b_sz, CONV_HALO, CONV_WIDTH), F32)],
        scratch_shapes=[pltpu.VMEM((tm + CONV_HALO, CONV_WIDTH), F32),
                        pltpu.VMEM((SUBLANES - 1, tm + CONV_HALO - SUBLANES, CONV_WIDTH), F32),
                        pltpu.VMEM((tm, CONV_WIDTH), BF16),
                        pltpu.VMEM((tm, d), BF16),
                        pltpu.VMEM((3, tm, d), F32)],
        compiler_params=_cparams(("arbitrary", "arbitrary"), 56),
        name="merge_prompt",
    )(x, modp, gnorm, w_gate, b_gate, y_ssm, zc, conv_w, conv_b, ln_g, ln_b, *flat, w_bs, w_bc, w_ba, w_out)


def _merge_sample(x, mods, l, gnorm, w_gate, b_gate, y_ssm, y_conv, y_attn, w_bs, w_bc, w_ba, w_out):
    n, d = x.shape
    full = lambda *shape: pl.BlockSpec(shape, lambda i: (0,) * len(shape))
    lsel = lambda i: (l, 0, 0)
    wsp = lambda r, c: pl.BlockSpec((None, r, c), lsel)
    in_specs = [
        full(n, d),
        pl.BlockSpec((None, 6, n, d), lambda i: (l, 0, 0, 0)),
        wsp(1, d), wsp(d, 3 * d), wsp(1, 3 * d),
        full(n, SSM_WIDTH), full(n, CONV_WIDTH), full(n, ATTN_OUT),
        wsp(SSM_WIDTH, d), wsp(CONV_WIDTH, d), wsp(ATTN_OUT, d), wsp(d, d),
    ]
    return pl.pallas_call(
        functools.partial(_merge_body, sample=True),
        grid=(1,),
        in_specs=in_specs,
        out_specs=full(n, d),
        out_shape=jax.ShapeDtypeStruct((n, d), F32),
        compiler_params=_cparams(("arbitrary",), 56),
        name="merge_sample",
    )(x, mods, gnorm, w_gate, b_gate, y_ssm, y_conv, y_attn, w_bs, w_bc, w_ba, w_out)


def _shift_rows(x, hist, k):
    r = pltpu.roll(x, k, 0)
    row = lax.broadcasted_iota(jnp.int32, hist.shape, 0)
    head = jnp.where(row < k, pltpu.roll(hist, k, 0), r[0:SUBLANES])
    return jnp.concatenate([head, r[SUBLANES:]], axis=0)


def _ffn_prompt_body(x_ref, mod_ref, g_ref, wup_ref, cw_ref, cb_ref, wdn_ref, o_ref, tail_ref,
                     hist_ref, hb_ref, up_ref, act_ref):
    i = pl.program_id(1)
    rows = x_ref.shape[0]
    f2 = 2 * FFN_HIDDEN

    @pl.when(i == 0)
    def _():
        hist_ref[...] = jnp.zeros((HIST, f2), F32)

    x = x_ref[...]
    hb_ref[...] = _norm_mod(x, g_ref[...], mod_ref[4], mod_ref[3]).astype(BF16)
    acc = jnp.zeros((rows, D_MODEL), F32)
    cw = FFN_CHUNK
    n_chunks = FFN_HIDDEN // cw

    def project(c):
        for half, base in enumerate((c * cw, FFN_HIDDEN + c * cw)):
            up_ref[c % 2, half] = jnp.dot(hb_ref[...], wup_ref[:, base:base + cw],
                                          preferred_element_type=F32)

    def contract(c, acc):
        return acc + jnp.dot(act_ref[c % 2], wdn_ref[c * cw:(c + 1) * cw, :], preferred_element_type=F32)

    project(0)
    for c in range(n_chunks):
        if c + 1 < n_chunks:
            project(c + 1)
        if c > 0:
            acc = contract(c - 1, acc)
        halves = []
        for half, base in enumerate((c * cw, FFN_HIDDEN + c * cw)):
            cols = slice(base, base + cw)
            up = up_ref[c % 2, half]
            hist = hist_ref[:, cols]
            hist_ref[:, cols] = up[rows - HIST:rows]
            halves.append(cw_ref[0:1, cols] * _shift_rows(up, hist, 2)
                          + cw_ref[1:2, cols] * _shift_rows(up, hist, 1)
                          + cw_ref[2:3, cols] * up + cb_ref[:, cols])
        act_ref[c % 2] = (_gelu_tanh(halves[0]) * halves[1]).astype(BF16)
    acc = contract(n_chunks - 1, acc)
    o_ref[...] = x_ref[...] + mod_ref[5] * acc
    tail_ref[...] = hist_ref[...]


def _ffn_prompt(x, modp, l, gnorm, w_up, conv_w, conv_b, w_down):
    b_sz, seq, d = x.shape
    tm = TM_FFN
    f2 = 2 * FFN_HIDDEN
    lsel = lambda b, i: (l, 0, 0)
    once = pl.Buffered(1)
    return pl.pallas_call(
        _ffn_prompt_body,
        grid=(b_sz, seq // tm),
        in_specs=[
            pl.BlockSpec((None, tm, d), lambda b, i: (b, i, 0)),
            pl.BlockSpec((None, None, 6, 1, d), lambda b, i: (l, b, 0, 0, 0)),
            pl.BlockSpec((None, 1, d), lsel),
            pl.BlockSpec((None, d, f2), lsel, pipeline_mode=once),
            pl.BlockSpec((None, FFN_CONV_K, f2), lsel),
            pl.BlockSpec((None, 1, f2), lsel),
            pl.BlockSpec((None, FFN_HIDDEN, d), lsel, pipeline_mode=once),
        ],
        out_specs=[
            pl.BlockSpec((None, tm, d), lambda b, i: (b, i, 0)),
            pl.BlockSpec((None, HIST, f2), lambda b, i: (b, 0, 0)),
        ],
        out_shape=[jax.ShapeDtypeStruct((b_sz, seq, d), F32),
                   jax.ShapeDtypeStruct((b_sz, HIST, f2), F32)],
        scratch_shapes=[pltpu.VMEM((HIST, f2), F32),
                        pltpu.VMEM((tm, d), BF16),
                        pltpu.VMEM((2, 2, tm, FFN_CHUNK), F32),
                        pltpu.VMEM((2, tm, FFN_CHUNK), BF16)],
        compiler_params=_cparams(("arbitrary", "arbitrary"), 56),
        name="ffn_prompt",
    )(x, modp, gnorm, w_up, conv_w, conv_b, w_down)


def _ffn_sample_body(x_ref, mod_ref, g_ref, wup_ref, c0_ref, c1_ref, cw_ref, cb_ref, wdn_ref, o_ref, up_ref):
    x = x_ref[...]
    h2 = _norm_mod(x, g_ref[...], mod_ref[4], mod_ref[3])
    up = _mm(h2, wup_ref[...], True)
    up_ref[...] = up
    cv = cw_ref[0:1, :] * c0_ref[...] + cw_ref[1:2, :] * c1_ref[...] + cw_ref[2:3, :] * up + cb_ref[...]
    act = _gelu_tanh(cv[:, 0:FFN_HIDDEN]) * cv[:, FFN_HIDDEN:2 * FFN_HIDDEN]
    o_ref[...] = x + mod_ref[5] * _mm(act, wdn_ref[...], True)


def _ffn_sample(x, mods, l, gnorm, w_up, c0, c1, conv_w, conv_b, w_down):
    n, d = x.shape
    f2 = 2 * FFN_HIDDEN
    full = lambda *shape: pl.BlockSpec(shape, lambda i: (0,) * len(shape))
    lsel = lambda i: (l, 0, 0)
    once = pl.Buffered(1)
    return pl.pallas_call(
        _ffn_sample_body,
        grid=(1,),
        in_specs=[
            full(n, d),
            pl.BlockSpec((None, 6, n, d), lambda i: (l, 0, 0, 0)),
            pl.BlockSpec((None, 1, d), lsel),
            pl.BlockSpec((None, d, f2), lsel, pipeline_mode=once),
            pl.BlockSpec((None, n, f2), lsel), pl.BlockSpec((None, n, f2), lsel),
            pl.BlockSpec((None, FFN_CONV_K, f2), lsel),
            pl.BlockSpec((None, 1, f2), lsel),
            pl.BlockSpec((None, FFN_HIDDEN, d), lsel, pipeline_mode=once),
        ],
        out_specs=[full(n, d), full(n, f2)],
        out_shape=[jax.ShapeDtypeStruct((n, d), F32), jax.ShapeDtypeStruct((n, f2), F32)],
        compiler_params=_cparams(("arbitrary",), 56),
        name="ffn_sample",
    )(x, mods, gnorm, w_up, c0, c1, conv_w, conv_b, w_down)


def _rope_tables(pos):
    half = HEAD_DIM // 2
    inv = ROPE_THETA ** (-jnp.arange(half, dtype=F32) / half)
    ang = pos.astype(F32)[:, None] * inv[None, :]
    cos, sin = jnp.cos(ang), jnp.sin(ang)
    cos_h = jnp.concatenate([cos, cos], axis=1)
    sin_h = jnp.concatenate([-sin, sin], axis=1)
    reps = LANES // HEAD_DIM
    return jnp.tile(cos_h, (1, reps)), jnp.tile(sin_h, (1, reps))


def _residue_perm(rows, dil):
    i = jnp.arange(rows)
    src = (i % (rows // dil)) * dil + i // (rows // dil)
    return (src[:, None] == jnp.arange(rows)[None, :]).astype(BF16)


def _block_ones(dtype):
    i = jnp.arange(MXU_DIM) // HEAD_DIM
    return (i[:, None] == i[None, :]).astype(dtype)


def kernel(x_prompt, x_sample, state_ssm, cache_conv, cache_kv_w128, cache_kv_w512, cache_kv_w2048, cache_ffn, c_prompt, c_sample, w_ada, b_ada, g_norm_mix, w_in, ssm_a_re, ssm_a_im, ssm_log_dt, ssm_b_re, ssm_b_im, ssm_c_re, ssm_c_im, ssm_d, ssm_w_glu, ssm_b_glu, conv_w, conv_b, conv_ln_g, conv_ln_b, attn_gq, attn_gk, w_gate, b_gate, w_br_ssm, w_br_conv, w_br_attn, w_out, g_norm_ffn, ffn_w_up, ffn_conv_w, ffn_conv_b, ffn_w_down):
    bp, seq, d = x_prompt.shape
    ns = x_sample.shape[0]
    depth = w_ada.shape[0]
    f2 = 2 * FFN_HIDDEN
    assert x_sample.shape[1] == 1
    assert all(seq % (ATTN_BLOCK * dil * ATTN_STEP_BLOCKS[dil]) == 0 for _, dil in ATTN_GROUPS)
    caches = (cache_kv_w128, cache_kv_w512, cache_kv_w2048)

    pad = (-(bp + ns)) % SUBLANES
    c_all = jnp.concatenate([c_prompt, c_sample, jnp.zeros((pad, d), F32)], axis=0)
    mod = _ada_call(c_all, w_ada, b_ada)
    modp = mod[:, :bp].reshape(depth, bp, 6, 1, d)
    mods = mod[:, bp:bp + ns].reshape(depth, ns, 6, d).transpose(0, 2, 1, 3)

    bf = lambda w: w.astype(BF16)
    w_in_b, w_gate_b, w_out_b = bf(w_in), bf(w_gate), bf(w_out)
    w_bs_b, w_bc_b, w_ba_b = bf(w_br_ssm), bf(w_br_conv), bf(w_br_attn)
    w_up_b, w_dn_b, w_glu_b = bf(ffn_w_up), bf(ffn_w_down), bf(ssm_w_glu)

    row3 = lambda t: t.reshape(depth, 1, t.shape[-1])
    g_mix, g_ffn = row3(g_norm_mix), row3(g_norm_ffn)
    gq = row3(jnp.tile(attn_gq, (1, ATTN_HEADS)))
    gk = row3(jnp.tile(attn_gk, (1, ATTN_HEADS)))
    b_gate3, ssm_d3, b_glu3 = row3(b_gate), row3(ssm_d), row3(ssm_b_glu)
    conv_b3, ln_g3, ln_b3, ffn_cb3 = row3(conv_b), row3(conv_ln_g), row3(conv_ln_b), row3(ffn_conv_b)
    conv_w4 = conv_w.reshape(depth, CONV_K, 1, CONV_WIDTH)
    conv_w8 = jnp.repeat(conv_w4, SUBLANES, axis=2)

    cos_p, sin_p = _rope_tables(jnp.arange(seq, dtype=jnp.int32))
    cos_s, sin_s = _rope_tables(jnp.full((ns,), PAST_LEN, dtype=jnp.int32))
    ones_b, ones_f = _block_ones(BF16), _block_ones(F32)
    p4 = _residue_perm(TM_INPROJ, ATTN_GROUPS[1][1])
    p16 = _residue_perm(TM_INPROJ, ATTN_GROUPS[2][1])
    seg = L_SSM // SUBLANES
    pm = _residue_perm(L_SSM, seg)
    pmt = pm.T
    zero_state = jnp.zeros((bp, 1, STATE_W), F32)

    tabs = _ssm_tables(ssm_a_re, ssm_a_im, ssm_log_dt, ssm_b_re, ssm_b_im, ssm_c_re, ssm_c_im, seg)
    state_slab = _state_to_slab(state_ssm.reshape((depth * ns,) + state_ssm.shape[2:])).reshape(depth, ns, STATE_W)
    conv_hist = cache_conv.transpose(0, 2, 1, 3)
    ffn_old, ffn_new = cache_ffn[:, :, 0], cache_ffn[:, :, 1]
    caches_t = [c.transpose(0, 1, 3, 4, 5, 2) for c in caches]

    yp = x_prompt
    ys = x_sample.reshape(ns, d)
    st_p, st_s, a_tails, a_news, up_tails, up_news, k_tails, v_tails = [], [], [], [], [], [], [], []
    rolled = None
    for l in range(depth):
        (zs, zc, k_tail, v_tail, q0, k0, v0, q1, k1, v1, q2, k2, v2) = _inproj_prompt(
            yp, modp, l, g_mix, w_in_b, gq, gk, cos_p, sin_p, ones_b, p4, p16)
        y_ssm, st = _ssm_prompt(zs, zero_state, l, pm, pmt, tabs, ssm_d3, w_glu_b, b_glu3)
        attn = [_attn_prompt(q0, k0, v0, 1), _attn_prompt(q1, k1, v1, ATTN_GROUPS[1][1]),
                _attn_prompt(q2, k2, v2, ATTN_GROUPS[2][1])]
        x_mid, a_tail = _merge_prompt(yp, modp, l, g_mix, w_gate_b, b_gate3, y_ssm, zc, conv_w8, conv_b3,
                                      ln_g3, ln_b3, attn, w_bs_b, w_bc_b, w_ba_b, w_out_b)
        yp, up_tail = _ffn_prompt(x_mid, modp, l, g_ffn, w_up_b, ffn_conv_w, ffn_cb3, w_dn_b)
        st_p.append(st)
        a_tails.append(a_tail)
        up_tails.append(up_tail)
        k_tails.append(k_tail)
        v_tails.append(v_tail)

        zs_s, zc_s, q_s, k_s, v_s = _inproj_sample(ys, mods, l, g_mix, w_in, gq, gk, cos_s, sin_s, ones_f)
        y_ssm_s, st_l = _ssm_sample(zs_s, state_slab, l, tabs, ssm_d3, ssm_w_glu, b_glu3)
        y_conv_s, a_s = _conv_sample(zc_s, conv_hist, l, conv_w4, conv_b3, ln_g3, ln_b3)
        rows8 = lambda t: jnp.pad(t.reshape(ns, ATTN_WIDTH // LANES, LANES),
                                  ((0, 0), (0, SUBLANES - ATTN_WIDTH // LANES), (0, 0)))
        y_attn_s, *rolled = _attn_roll(rows8(q_s), rows8(k_s), rows8(v_s), caches_t, rolled, l)
        y_attn_s = y_attn_s[:, :ATTN_OUT // LANES].reshape(ns, ATTN_OUT)
        x_mid_s = _merge_sample(ys, mods, l, g_mix, w_gate, b_gate3, y_ssm_s, y_conv_s, y_attn_s,
                                w_br_ssm, w_br_conv, w_br_attn, w_out)
        ys, up_s = _ffn_sample(x_mid_s, mods, l, g_ffn, ffn_w_up, ffn_old, ffn_new, ffn_conv_w, ffn_cb3,
                               ffn_w_down)
        st_s.append(st_l)
        a_news.append(a_s)
        up_news.append(up_s)

    stk = lambda t: jnp.stack(t, axis=0)
    unslab = lambda t, n: _slab_to_state(t.reshape(depth * n, STATE_W)).reshape(depth, n, SSM_GROUPS, SSM_N, 2)
    ssm_p = unslab(stk(st_p), bp)
    ssm_s = unslab(stk(st_s), ns)
    conv_p = stk(a_tails)[:, :, CONV_HALO - (CONV_K - 1):]
    conv_s = jnp.concatenate([cache_conv[:, :, 1:], stk(a_news)[:, :, None, :]], axis=2)
    ffn_p = stk(up_tails)[:, :, HIST - (FFN_CONV_K - 1):]
    ffn_s = jnp.stack([ffn_new, stk(up_news)], axis=2)
    kt, vt = stk(k_tails), stk(v_tails)
    gw = HEADS_PER_GROUP * HEAD_DIM
    kv_p = []
    for gi, (win, _) in enumerate(ATTN_GROUPS):
        keep = min(win, seq)
        pick = lambda t: t[:, :, gi * gw:(gi + 1) * gw, t.shape[-1] - keep:].reshape(
            depth, bp, HEADS_PER_GROUP, HEAD_DIM, keep)
        kv_p.append(jnp.stack([pick(kt), pick(vt)], axis=2).transpose(0, 1, 5, 2, 3, 4))
    kv_s = [t.transpose(0, 1, 5, 2, 3, 4) for t in rolled]
    return (yp, ys.reshape(ns, 1, d), ssm_p, ssm_s, conv_p, conv_s,
            kv_p[0], kv_s[0], kv_p[1], kv_s[1], kv_p[2], kv_s[2], ffn_p, ffn_s)
```

```python
import functools
import math

import jax
import jax.numpy as jnp
from jax import lax
from jax.experimental import pallas as pl
from jax.experimental.pallas import tpu as pltpu

F32 = jnp.float32
BF16 = jnp.bfloat16

D_MODEL = 1024
SSM_WIDTH = 512
SSM_GROUP = 16
SSM_GROUPS = 32
SSM_N = 64
CONV_WIDTH = 512
CONV_K = 31
HEAD_DIM = 64
HEADS_PER_GROUP = 4
ATTN_GROUPS = ((128, 1), (512, 4), (2048, 16))
ATTN_HEADS = 12
ATTN_WIDTH = 768
ATTN_OUT = 256
ATTN_BLOCK = 128
ROPE_THETA = 10000.0
FFN_HIDDEN = 2816
FFN_CONV_K = 3
IN_WIDTH = SSM_WIDTH + 2 * CONV_WIDTH + 3 * ATTN_WIDTH
EPS = 1e-6
PAST_LEN = 8192

SUBLANES = 8
LANES = 128
MXU_DIM = 256
SLAB_GROUPS = 8
N_SLABS = SSM_GROUPS // SLAB_GROUPS
SLAB_CH = SLAB_GROUPS * SSM_GROUP
SLAB_ST = SLAB_GROUPS * SSM_N
STATE_W = 2 * SSM_GROUPS * SSM_N
NEG_BIG = -1e30
GELU_C = math.sqrt(2.0 / math.pi)

TM_INPROJ = 512
L_SSM = 256
TM_MERGE = 256
TM_FFN = 256
FFN_CHUNK = 256
HIST = 8
CONV_HALO = 32
CONV_ROW_BLOCK = 32
KV_TAIL = 2048
ATTN_STEP_BLOCKS = {1: 8, 4: 8, 16: 4}


def _cparams(sem, vmem_mb):
    return pltpu.CompilerParams(dimension_semantics=sem, vmem_limit_bytes=vmem_mb * 1024 * 1024)


def _sigmoid(x):
    return 1.0 / (1.0 + jnp.exp(-x))


def _gelu_tanh(x):
    neg_2u = x * ((-2.0 * GELU_C * 0.044715) * (x * x) - 2.0 * GELU_C)
    return x / (1.0 + jnp.exp(neg_2u))


def _mm(a, w, precise):
    if precise:
        return jnp.dot(a, w, precision=lax.Precision.HIGHEST, preferred_element_type=F32)
    return jnp.dot(a.astype(BF16), w, preferred_element_type=F32)


def _norm_mod(x, g, scale, shift):
    ms = jnp.mean(x * x, axis=-1, keepdims=True)
    return (x * lax.rsqrt(ms + EPS) * g) * (1.0 + scale) + shift


def _ada_body(c_ref, w_ref, b_ref, o_ref):
    c = c_ref[...]
    s = c * _sigmoid(c)
    o_ref[...] = jnp.dot(s, w_ref[...], precision=lax.Precision.HIGHEST,
                         preferred_element_type=F32) + b_ref[...]


def _ada_call(c_all, w_ada, b_ada):
    depth, d, n6 = w_ada.shape
    rows = c_all.shape[0]
    tn = 1536
    return pl.pallas_call(
        _ada_body,
        grid=(depth, n6 // tn),
        in_specs=[
            pl.BlockSpec((rows, d), lambda l, j: (0, 0)),
            pl.BlockSpec((None, d, tn), lambda l, j: (l, 0, j)),
            pl.BlockSpec((None, 1, tn), lambda l, j: (l, 0, j)),
        ],
        out_specs=pl.BlockSpec((None, rows, tn), lambda l, j: (l, 0, j)),
        out_shape=jax.ShapeDtypeStruct((depth, rows, n6), F32),
        compiler_params=_cparams(("arbitrary", "arbitrary"), 40),
        name="ada_mod",
    )(c_all, w_ada, b_ada.reshape(depth, 1, n6))


def _head_norm_rope(z, gvec, ones, cos, sins, precise):
    sq = z * z
    parts = []
    for c in range(ATTN_WIDTH // MXU_DIM):
        parts.append(_mm(sq[:, c * MXU_DIM:(c + 1) * MXU_DIM], ones, precise))
    ms = jnp.concatenate(parts, axis=1) * (1.0 / HEAD_DIM)
    y = z * lax.rsqrt(ms + EPS) * gvec
    lane = lax.broadcasted_iota(jnp.int32, (1, LANES), 1)
    first = (lane % HEAD_DIM) < (HEAD_DIM // 2)
    outs = []
    for c in range(ATTN_WIDTH // LANES):
        yc = y[:, c * LANES:(c + 1) * LANES]
        partner = jnp.where(first, pltpu.roll(yc, LANES - HEAD_DIM // 2, 1),
                            pltpu.roll(yc, HEAD_DIM // 2, 1))
        outs.append(yc * cos + partner * sins)
    return jnp.concatenate(outs, axis=1)


def _inproj_body(*refs, sample, first_tail=0):
    if sample:
        (x_ref, mod_ref, g_ref, w_ref, gq_ref, gk_ref, cos_ref, sin_ref, ones_ref,
         zs_ref, zc_ref, q_ref, k_ref, v_ref) = refs
    else:
        (x_ref, mod_ref, g_ref, w_ref, gq_ref, gk_ref, cos_ref, sin_ref, ones_ref, p4_ref, p16_ref,
         zs_ref, zc_ref, kt_ref, vt_ref,
         q0_ref, k0_ref, v0_ref, q1_ref, k1_ref, v1_ref, q2_ref, k2_ref, v2_ref, kf_ref, vf_ref) = refs
    precise = sample
    x = x_ref[...]
    h = _norm_mod(x, g_ref[...], mod_ref[1], mod_ref[0])
    hm = h if precise else h.astype(BF16)
    c1 = SSM_WIDTH
    c2 = c1 + 2 * CONV_WIDTH
    c3 = c2 + ATTN_WIDTH
    c4 = c3 + ATTN_WIDTH
    zs_ref[...] = _mm(hm, w_ref[:, 0:c1], precise)
    zc_ref[...] = _mm(hm, w_ref[:, c1:c2], precise)
    zq = _mm(hm, w_ref[:, c2:c3], precise)
    zk = _mm(hm, w_ref[:, c3:c4], precise)
    zv = _mm(hm, w_ref[:, c4:IN_WIDTH], precise)
    cos = cos_ref[...]
    sins = sin_ref[...]
    ones = ones_ref[...]
    q = _head_norm_rope(zq, gq_ref[...], ones, cos, sins, precise)
    k = _head_norm_rope(zk, gk_ref[...], ones, cos, sins, precise)
    if sample:
        q_ref[...] = q
        k_ref[...] = k
        v_ref[...] = zv
        return
    kf_ref[...] = k
    vf_ref[...] = zv
    qb = (q * (HEAD_DIM ** -0.5)).astype(BF16)
    kb = k.astype(BF16)
    vb = zv.astype(BF16)
    gw = HEADS_PER_GROUP * HEAD_DIM
    q0_ref[0] = qb[:, 0:gw]
    k0_ref[0] = kb[:, 0:gw]
    v0_ref[0] = vb[:, 0:gw]
    for gi, (p_ref, outs) in enumerate(((p4_ref, (q1_ref, k1_ref, v1_ref)),
                                        (p16_ref, (q2_ref, k2_ref, v2_ref))), start=1):
        pm = p_ref[...]
        for src, o_ref in zip((qb, kb, vb), outs):
            dil, rows = o_ref.shape[0], o_ref.shape[1]
            perm = jnp.dot(pm, src[:, gi * gw:(gi + 1) * gw], preferred_element_type=F32).astype(BF16)
            for r in range(dil):
                o_ref[r] = perm[r * rows:(r + 1) * rows]

    @pl.when(pl.program_id(1) >= first_tail)
    def _():
        kt_ref[...] = kf_ref[...].T
        vt_ref[...] = vf_ref[...].T


def _inproj_prompt(x, modp, l, gnorm, w_in, gq, gk, cos, sins, ones, p4, p16):
    b_sz, seq, d = x.shape
    tm = TM_INPROJ
    nt = seq // tm
    tail = min(KV_TAIL, seq)
    ft = nt - tail // tm
    gw = HEADS_PER_GROUP * HEAD_DIM
    tmap = lambda b, i: (b, 0, jnp.maximum(i - ft, 0))
    in_specs = [
        pl.BlockSpec((None, tm, d), lambda b, i: (b, i, 0)),
        pl.BlockSpec((None, None, 6, 1, d), lambda b, i: (l, b, 0, 0, 0)),
        pl.BlockSpec((None, 1, d), lambda b, i: (l, 0, 0)),
        pl.BlockSpec((None, d, IN_WIDTH), lambda b, i: (l, 0, 0), pipeline_mode=pl.Buffered(1)),
        pl.BlockSpec((None, 1, ATTN_WIDTH), lambda b, i: (l, 0, 0)),
        pl.BlockSpec((None, 1, ATTN_WIDTH), lambda b, i: (l, 0, 0)),
        pl.BlockSpec((tm, LANES), lambda b, i: (i, 0)),
        pl.BlockSpec((tm, LANES), lambda b, i: (i, 0)),
        pl.BlockSpec((MXU_DIM, MXU_DIM), lambda b, i: (0, 0)),
        pl.BlockSpec((tm, tm), lambda b, i: (0, 0)),
        pl.BlockSpec((tm, tm), lambda b, i: (0, 0)),
    ]
    out_shape = [
        jax.ShapeDtypeStruct((b_sz, seq, SSM_WIDTH), F32),
        jax.ShapeDtypeStruct((b_sz, seq, 2 * CONV_WIDTH), F32),
        jax.ShapeDtypeStruct((b_sz, ATTN_WIDTH, tail), F32),
        jax.ShapeDtypeStruct((b_sz, ATTN_WIDTH, tail), F32),
    ]
    out_specs = [
        pl.BlockSpec((None, tm, SSM_WIDTH), lambda b, i: (b, i, 0)),
        pl.BlockSpec((None, tm, 2 * CONV_WIDTH), lambda b, i: (b, i, 0)),
        pl.BlockSpec((None, ATTN_WIDTH, tm), tmap),
        pl.BlockSpec((None, ATTN_WIDTH, tm), tmap),
    ]
    for _, dil in ATTN_GROUPS:
        for _ in range(3):
            out_shape.append(jax.ShapeDtypeStruct((b_sz, dil, seq // dil, gw), BF16))
            out_specs.append(pl.BlockSpec((None, dil, tm // dil, gw), lambda b, i: (b, 0, i, 0)))
    return pl.pallas_call(
        functools.partial(_inproj_body, sample=False, first_tail=ft),
        grid=(b_sz, nt),
        in_specs=in_specs,
        out_specs=out_specs,
        out_shape=out_shape,
        scratch_shapes=[pltpu.VMEM((tm, ATTN_WIDTH), F32), pltpu.VMEM((tm, ATTN_WIDTH), F32)],
        compiler_params=_cparams(("arbitrary", "arbitrary"), 56),
        name="inproj_prompt",
    )(x, modp, gnorm, w_in, gq, gk, cos, sins, ones, p4, p16)


def _inproj_sample(x, mods, l, gnorm, w_in, gq, gk, cos, sins, ones):
    n, d = x.shape
    lsel = lambda i: (l, 0, 0)
    in_specs = [
        pl.BlockSpec((n, d), lambda i: (0, 0)),
        pl.BlockSpec((None, 6, n, d), lambda i: (l, 0, 0, 0)),
        pl.BlockSpec((None, 1, d), lsel),
        pl.BlockSpec((None, d, IN_WIDTH), lsel),
        pl.BlockSpec((None, 1, ATTN_WIDTH), lsel),
        pl.BlockSpec((None, 1, ATTN_WIDTH), lsel),
        pl.BlockSpec((n, LANES), lambda i: (0, 0)),
        pl.BlockSpec((n, LANES), lambda i: (0, 0)),
        pl.BlockSpec((MXU_DIM, MXU_DIM), lambda i: (0, 0)),
    ]
    widths = (SSM_WIDTH, 2 * CONV_WIDTH, ATTN_WIDTH, ATTN_WIDTH, ATTN_WIDTH)
    return pl.pallas_call(
        functools.partial(_inproj_body, sample=True),
        grid=(1,),
        in_specs=in_specs,
        out_specs=[pl.BlockSpec((n, w), lambda i: (0, 0)) for w in widths],
        out_shape=[jax.ShapeDtypeStruct((n, w), F32) for w in widths],
        compiler_params=_cparams(("arbitrary",), 48),
        name="inproj_sample",
    )(x, mods, gnorm, w_in, gq, gk, cos, sins, ones)


def _ssm_tail(y_raw, u, d_ref, wglu_ref, bglu_ref, precise):
    y = _gelu_tanh(y_raw + d_ref[...] * u)
    return y * _sigmoid(_mm(y, wglu_ref[...], precise) + bglu_ref[...])


def _ssm_prompt_body(zs_ref, s0_ref, pm_ref, pmt_ref, b_ref, c_ref, lam_ref, lamp_ref, pw_ref,
                     d_ref, wglu_ref, bglu_ref, y_ref, st_ref,
                     v_ref, sb_ref, cs_ref, carry_ref):
    ci = pl.program_id(1)
    chunk = zs_ref.shape[0]
    seg = chunk // SUBLANES

    @pl.when(ci == 0)
    def _():
        carry_ref[...] = s0_ref[...]

    u = zs_ref[...]
    up = jnp.dot(pm_ref[...], u.astype(BF16), preferred_element_type=F32).astype(BF16)
    re, im = slice(0, SLAB_ST), slice(SLAB_ST, 2 * SLAB_ST)
    for m in range(N_SLABS):
        v_ref[m] = jnp.dot(up[:, m * SLAB_CH:(m + 1) * SLAB_CH], b_ref[m], preferred_element_type=F32)
    y_parts = [None] * N_SLABS
    for pair in range(0, N_SLABS, 2):
        slabs = (pair, pair + 1)
        zero = jnp.zeros((SUBLANES, SLAB_ST), F32)
        state = {m: (zero, zero) for m in slabs}
        for i in range(seg):
            rows = slice(i * SUBLANES, (i + 1) * SUBLANES)
            for m in slabs:
                lr = lam_ref[m, 0]
                li = lam_ref[m, 1]
                sr, si = state[m]
                nsr = lr * sr - li * si + v_ref[m, rows, re]
                nsi = lr * si + li * sr + v_ref[m, rows, im]
                v_ref[m, rows, re] = nsr
                v_ref[m, rows, im] = nsi
                state[m] = (nsr, nsi)
        for m in slabs:
            base = m * 2 * SLAB_ST
            sr, si = state[m]
            er = carry_ref[0:1, base:base + SLAB_ST]
            ei = carry_ref[0:1, base + SLAB_ST:base + 2 * SLAB_ST]
            pr = lamp_ref[m, 0:1, :]
            pi = lamp_ref[m, 1:2, :]
            for r in range(SUBLANES):
                cs_ref[m, r:r + 1, re] = er
                cs_ref[m, r:r + 1, im] = ei
                ner = sr[r:r + 1] + pr * er - pi * ei
                nei = si[r:r + 1] + pr * ei + pi * er
                er, ei = ner, nei
            carry_ref[0:1, base:base + SLAB_ST] = er
            carry_ref[0:1, base + SLAB_ST:base + 2 * SLAB_ST] = ei
        for m in slabs:
            csr = cs_ref[m, :, re]
            csi = cs_ref[m, :, im]
            for i2 in range(seg // 2):
                rows16 = slice(i2 * 2 * SUBLANES, (i2 + 1) * 2 * SUBLANES)
                rows_r, rows_i = [], []
                for h in range(2):
                    i = i2 * 2 + h
                    rows = slice(i * SUBLANES, (i + 1) * SUBLANES)
                    qr = pw_ref[m, 0, rows, :]
                    qi = pw_ref[m, 1, rows, :]
                    rows_r.append(v_ref[m, rows, re] + qr * csr - qi * csi)
                    rows_i.append(v_ref[m, rows, im] + qr * csi + qi * csr)
                sb_ref[m, rows16, re] = jnp.concatenate(rows_r, axis=0).astype(BF16)
                sb_ref[m, rows16, im] = jnp.concatenate(rows_i, axis=0).astype(BF16)
            y_parts[m] = jnp.dot(sb_ref[m], c_ref[m], preferred_element_type=F32)
    y_perm = jnp.concatenate(y_parts, axis=1)
    hi = y_perm.astype(BF16)
    lo = (y_perm - hi.astype(F32)).astype(BF16)
    pmt = pmt_ref[...]
    y_nat = jnp.dot(pmt, hi, preferred_element_type=F32) + jnp.dot(pmt, lo, preferred_element_type=F32)
    y_ref[...] = _ssm_tail(y_nat, u, d_ref, wglu_ref, bglu_ref, False).astype(BF16)
    st_ref[...] = carry_ref[...]


def _ssm_prompt(zs, s0, l, pm, pmt, tabs, ssm_d, w_glu, b_glu):
    b_sz, seq, _ = zs.shape
    chunk = L_SSM
    full = lambda *shape: pl.BlockSpec(shape, lambda b, c: (0,) * len(shape))
    tab = lambda *shape: pl.BlockSpec((None,) + shape, lambda b, c: (l,) + (0,) * len(shape))
    lsel = lambda b, c: (l, 0, 0)
    return pl.pallas_call(
        _ssm_prompt_body,
        grid=(b_sz, seq // chunk),
        in_specs=[
            pl.BlockSpec((None, chunk, SSM_WIDTH), lambda b, c: (b, c, 0)),
            pl.BlockSpec((None, 1, STATE_W), lambda b, c: (b, 0, 0)),
            full(chunk, chunk), full(chunk, chunk),
            tab(N_SLABS, SLAB_CH, 2 * SLAB_ST),
            tab(N_SLABS, 2 * SLAB_ST, SLAB_CH),
            tab(N_SLABS, 2, SUBLANES, SLAB_ST), tab(N_SLABS, 2, SLAB_ST),
            tab(N_SLABS, 2, chunk, SLAB_ST),
            pl.BlockSpec((None, 1, SSM_WIDTH), lsel),
            pl.BlockSpec((None, SSM_WIDTH, SSM_WIDTH), lsel),
            pl.BlockSpec((None, 1, SSM_WIDTH), lsel),
        ],
        out_specs=[
            pl.BlockSpec((None, chunk, SSM_WIDTH), lambda b, c: (b, c, 0)),
            pl.BlockSpec((None, 1, STATE_W), lambda b, c: (b, 0, 0)),
        ],
        out_shape=[jax.ShapeDtypeStruct((b_sz, seq, SSM_WIDTH), BF16),
                   jax.ShapeDtypeStruct((b_sz, 1, STATE_W), F32)],
        scratch_shapes=[
            pltpu.VMEM((N_SLABS, chunk, 2 * SLAB_ST), F32),
            pltpu.VMEM((N_SLABS, chunk, 2 * SLAB_ST), BF16),
            pltpu.VMEM((N_SLABS, SUBLANES, 2 * SLAB_ST), F32),
            pltpu.VMEM((1, STATE_W), F32),
        ],
        compiler_params=_cparams(("arbitrary", "arbitrary"), 40),
        name="ssm_prompt",
    )(zs, s0, pm, pmt, tabs["b_bf"], tabs["c_bf"], tabs["lam_rows"], tabs["lam_seg"], tabs["pw_rows"],
      ssm_d, w_glu, b_glu)


def _ssm_sample_body(u_ref, s0_ref, b_ref, c_ref, lam_ref, d_ref, wglu_ref, bglu_ref, y_ref, st_ref):
    u = u_ref[...]
    y_parts = []
    for m in range(N_SLABS):
        base = m * 2 * SLAB_ST
        v = _mm(u[:, m * SLAB_CH:(m + 1) * SLAB_CH], b_ref[m], True)
        lr = lam_ref[m, 0:1, :]
        li = lam_ref[m, 1:2, :]
        sr0 = s0_ref[:, base:base + SLAB_ST]
        si0 = s0_ref[:, base + SLAB_ST:base + 2 * SLAB_ST]
        sr = lr * sr0 - li * si0 + v[:, 0:SLAB_ST]
        si = lr * si0 + li * sr0 + v[:, SLAB_ST:2 * SLAB_ST]
        st_ref[:, base:base + SLAB_ST] = sr
        st_ref[:, base + SLAB_ST:base + 2 * SLAB_ST] = si
        y_parts.append(_mm(jnp.concatenate([sr, si], axis=1), c_ref[m], True))
    y_raw = jnp.concatenate(y_parts, axis=1)
    y_ref[...] = _ssm_tail(y_raw, u, d_ref, wglu_ref, bglu_ref, True)


def _ssm_sample(zs, s0, l, tabs, ssm_d, w_glu, b_glu):
    n = zs.shape[0]
    full = lambda *shape: pl.BlockSpec(shape, lambda i: (0,) * len(shape))
    tab = lambda *shape: pl.BlockSpec((None,) + shape, lambda i: (l,) + (0,) * len(shape))
    lsel = lambda i: (l, 0, 0)
    return pl.pallas_call(
        _ssm_sample_body,
        grid=(1,),
        in_specs=[
            full(n, SSM_WIDTH), tab(n, STATE_W),
            tab(N_SLABS, SLAB_CH, 2 * SLAB_ST), tab(N_SLABS, 2 * SLAB_ST, SLAB_CH),
            tab(N_SLABS, 2, SLAB_ST),
            pl.BlockSpec((None, 1, SSM_WIDTH), lsel),
            pl.BlockSpec((None, SSM_WIDTH, SSM_WIDTH), lsel),
            pl.BlockSpec((None, 1, SSM_WIDTH), lsel),
        ],
        out_specs=[full(n, SSM_WIDTH), full(n, STATE_W)],
        out_shape=[jax.ShapeDtypeStruct((n, SSM_WIDTH), F32), jax.ShapeDtypeStruct((n, STATE_W), F32)],
        compiler_params=_cparams(("arbitrary",), 32),
        name="ssm_sample",
    )(zs, s0, tabs["b_f32"], tabs["c_f32"], tabs["lam"], ssm_d, w_glu, b_glu)


def _ssm_tables(a_re, a_im, log_dt, b_re, b_im, c_re, c_im, seg):
    depth = a_re.shape[0]
    dt = jnp.exp(log_dt)[..., None]
    xr, xi = a_re * dt, a_im * dt
    mag = jnp.exp(xr)
    lr, li = mag * jnp.cos(xi), mag * jnp.sin(xi)
    den = a_re * a_re + a_im * a_im
    nr, ni = lr - 1.0, li
    cr = (nr * a_re + ni * a_im) / den
    cim = (ni * a_re - nr * a_im) / den
    bbr = cr[..., None] * b_re - cim[..., None] * b_im
    bbi = cr[..., None] * b_im + cim[..., None] * b_re
    eye = jnp.eye(SLAB_GROUPS, dtype=F32)

    def b_slab(t):
        t = t.reshape(depth, N_SLABS, SLAB_GROUPS, SSM_N, SSM_GROUP)
        return jnp.einsum('lmgnc,gh->lmgchn', t, eye).reshape(depth, N_SLABS, SLAB_CH, SLAB_ST)

    def c_slab(t):
        t = t.reshape(depth, N_SLABS, SLAB_GROUPS, SSM_GROUP, SSM_N)
        return jnp.einsum('lmgcn,gh->lmgnhc', t, eye).reshape(depth, N_SLABS, SLAB_ST, SLAB_CH)

    b_mat = jnp.concatenate([b_slab(bbr), b_slab(bbi)], axis=3)
    c_mat = jnp.concatenate([c_slab(c_re), c_slab(-c_im)], axis=2)

    def powers(ks):
        k = jnp.asarray(ks, F32)[:, None, None, None]
        mk = jnp.exp(k * xr[None])
        both = jnp.stack([mk * jnp.cos(k * xi[None]), mk * jnp.sin(k * xi[None])], axis=0)
        return both.reshape(2, len(ks), depth, N_SLABS, SLAB_ST).transpose(2, 3, 0, 1, 4)

    pw = powers(range(1, seg + 1))
    lam = powers([1])
    rep = lambda t: jnp.repeat(t, SUBLANES, axis=3)
    return dict(b_f32=b_mat, c_f32=c_mat, b_bf=b_mat.astype(BF16), c_bf=c_mat.astype(BF16),
                lam=lam[:, :, :, 0], lam_rows=rep(lam), lam_seg=powers([seg])[:, :, :, 0], pw_rows=rep(pw))


def _state_to_slab(s):
    b = s.shape[0]
    return s.reshape(b, N_SLABS, SLAB_GROUPS, SSM_N, 2).transpose(0, 1, 4, 2, 3).reshape(b, STATE_W)


def _slab_to_state(x):
    b = x.shape[0]
    return x.reshape(b, N_SLABS, 2, SLAB_GROUPS, SSM_N).transpose(0, 1, 3, 4, 2).reshape(b, SSM_GROUPS, SSM_N, 2)


def _ln_silu(y, g, b):
    mu = jnp.mean(y, axis=-1, keepdims=True)
    var = jnp.mean(jnp.square(y - mu), axis=-1, keepdims=True)
    t = (y - mu) * lax.rsqrt(var + EPS) * g + b
    return t * _sigmoid(t)


def _conv_tile(zc_ref, w_ref, b_ref, lg_ref, lb_ref, tail_ref, abuf_ref, shift_ref, y_ref, between=()):
    rows = zc_ref.shape[0]

    @pl.when(pl.program_id(1) == 0)
    def _():
        abuf_ref[0:CONV_HALO, :] = jnp.zeros((CONV_HALO, CONV_WIDTH), F32)

    z = zc_ref[...]
    a = z[:, 0:CONV_WIDTH] * _sigmoid(z[:, CONV_WIDTH:2 * CONV_WIDTH])
    abuf_ref[CONV_HALO:CONV_HALO + rows, :] = a
    span = shift_ref.shape[1]
    for c in range(1, SUBLANES):
        shift_ref[c - 1] = abuf_ref[c:c + span, :]
    first = CONV_HALO - (CONV_K - 1)
    blk = CONV_ROW_BLOCK
    n_blk = rows // blk
    pending = list(between)
    every = -(-n_blk // (len(pending) + 1)) if pending else n_blk
    for r in range(n_blk):
        if pending and r % every == 0:
            pending.pop(0)()
        acc = jnp.zeros((blk, CONV_WIDTH), F32) + b_ref[...]
        for j in range(CONV_K):
            c = (first + j) % SUBLANES
            lo = first + j - c + r * blk
            src = abuf_ref[lo:lo + blk, :] if c == 0 else shift_ref[c - 1, lo:lo + blk, :]
            acc = acc + jnp.concatenate([w_ref[j]] * (blk // SUBLANES), axis=0) * src
        y_ref[r * blk:(r + 1) * blk, :] = _ln_silu(acc, lg_ref[...], lb_ref[...]).astype(y_ref.dtype)
    for thunk in pending:
        thunk()
    last = a[rows - CONV_HALO:rows]
    tail_ref[...] = last
    abuf_ref[0:CONV_HALO, :] = last


def _conv_sample_body(zc_ref, hist_ref, w_ref, b_ref, lg_ref, lb_ref, y_ref, a_ref):
    z = zc_ref[...]
    a = z[:, 0:CONV_WIDTH] * _sigmoid(z[:, CONV_WIDTH:2 * CONV_WIDTH])
    a_ref[...] = a
    acc = b_ref[...] + w_ref[CONV_K - 1] * a
    for j in range(CONV_K - 1):
        acc = acc + w_ref[j] * hist_ref[j]
    y_ref[...] = _ln_silu(acc, lg_ref[...], lb_ref[...])


def _conv_sample(zc, hist_t, l, conv_w4, conv_b, ln_g, ln_b):
    n = zc.shape[0]
    full = lambda *shape: pl.BlockSpec(shape, lambda i: (0,) * len(shape))
    lsel = lambda i: (l, 0, 0)
    return pl.pallas_call(
        _conv_sample_body,
        grid=(1,),
        in_specs=[
            full(n, 2 * CONV_WIDTH),
            pl.BlockSpec((None, CONV_K - 1, n, CONV_WIDTH), lambda i: (l, 0, 0, 0)),
            pl.BlockSpec((None, CONV_K, 1, CONV_WIDTH), lambda i: (l, 0, 0, 0)),
            pl.BlockSpec((None, 1, CONV_WIDTH), lsel),
            pl.BlockSpec((None, 1, CONV_WIDTH), lsel),
            pl.BlockSpec((None, 1, CONV_WIDTH), lsel),
        ],
        out_specs=[full(n, CONV_WIDTH), full(n, CONV_WIDTH)],
        out_shape=[jax.ShapeDtypeStruct((n, CONV_WIDTH), F32)] * 2,
        compiler_params=_cparams(("arbitrary",), 32),
        name="conv_sample",
    )(zc, hist_t, conv_w4, conv_b, ln_g, ln_b)


def _attn_prompt_body(q_ref, kc_ref, kp_ref, vc_ref, vp_ref, oa_ref, ob_ref, la_ref, lb_ref, *, dil):
    j = pl.program_id(1)
    r = pl.program_id(2)
    blk = ATTN_BLOCK
    nblk = q_ref.shape[0] // blk
    q_all = q_ref[...]
    k_all = jnp.concatenate([kp_ref[...], kc_ref[...]], axis=0)
    v_all = jnp.concatenate([vp_ref[...], vc_ref[...]], axis=0)
    qi = lax.broadcasted_iota(jnp.int32, (blk, 2 * blk), 0)
    ki = lax.broadcasted_iota(jnp.int32, (blk, 2 * blk), 1)
    dist = qi + blk - ki
    band = jnp.where(dist >= 0, jnp.where(dist <= blk, 1, 0), 0)
    first_ok = jnp.where(ki >= blk, 1, jnp.where(j > 0, 1, 0))
    lane = lax.broadcasted_iota(jnp.int32, (1, LANES), 1)
    lo = lane < HEAD_DIM
    for s_blk in range(nblk):
        q = q_all[s_blk * blk:(s_blk + 1) * blk]
        k = k_all[s_blk * blk:(s_blk + 2) * blk]
        v = v_all[s_blk * blk:(s_blk + 2) * blk]
        valid = ((band * first_ok) if s_blk == 0 else band) > 0
        o_parts, lse_parts = [], []
        for hp in range(HEADS_PER_GROUP // 2):
            sl = slice(hp * LANES, (hp + 1) * LANES)
            q2, k2, v2 = q[:, sl], k[:, sl], v[:, sl]
            res = []
            for half in range(2):
                keep = lo if half == 0 else jnp.logical_not(lo)
                qm = jnp.where(keep, q2, jnp.zeros_like(q2))
                s = lax.dot_general(qm, k2, (((1,), (1,)), ((), ())), preferred_element_type=F32)
                s = jnp.where(valid, s, NEG_BIG)
                m = jnp.max(s, axis=-1, keepdims=True)
                p = jnp.exp(s - m)
                den = jnp.sum(p, axis=-1, keepdims=True)
                o = jnp.dot(p.astype(BF16), v2, preferred_element_type=F32)
                res.append((o / den, m + jnp.log(den)))
            o_parts.append(jnp.where(lo, res[0][0], res[1][0]))
            lse_parts.append(jnp.where(lo, res[0][1], res[1][1]))
        for val, ref in zip(o_parts + lse_parts, (oa_ref, ob_ref, la_ref, lb_ref)):
            if dil == 1:
                ref[s_blk * blk:(s_blk + 1) * blk, :] = val
            else:
                ref[pl.ds(r + s_blk * blk * dil, blk, stride=dil), :] = val


def _attn_prompt(q, k, v, dil):
    b_sz, _, m_len, gw = q.shape
    blk = ATTN_BLOCK
    seq = m_len * dil
    nblk = ATTN_STEP_BLOCKS[dil]
    cur = lambda b, j, r: (b, r, j, 0)
    prev = lambda b, j, r: (b, r, jnp.maximum(j * nblk - 1, 0), 0)
    bs = lambda f: pl.BlockSpec((None, None, nblk * blk, gw), f)
    bp = lambda f: pl.BlockSpec((None, None, blk, gw), f)
    span = nblk * blk * dil
    return pl.pallas_call(
        functools.partial(_attn_prompt_body, dil=dil),
        grid=(b_sz, m_len // (nblk * blk), dil),
        in_specs=[bs(cur), bs(cur), bp(prev), bs(cur), bp(prev)],
        out_specs=[pl.BlockSpec((None, span, LANES), lambda b, j, r: (b, j, 0))] * 4,
        out_shape=[jax.ShapeDtypeStruct((b_sz, seq, LANES), F32)] * 4,
        compiler_params=_cparams(("arbitrary", "arbitrary", "arbitrary"), 48),
        name=f"attn_prompt_d{dil}",
    )(q, k, k, v, v)


def _attn_roll_body(*refs):
    ng = len(ATTN_GROUPS)
    q_ref, kn_ref, vn_ref = refs[:3]
    c_refs = refs[3:3 + ng]
    y_ref = refs[-1 - ng]
    o_refs = refs[-ng:]
    scale = HEAD_DIM ** -0.5
    outs = [[None] * ng for _ in range(HEADS_PER_GROUP)]
    lses = [[None] * ng for _ in range(HEADS_PER_GROUP)]

    def columns(x_ref):
        x = jnp.concatenate([x_ref[...], jnp.zeros((LANES - SUBLANES, LANES), F32)], axis=0)
        return x.T

    def head_col(t, head):
        r, half = head // 2, head % 2
        return t[half * HEAD_DIM:(half + 1) * HEAD_DIM, r:r + 1]

    q_t, kn_t, vn_t = columns(q_ref), columns(kn_ref), columns(vn_ref)
    pairs = [(g, h) for g in range(ng) for h in range(HEADS_PER_GROUP)]
    col = lambda t, g, h: head_col(t, g * HEADS_PER_GROUP + h)
    lanes = [lax.broadcasted_iota(jnp.int32, (1, c.shape[-1]), 1) for c in c_refs]
    for g, h in pairs:
        c_ref, o_ref = c_refs[g], o_refs[g]
        buf_len = c_ref.shape[-1]
        last = lanes[g] == buf_len - 1
        o_ref[0, h] = jnp.where(last, col(kn_t, g, h), pltpu.roll(c_ref[0, h], buf_len - 1, 1))
        o_ref[1, h] = jnp.where(last, col(vn_t, g, h), pltpu.roll(c_ref[1, h], buf_len - 1, 1))
    s_all, sn_all = {}, {}
    for g, h in pairs:
        qc = col(q_t, g, h)
        used = (lanes[g] % ATTN_GROUPS[g][1]) == 0
        s_all[g, h] = jnp.where(used, jnp.sum(c_refs[g][0, h] * qc, axis=0, keepdims=True) * scale, NEG_BIG)
        sn_all[g, h] = jnp.sum(col(kn_t, g, h) * qc, axis=0, keepdims=True) * scale
    m_all = {k: jnp.maximum(jnp.max(s_all[k], axis=1, keepdims=True), sn_all[k]) for k in pairs}
    p_all = {k: jnp.exp(s_all[k] - m_all[k]) for k in pairs}
    pn_all = {k: jnp.exp(sn_all[k] - m_all[k]) for k in pairs}
    den_all = {k: jnp.sum(p_all[k], axis=1, keepdims=True) + pn_all[k] for k in pairs}
    for g, h in pairs:
        k = (g, h)
        pv = jnp.sum(c_refs[g][1, h] * p_all[k], axis=1, keepdims=True)
        outs[h][g] = (pv + pn_all[k] * col(vn_t, g, h)) / den_all[k]
        lses[h][g] = m_all[k] + jnp.log(den_all[k])
    ys = []
    for h in range(HEADS_PER_GROUP):
        top = functools.reduce(jnp.maximum, lses[h])
        ws = [jnp.exp(t - top) for t in lses[h]]
        ys.append(sum(w * o for w, o in zip(ws, outs[h])) / sum(ws))
    lane = lax.broadcasted_iota(jnp.int32, (1, LANES), 1)
    square = jnp.zeros((LANES, LANES), F32)
    for r in range(HEADS_PER_GROUP // 2):
        square = jnp.where(lane == r, jnp.concatenate([ys[2 * r], ys[2 * r + 1]], axis=0), square)
    y_ref[...] = square.T[0:SUBLANES]


def _attn_roll(q, kn, vn, caches_t, prev, l):
    n = q.shape[0]
    ng = len(ATTN_GROUPS)
    small = pl.BlockSpec((None, SUBLANES, LANES), lambda b: (b, 0, 0))
    c_specs = []
    for (win, dil), c in zip(ATTN_GROUPS, caches_t):
        assert c.shape[-1] == win and win % dil == 0
        c_specs.append(pl.BlockSpec((None, None) + c.shape[2:], lambda b: (l, b, 0, 0, 0, 0)))
    in_specs = [small, small, small] + c_specs
    args = [q, kn, vn, *caches_t]
    aliases = {}
    if prev is not None:
        in_specs += [pl.BlockSpec(memory_space=pl.ANY)] * ng
        aliases = {len(args) + g: 1 + g for g in range(ng)}
        args += list(prev)
    return pl.pallas_call(
        _attn_roll_body,
        grid=(n,),
        in_specs=in_specs,
        out_specs=[small] + c_specs,
        out_shape=[jax.ShapeDtypeStruct((n, SUBLANES, LANES), F32)]
        + [jax.ShapeDtypeStruct(c.shape, c.dtype) for c in caches_t],
        input_output_aliases=aliases,
        compiler_params=_cparams(("arbitrary",), 48),
        name="attn_roll_sample",
    )(*args)


def _merge_body(*refs, sample):
    if sample:
        (x_ref, mod_ref, g_ref, wg_ref, bg_ref, ys_ref, yc_ref, ya_ref,
         ws_ref, wc_ref, wa_ref, wo_ref, o_ref) = refs
    else:
        (x_ref, mod_ref, g_ref, wg_ref, bg_ref, ys_ref, zc_ref, cw_ref, cb_ref, lg_ref, lb_ref,
         *attn_refs, ws_ref, wc_ref, wa_ref, wo_ref, o_ref, tail_ref,
         abuf_ref, shift_ref, ycs_ref, hm_ref, gate_ref) = refs
    precise = sample
    d = D_MODEL
    x = x_ref[...]
    h = _norm_mod(x, g_ref[...], mod_ref[1], mod_ref[0])
    hm = h if precise else h.astype(BF16)

    def gate_logits(bi):
        return _mm(hm, wg_ref[:, bi * d:(bi + 1) * d], precise) + bg_ref[:, bi * d:(bi + 1) * d]

    if sample:
        ya = ya_ref[...]
        yc = yc_ref[...]
        gates = [gate_logits(bi) for bi in range(3)]
    else:
        hm_ref[...] = hm

        def stage_gate(bi):
            def thunk():
                gate_ref[bi] = jnp.dot(hm_ref[...], wg_ref[:, bi * d:(bi + 1) * d],
                                       preferred_element_type=F32)
            return thunk

        _conv_tile(zc_ref, cw_ref, cb_ref, lg_ref, lb_ref, tail_ref, abuf_ref, shift_ref, ycs_ref,
                   between=[stage_gate(bi) for bi in range(3)])
        yc = ycs_ref[...]
        gates = [gate_ref[bi] + bg_ref[:, bi * d:(bi + 1) * d] for bi in range(3)]
        halves = []
        for hp in range(2):
            o0, l0, o1, l1, o2, l2 = [attn_refs[4 * g + s_ + hp][...] for g in range(3) for s_ in (0, 2)]
            top = jnp.maximum(jnp.maximum(l0, l1), l2)
            e0, e1, e2 = jnp.exp(l0 - top), jnp.exp(l1 - top), jnp.exp(l2 - top)
            halves.append((e0 * o0 + e1 * o1 + e2 * o2) / (e0 + e1 + e2))
        ya = jnp.concatenate(halves, axis=1)
    merged = None
    for bi, (y, w_ref) in enumerate(((ys_ref[...], ws_ref), (yc, wc_ref), (ya, wa_ref))):
        part = _sigmoid(gates[bi]) * _mm(y, w_ref[...], precise)
        merged = part if merged is None else merged + part
    o_ref[...] = x + mod_ref[2] * _mm(merged, wo_ref[...], precise)


def _merge_prompt(x, modp, l, gnorm, w_gate, b_gate, y_ssm, zc, conv_w, conv_b, ln_g, ln_b, attn,
                  w_bs, w_bc, w_ba, w_out):
    b_sz, seq, d = x.shape
    tm = TM_MERGE
    row = lambda w: pl.BlockSpec((None, tm, w), lambda b, i: (b, i, 0))
    lsel = lambda b, i: (l, 0, 0)
    wsp = lambda r, c: pl.BlockSpec((None, r, c), lsel)
    wsp1 = lambda r, c: pl.BlockSpec((None, r, c), lsel, pipeline_mode=pl.Buffered(1))
    in_specs = [
        row(d),
        pl.BlockSpec((None, None, 6, 1, d), lambda b, i: (l, b, 0, 0, 0)),
        wsp(1, d), wsp1(d, 3 * d), wsp(1, 3 * d),
        row(SSM_WIDTH), row(2 * CONV_WIDTH),
        pl.BlockSpec((None, CONV_K, SUBLANES, CONV_WIDTH), lambda b, i: (l, 0, 0, 0)),
        wsp(1, CONV_WIDTH), wsp(1, CONV_WIDTH), wsp(1, CONV_WIDTH),
    ] + [row(LANES)] * 12 + [wsp1(SSM_WIDTH, d), wsp1(CONV_WIDTH, d), wsp1(ATTN_OUT, d), wsp1(d, d)]
    flat = [t for group in attn for t in group]
    return pl.pallas_call(
        functools.partial(_merge_body, sample=False),
        grid=(b_sz, seq // tm),
        in_specs=in_specs,
        out_specs=[row(d), pl.BlockSpec((None, CONV_HALO, CONV_WIDTH), lambda b, i: (b, 0, 0))],
        out_shape=[jax.ShapeDtypeStruct((b_sz, seq, d), F32),
                   jax.ShapeDtypeStruct((b_sz, CONV_HALO, CONV_WIDTH), F32)],
        scratch_shapes=[pltpu.VMEM((tm + CONV_HALO, CONV_WIDTH), F32),
                        pltpu.VMEM((SUBLANES - 1, tm + CONV_HALO - SUBLANES, CONV_WIDTH), F32),
                        pltpu.VMEM((tm, CONV_WIDTH), BF16),
                        pltpu.VMEM((tm, d), BF16),
                        pltpu.VMEM((3, tm, d), F32)],
        compiler_params=_cparams(("arbitrary", "arbitrary"), 56),
        name="merge_prompt",
    )(x, modp, gnorm, w_gate, b_gate, y_ssm, zc, conv_w, conv_b, ln_g, ln_b, *flat, w_bs, w_bc, w_ba, w_out)


def _merge_sample(x, mods, l, gnorm, w_gate, b_gate, y_ssm, y_conv, y_attn, w_bs, w_bc, w_ba, w_out):
    n, d = x.shape
    full = lambda *shape: pl.BlockSpec(shape, lambda i: (0,) * len(shape))
    lsel = lambda i: (l, 0, 0)
    wsp = lambda r, c: pl.BlockSpec((None, r, c), lsel)
    in_specs = [
        full(n, d),
        pl.BlockSpec((None, 6, n, d), lambda i: (l, 0, 0, 0)),
        wsp(1, d), wsp(d, 3 * d), wsp(1, 3 * d),
        full(n, SSM_WIDTH), full(n, CONV_WIDTH), full(n, ATTN_OUT),
        wsp(SSM_WIDTH, d), wsp(CONV_WIDTH, d), wsp(ATTN_OUT, d), wsp(d, d),
    ]
    return pl.pallas_call(
        functools.partial(_merge_body, sample=True),
        grid=(1,),
        in_specs=in_specs,
        out_specs=full(n, d),
        out_shape=jax.ShapeDtypeStruct((n, d), F32),
        compiler_params=_cparams(("arbitrary",), 56),
        name="merge_sample",
    )(x, mods, gnorm, w_gate, b_gate, y_ssm, y_conv, y_attn, w_bs, w_bc, w_ba, w_out)


def _shift_rows(x, hist, k):
    r = pltpu.roll(x, k, 0)
    row = lax.broadcasted_iota(jnp.int32, hist.shape, 0)
    head = jnp.where(row < k, pltpu.roll(hist, k, 0), r[0:SUBLANES])
    return jnp.concatenate([head, r[SUBLANES:]], axis=0)


def _ffn_prompt_body(x_ref, mod_ref, g_ref, wup_ref, cw_ref, cb_ref, wdn_ref, o_ref, tail_ref,
                     hist_ref, hb_ref, up_ref, act_ref):
    i = pl.program_id(1)
    rows = x_ref.shape[0]
    f2 = 2 * FFN_HIDDEN

    @pl.when(i == 0)
    def _():
        hist_ref[...] = jnp.zeros((HIST, f2), F32)

    x = x_ref[...]
    hb_ref[...] = _norm_mod(x, g_ref[...], mod_ref[4], mod_ref[3]).astype(BF16)
    acc = jnp.zeros((rows, D_MODEL), F32)
    cw = FFN_CHUNK
    n_chunks = FFN_HIDDEN // cw

    def project(c):
        for half, base in enumerate((c * cw, FFN_HIDDEN + c * cw)):
            up_ref[c % 2, half] = jnp.dot(hb_ref[...], wup_ref[:, base:base + cw],
                                          preferred_element_type=F32)

    def contract(c, acc):
        return acc + jnp.dot(act_ref[c % 2], wdn_ref[c * cw:(c + 1) * cw, :], preferred_element_type=F32)

    project(0)
    for c in range(n_chunks):
        if c + 1 < n_chunks:
            project(c + 1)
        if c > 0:
            acc = contract(c - 1, acc)
        halves = []
        for half, base in enumerate((c * cw, FFN_HIDDEN + c * cw)):
            cols = slice(base, base + cw)
            up = up_ref[c % 2, half]
            hist = hist_ref[:, cols]
            hist_ref[:, cols] = up[rows - HIST:rows]
            halves.append(cw_ref[0:1, cols] * _shift_rows(up, hist, 2)
                          + cw_ref[1:2, cols] * _shift_rows(up, hist, 1)
                          + cw_ref[2:3, cols] * up + cb_ref[:, cols])
        act_ref[c % 2] = (_gelu_tanh(halves[0]) * halves[1]).astype(BF16)
    acc = contract(n_chunks - 1, acc)
    o_ref[...] = x_ref[...] + mod_ref[5] * acc
    tail_ref[...] = hist_ref[...]


def _ffn_prompt(x, modp, l, gnorm, w_up, conv_w, conv_b, w_down):
    b_sz, seq, d = x.shape
    tm = TM_FFN
    f2 = 2 * FFN_HIDDEN
    lsel = lambda b, i: (l, 0, 0)
    once = pl.Buffered(1)
    return pl.pallas_call(
        _ffn_prompt_body,
        grid=(b_sz, seq // tm),
        in_specs=[
            pl.BlockSpec((None, tm, d), lambda b, i: (b, i, 0)),
            pl.BlockSpec((None, None, 6, 1, d), lambda b, i: (l, b, 0, 0, 0)),
            pl.BlockSpec((None, 1, d), lsel),
            pl.BlockSpec((None, d, f2), lsel, pipeline_mode=once),
            pl.BlockSpec((None, FFN_CONV_K, f2), lsel),
            pl.BlockSpec((None, 1, f2), lsel),
            pl.BlockSpec((None, FFN_HIDDEN, d), lsel, pipeline_mode=once),
        ],
        out_specs=[
            pl.BlockSpec((None, tm, d), lambda b, i: (b, i, 0)),
            pl.BlockSpec((None, HIST, f2), lambda b, i: (b, 0, 0)),
        ],
        out_shape=[jax.ShapeDtypeStruct((b_sz, seq, d), F32),
                   jax.ShapeDtypeStruct((b_sz, HIST, f2), F32)],
        scratch_shapes=[pltpu.VMEM((HIST, f2), F32),
                        pltpu.VMEM((tm, d), BF16),
                        pltpu.VMEM((2, 2, tm, FFN_CHUNK), F32),
                        pltpu.VMEM((2, tm, FFN_CHUNK), BF16)],
        compiler_params=_cparams(("arbitrary", "arbitrary"), 56),
        name="ffn_prompt",
    )(x, modp, gnorm, w_up, conv_w, conv_b, w_down)


def _ffn_sample_body(x_ref, mod_ref, g_ref, wup_ref, c0_ref, c1_ref, cw_ref, cb_ref, wdn_ref, o_ref, up_ref):
    x = x_ref[...]
    h2 = _norm_mod(x, g_ref[...], mod_ref[4], mod_ref[3])
    up = _mm(h2, wup_ref[...], True)
    up_ref[...] = up
    cv = cw_ref[0:1, :] * c0_ref[...] + cw_ref[1:2, :] * c1_ref[...] + cw_ref[2:3, :] * up + cb_ref[...]
    act = _gelu_tanh(cv[:, 0:FFN_HIDDEN]) * cv[:, FFN_HIDDEN:2 * FFN_HIDDEN]
    o_ref[...] = x + mod_ref[5] * _mm(act, wdn_ref[...], True)


def _ffn_sample(x, mods, l, gnorm, w_up, c0, c1, conv_w, conv_b, w_down):
    n, d = x.shape
    f2 = 2 * FFN_HIDDEN
    full = lambda *shape: pl.BlockSpec(shape, lambda i: (0,) * len(shape))
    lsel = lambda i: (l, 0, 0)
    once = pl.Buffered(1)
    return pl.pallas_call(
        _ffn_sample_body,
        grid=(1,),
        in_specs=[
            full(n, d),
            pl.BlockSpec((None, 6, n, d), lambda i: (l, 0, 0, 0)),
            pl.BlockSpec((None, 1, d), lsel),
            pl.BlockSpec((None, d, f2), lsel, pipeline_mode=once),
            pl.BlockSpec((None, n, f2), lsel), pl.BlockSpec((None, n, f2), lsel),
            pl.BlockSpec((None, FFN_CONV_K, f2), lsel),
            pl.BlockSpec((None, 1, f2), lsel),
            pl.BlockSpec((None, FFN_HIDDEN, d), lsel, pipeline_mode=once),
        ],
        out_specs=[full(n, d), full(n, f2)],
        out_shape=[jax.ShapeDtypeStruct((n, d), F32), jax.ShapeDtypeStruct((n, f2), F32)],
        compiler_params=_cparams(("arbitrary",), 56),
        name="ffn_sample",
    )(x, mods, gnorm, w_up, c0, c1, conv_w, conv_b, w_down)


def _rope_tables(pos):
    half = HEAD_DIM // 2
    inv = ROPE_THETA ** (-jnp.arange(half, dtype=F32) / half)
    ang = pos.astype(F32)[:, None] * inv[None, :]
    cos, sin = jnp.cos(ang), jnp.sin(ang)
    cos_h = jnp.concatenate([cos, cos], axis=1)
    sin_h = jnp.concatenate([-sin, sin], axis=1)
    reps = LANES // HEAD_DIM
    return jnp.tile(cos_h, (1, reps)), jnp.tile(sin_h, (1, reps))


def _residue_perm(rows, dil):
    i = jnp.arange(rows)
    src = (i % (rows // dil)) * dil + i // (rows // dil)
    return (src[:, None] == jnp.arange(rows)[None, :]).astype(BF16)


def _block_ones(dtype):
    i = jnp.arange(MXU_DIM) // HEAD_DIM
    return (i[:, None] == i[None, :]).astype(dtype)


def kernel(x_prompt, x_sample, state_ssm, cache_conv, cache_kv_w128, cache_kv_w512, cache_kv_w2048, cache_ffn, c_prompt, c_sample, w_ada, b_ada, g_norm_mix, w_in, ssm_a_re, ssm_a_im, ssm_log_dt, ssm_b_re, ssm_b_im, ssm_c_re, ssm_c_im, ssm_d, ssm_w_glu, ssm_b_glu, conv_w, conv_b, conv_ln_g, conv_ln_b, attn_gq, attn_gk, w_gate, b_gate, w_br_ssm, w_br_conv, w_br_attn, w_out, g_norm_ffn, ffn_w_up, ffn_conv_w, ffn_conv_b, ffn_w_down):
    bp, seq, d = x_prompt.shape
    ns = x_sample.shape[0]
    depth = w_ada.shape[0]
    f2 = 2 * FFN_HIDDEN
    assert x_sample.shape[1] == 1
    assert all(seq % (ATTN_BLOCK * dil * ATTN_STEP_BLOCKS[dil]) == 0 for _, dil in ATTN_GROUPS)
    caches = (cache_kv_w128, cache_kv_w512, cache_kv_w2048)

    pad = (-(bp + ns)) % SUBLANES
    c_all = jnp.concatenate([c_prompt, c_sample, jnp.zeros((pad, d), F32)], axis=0)
    mod = _ada_call(c_all, w_ada, b_ada)
    modp = mod[:, :bp].reshape(depth, bp, 6, 1, d)
    mods = mod[:, bp:bp + ns].reshape(depth, ns, 6, d).transpose(0, 2, 1, 3)

    bf = lambda w: w.astype(BF16)
    w_in_b, w_gate_b, w_out_b = bf(w_in), bf(w_gate), bf(w_out)
    w_bs_b, w_bc_b, w_ba_b = bf(w_br_ssm), bf(w_br_conv), bf(w_br_attn)
    w_up_b, w_dn_b, w_glu_b = bf(ffn_w_up), bf(ffn_w_down), bf(ssm_w_glu)

    row3 = lambda t: t.reshape(depth, 1, t.shape[-1])
    g_mix, g_ffn = row3(g_norm_mix), row3(g_norm_ffn)
    gq = row3(jnp.tile(attn_gq, (1, ATTN_HEADS)))
    gk = row3(jnp.tile(attn_gk, (1, ATTN_HEADS)))
    b_gate3, ssm_d3, b_glu3 = row3(b_gate), row3(ssm_d), row3(ssm_b_glu)
    conv_b3, ln_g3, ln_b3, ffn_cb3 = row3(conv_b), row3(conv_ln_g), row3(conv_ln_b), row3(ffn_conv_b)
    conv_w4 = conv_w.reshape(depth, CONV_K, 1, CONV_WIDTH)
    conv_w8 = jnp.repeat(conv_w4, SUBLANES, axis=2)

    cos_p, sin_p = _rope_tables(jnp.arange(seq, dtype=jnp.int32))
    cos_s, sin_s = _rope_tables(jnp.full((ns,), PAST_LEN, dtype=jnp.int32))
    ones_b, ones_f = _block_ones(BF16), _block_ones(F32)
    p4 = _residue_perm(TM_INPROJ, ATTN_GROUPS[1][1])
    p16 = _residue_perm(TM_INPROJ, ATTN_GROUPS[2][1])
    seg = L_SSM // SUBLANES
    pm = _residue_perm(L_SSM, seg)
    pmt = pm.T
    zero_state = jnp.zeros((bp, 1, STATE_W), F32)

    tabs = _ssm_tables(ssm_a_re, ssm_a_im, ssm_log_dt, ssm_b_re, ssm_b_im, ssm_c_re, ssm_c_im, seg)
    state_slab = _state_to_slab(state_ssm.reshape((depth * ns,) + state_ssm.shape[2:])).reshape(depth, ns, STATE_W)
    conv_hist = cache_conv.transpose(0, 2, 1, 3)
    ffn_old, ffn_new = cache_ffn[:, :, 0], cache_ffn[:, :, 1]
    caches_t = [c.transpose(0, 1, 3, 4, 5, 2) for c in caches]

    yp = x_prompt
    ys = x_sample.reshape(ns, d)
    st_p, st_s, a_tails, a_news, up_tails, up_news, k_tails, v_tails = [], [], [], [], [], [], [], []
    rolled = None
    for l in range(depth):
        (zs, zc, k_tail, v_tail, q0, k0, v0, q1, k1, v1, q2, k2, v2) = _inproj_prompt(
            yp, modp, l, g_mix, w_in_b, gq, gk, cos_p, sin_p, ones_b, p4, p16)
        y_ssm, st = _ssm_prompt(zs, zero_state, l, pm, pmt, tabs, ssm_d3, w_glu_b, b_glu3)
        attn = [_attn_prompt(q0, k0, v0, 1), _attn_prompt(q1, k1, v1, ATTN_GROUPS[1][1]),
                _attn_prompt(q2, k2, v2, ATTN_GROUPS[2][1])]
        x_mid, a_tail = _merge_prompt(yp, modp, l, g_mix, w_gate_b, b_gate3, y_ssm, zc, conv_w8, conv_b3,
                                      ln_g3, ln_b3, attn, w_bs_b, w_bc_b, w_ba_b, w_out_b)
        yp, up_tail = _ffn_prompt(x_mid, modp, l, g_ffn, w_up_b, ffn_conv_w, ffn_cb3, w_dn_b)
        st_p.append(st)
        a_tails.append(a_tail)
        up_tails.append(up_tail)
        k_tails.append(k_tail)
        v_tails.append(v_tail)

        zs_s, zc_s, q_s, k_s, v_s = _inproj_sample(ys, mods, l, g_mix, w_in, gq, gk, cos_s, sin_s, ones_f)
        y_ssm_s, st_l = _ssm_sample(zs_s, state_slab, l, tabs, ssm_d3, ssm_w_glu, b_glu3)
        y_conv_s, a_s = _conv_sample(zc_s, conv_hist, l, conv_w4, conv_b3, ln_g3, ln_b3)
        rows8 = lambda t: jnp.pad(t.reshape(ns, ATTN_WIDTH // LANES, LANES),
                                  ((0, 0), (0, SUBLANES - ATTN_WIDTH // LANES), (0, 0)))
        y_attn_s, *rolled = _attn_roll(rows8(q_s), rows8(k_s), rows8(v_s), caches_t, rolled, l)
        y_attn_s = y_attn_s[:, :ATTN_OUT // LANES].reshape(ns, ATTN_OUT)
        x_mid_s = _merge_sample(ys, mods, l, g_mix, w_gate, b_gate3, y_ssm_s, y_conv_s, y_attn_s,
                                w_br_ssm, w_br_conv, w_br_attn, w_out)
        ys, up_s = _ffn_sample(x_mid_s, mods, l, g_ffn, ffn_w_up, ffn_old, ffn_new, ffn_conv_w, ffn_cb3,
                               ffn_w_down)
        st_s.append(st_l)
        a_news.append(a_s)
        up_news.append(up_s)

    stk = lambda t: jnp.stack(t, axis=0)
    unslab = lambda t, n: _slab_to_state(t.reshape(depth * n, STATE_W)).reshape(depth, n, SSM_GROUPS, SSM_N, 2)
    ssm_p = unslab(stk(st_p), bp)
    ssm_s = unslab(stk(st_s), ns)
    conv_p = stk(a_tails)[:, :, CONV_HALO - (CONV_K - 1):]
    conv_s = jnp.concatenate([cache_conv[:, :, 1:], stk(a_news)[:, :, None, :]], axis=2)
    ffn_p = stk(up_tails)[:, :, HIST - (FFN_CONV_K - 1):]
    ffn_s = jnp.stack([ffn_new, stk(up_news)], axis=2)
    kt, vt = stk(k_tails), stk(v_tails)
    gw = HEADS_PER_GROUP * HEAD_DIM
    kv_p = []
    for gi, (win, _) in enumerate(ATTN_GROUPS):
        keep = min(win, seq)
        pick = lambda t: t[:, :, gi * gw:(gi + 1) * gw, t.shape[-1] - keep:].reshape(
            depth, bp, HEADS_PER_GROUP, HEAD_DIM, keep)
        kv_p.append(jnp.stack([pick(kt), pick(vt)], axis=2).transpose(0, 1, 5, 2, 3, 4))
    kv_s = [t.transpose(0, 1, 5, 2, 3, 4) for t in rolled]
    return (yp, ys.reshape(ns, 1, d), ssm_p, ssm_s, conv_p, conv_s,
            kv_p[0], kv_s[0], kv_p[1], kv_s[1], kv_p[2], kv_s[2], ffn_p, ffn_s)
```

```python
import functools
import math

import jax
import jax.numpy as jnp
from jax import lax
from jax.experimental import pallas as pl
from jax.experimental.pallas import tpu as pltpu

F32 = jnp.float32
BF16 = jnp.bfloat16

D_MODEL = 1024
SSM_WIDTH = 512
SSM_GROUP = 16
SSM_GROUPS = 32
SSM_N = 64
CONV_WIDTH = 512
CONV_K = 31
HEAD_DIM = 64
HEADS_PER_GROUP = 4
ATTN_GROUPS = ((128, 1), (512, 4), (2048, 16))
ATTN_HEADS = 12
ATTN_WIDTH = 768
ATTN_OUT = 256
ATTN_BLOCK = 128
ROPE_THETA = 10000.0
FFN_HIDDEN = 2816
FFN_CONV_K = 3
IN_WIDTH = SSM_WIDTH + 2 * CONV_WIDTH + 3 * ATTN_WIDTH
EPS = 1e-6
PAST_LEN = 8192

SUBLANES = 8
LANES = 128
MXU_DIM = 256
SLAB_GROUPS = 8
N_SLABS = SSM_GROUPS // SLAB_GROUPS
SLAB_CH = SLAB_GROUPS * SSM_GROUP
SLAB_ST = SLAB_GROUPS * SSM_N
STATE_W = 2 * SSM_GROUPS * SSM_N
NEG_BIG = -1e30
GELU_C = math.sqrt(2.0 / math.pi)

TM_INPROJ = 512
L_SSM = 512
TM_MERGE = 512
TM_FFN = 256
FFN_CHUNK = 256
HIST = 8
CONV_HALO = 32
CONV_ROW_BLOCK = 32
KV_TAIL = 2048
ATTN_STEP_BLOCKS = {1: 8, 4: 8, 16: 4}


def _cparams(sem, vmem_mb):
    return pltpu.CompilerParams(dimension_semantics=sem, vmem_limit_bytes=vmem_mb * 1024 * 1024)


def _sigmoid(x):
    return 1.0 / (1.0 + jnp.exp(-x))


def _gelu_tanh(x):
    neg_2u = x * ((-2.0 * GELU_C * 0.044715) * (x * x) - 2.0 * GELU_C)
    return x / (1.0 + jnp.exp(neg_2u))


def _mm(a, w, precise):
    if precise:
        return jnp.dot(a, w, precision=lax.Precision.HIGHEST, preferred_element_type=F32)
    return jnp.dot(a.astype(BF16), w, preferred_element_type=F32)


def _norm_mod(x, g, scale, shift):
    ms = jnp.mean(x * x, axis=-1, keepdims=True)
    return (x * lax.rsqrt(ms + EPS) * g) * (1.0 + scale) + shift


def _ada_body(c_ref, w_ref, b_ref, o_ref):
    c = c_ref[...]
    s = c * _sigmoid(c)
    o_ref[...] = jnp.dot(s, w_ref[...], precision=lax.Precision.HIGHEST,
                         preferred_element_type=F32) + b_ref[...]


def _ada_call(c_all, w_ada, b_ada):
    depth, d, n6 = w_ada.shape
    rows = c_all.shape[0]
    tn = 1536
    return pl.pallas_call(
        _ada_body,
        grid=(depth, n6 // tn),
        in_specs=[
            pl.BlockSpec((rows, d), lambda l, j: (0, 0)),
            pl.BlockSpec((None, d, tn), lambda l, j: (l, 0, j)),
            pl.BlockSpec((None, 1, tn), lambda l, j: (l, 0, j)),
        ],
        out_specs=pl.BlockSpec((None, rows, tn), lambda l, j: (l, 0, j)),
        out_shape=jax.ShapeDtypeStruct((depth, rows, n6), F32),
        compiler_params=_cparams(("arbitrary", "arbitrary"), 40),
        name="ada_mod",
    )(c_all, w_ada, b_ada.reshape(depth, 1, n6))


def _head_norm_rope(z, gvec, ones, cos, sins, precise):
    sq = z * z
    parts = []
    for c in range(ATTN_WIDTH // MXU_DIM):
        parts.append(_mm(sq[:, c * MXU_DIM:(c + 1) * MXU_DIM], ones, precise))
    ms = jnp.concatenate(parts, axis=1) * (1.0 / HEAD_DIM)
    y = z * lax.rsqrt(ms + EPS) * gvec
    lane = lax.broadcasted_iota(jnp.int32, (1, LANES), 1)
    first = (lane % HEAD_DIM) < (HEAD_DIM // 2)
    outs = []
    for c in range(ATTN_WIDTH // LANES):
        yc = y[:, c * LANES:(c + 1) * LANES]
        partner = jnp.where(first, pltpu.roll(yc, LANES - HEAD_DIM // 2, 1),
                            pltpu.roll(yc, HEAD_DIM // 2, 1))
        outs.append(yc * cos + partner * sins)
    return jnp.concatenate(outs, axis=1)


def _inproj_body(*refs, sample, first_tail=0):
    if sample:
        (x_ref, mod_ref, g_ref, w_ref, gq_ref, gk_ref, cos_ref, sin_ref, ones_ref,
         zs_ref, zc_ref, q_ref, k_ref, v_ref) = refs
    else:
        (x_ref, mod_ref, g_ref, w_ref, gq_ref, gk_ref, cos_ref, sin_ref, ones_ref, p4_ref, p16_ref,
         zs_ref, zc_ref, kt_ref, vt_ref,
         q0_ref, k0_ref, v0_ref, q1_ref, k1_ref, v1_ref, q2_ref, k2_ref, v2_ref, kf_ref, vf_ref) = refs
    precise = sample
    x = x_ref[...]
    h = _norm_mod(x, g_ref[...], mod_ref[1], mod_ref[0])
    hm = h if precise else h.astype(BF16)
    c1 = SSM_WIDTH
    c2 = c1 + 2 * CONV_WIDTH
    c3 = c2 + ATTN_WIDTH
    c4 = c3 + ATTN_WIDTH
    zs_ref[...] = _mm(hm, w_ref[:, 0:c1], precise)
    zc_ref[...] = _mm(hm, w_ref[:, c1:c2], precise)
    zq = _mm(hm, w_ref[:, c2:c3], precise)
    zk = _mm(hm, w_ref[:, c3:c4], precise)
    zv = _mm(hm, w_ref[:, c4:IN_WIDTH], precise)
    cos = cos_ref[...]
    sins = sin_ref[...]
    ones = ones_ref[...]
    q = _head_norm_rope(zq, gq_ref[...], ones, cos, sins, precise)
    k = _head_norm_rope(zk, gk_ref[...], ones, cos, sins, precise)
    if sample:
        q_ref[...] = q
        k_ref[...] = k
        v_ref[...] = zv
        return
    kf_ref[...] = k
    vf_ref[...] = zv
    qb = (q * (HEAD_DIM ** -0.5)).astype(BF16)
    kb = k.astype(BF16)
    vb = zv.astype(BF16)
    gw = HEADS_PER_GROUP * HEAD_DIM
    q0_ref[0] = qb[:, 0:gw]
    k0_ref[0] = kb[:, 0:gw]
    v0_ref[0] = vb[:, 0:gw]
    for gi, (p_ref, outs) in enumerate(((p4_ref, (q1_ref, k1_ref, v1_ref)),
                                        (p16_ref, (q2_ref, k2_ref, v2_ref))), start=1):
        pm = p_ref[...]
        for src, o_ref in zip((qb, kb, vb), outs):
            dil, rows = o_ref.shape[0], o_ref.shape[1]
            perm = jnp.dot(pm, src[:, gi * gw:(gi + 1) * gw], preferred_element_type=F32).astype(BF16)
            for r in range(dil):
                o_ref[r] = perm[r * rows:(r + 1) * rows]

    @pl.when(pl.program_id(1) >= first_tail)
    def _():
        kt_ref[...] = kf_ref[...].T
        vt_ref[...] = vf_ref[...].T


def _inproj_prompt(x, modp, l, gnorm, w_in, gq, gk, cos, sins, ones, p4, p16):
    b_sz, seq, d = x.shape
    tm = TM_INPROJ
    nt = seq // tm
    tail = min(KV_TAIL, seq)
    ft = nt - tail // tm
    gw = HEADS_PER_GROUP * HEAD_DIM
    tmap = lambda b, i: (b, 0, jnp.maximum(i - ft, 0))
    in_specs = [
        pl.BlockSpec((None, tm, d), lambda b, i: (b, i, 0)),
        pl.BlockSpec((None, None, 6, 1, d), lambda b, i: (l, b, 0, 0, 0)),
        pl.BlockSpec((None, 1, d), lambda b, i: (l, 0, 0)),
        pl.BlockSpec((None, d, IN_WIDTH), lambda b, i: (l, 0, 0), pipeline_mode=pl.Buffered(1)),
        pl.BlockSpec((None, 1, ATTN_WIDTH), lambda b, i: (l, 0, 0)),
        pl.BlockSpec((None, 1, ATTN_WIDTH), lambda b, i: (l, 0, 0)),
        pl.BlockSpec((tm, LANES), lambda b, i: (i, 0)),
        pl.BlockSpec((tm, LANES), lambda b, i: (i, 0)),
        pl.BlockSpec((MXU_DIM, MXU_DIM), lambda b, i: (0, 0)),
        pl.BlockSpec((tm, tm), lambda b, i: (0, 0)),
        pl.BlockSpec((tm, tm), lambda b, i: (0, 0)),
    ]
    out_shape = [
        jax.ShapeDtypeStruct((b_sz, seq, SSM_WIDTH), F32),
        jax.ShapeDtypeStruct((b_sz, seq, 2 * CONV_WIDTH), F32),
        jax.ShapeDtypeStruct((b_sz, ATTN_WIDTH, tail), F32),
        jax.ShapeDtypeStruct((b_sz, ATTN_WIDTH, tail), F32),
    ]
    out_specs = [
        pl.BlockSpec((None, tm, SSM_WIDTH), lambda b, i: (b, i, 0)),
        pl.BlockSpec((None, tm, 2 * CONV_WIDTH), lambda b, i: (b, i, 0)),
        pl.BlockSpec((None, ATTN_WIDTH, tm), tmap),
        pl.BlockSpec((None, ATTN_WIDTH, tm), tmap),
    ]
    for _, dil in ATTN_GROUPS:
        for _ in range(3):
            out_shape.append(jax.ShapeDtypeStruct((b_sz, dil, seq // dil, gw), BF16))
            out_specs.append(pl.BlockSpec((None, dil, tm // dil, gw), lambda b, i: (b, 0, i, 0)))
    return pl.pallas_call(
        functools.partial(_inproj_body, sample=False, first_tail=ft),
        grid=(b_sz, nt),
        in_specs=in_specs,
        out_specs=out_specs,
        out_shape=out_shape,
        scratch_shapes=[pltpu.VMEM((tm, ATTN_WIDTH), F32), pltpu.VMEM((tm, ATTN_WIDTH), F32)],
        compiler_params=_cparams(("arbitrary", "arbitrary"), 56),
        name="inproj_prompt",
    )(x, modp, gnorm, w_in, gq, gk, cos, sins, ones, p4, p16)


def _inproj_sample(x, mods, l, gnorm, w_in, gq, gk, cos, sins, ones):
    n, d = x.shape
    lsel = lambda i: (l, 0, 0)
    in_specs = [
        pl.BlockSpec((n, d), lambda i: (0, 0)),
        pl.BlockSpec((None, 6, n, d), lambda i: (l, 0, 0, 0)),
        pl.BlockSpec((None, 1, d), lsel),
        pl.BlockSpec((None, d, IN_WIDTH), lsel),
        pl.BlockSpec((None, 1, ATTN_WIDTH), lsel),
        pl.BlockSpec((None, 1, ATTN_WIDTH), lsel),
        pl.BlockSpec((n, LANES), lambda i: (0, 0)),
        pl.BlockSpec((n, LANES), lambda i: (0, 0)),
        pl.BlockSpec((MXU_DIM, MXU_DIM), lambda i: (0, 0)),
    ]
    widths = (SSM_WIDTH, 2 * CONV_WIDTH, ATTN_WIDTH, ATTN_WIDTH, ATTN_WIDTH)
    return pl.pallas_call(
        functools.partial(_inproj_body, sample=True),
        grid=(1,),
        in_specs=in_specs,
        out_specs=[pl.BlockSpec((n, w), lambda i: (0, 0)) for w in widths],
        out_shape=[jax.ShapeDtypeStruct((n, w), F32) for w in widths],
        compiler_params=_cparams(("arbitrary",), 48),
        name="inproj_sample",
    )(x, mods, gnorm, w_in, gq, gk, cos, sins, ones)


def _ssm_tail(y_raw, u, d_ref, wglu_ref, bglu_ref, precise):
    y = _gelu_tanh(y_raw + d_ref[...] * u)
    return y * _sigmoid(_mm(y, wglu_ref[...], precise) + bglu_ref[...])


def _ssm_prompt_body(zs_ref, s0_ref, pm_ref, pmt_ref, b_ref, c_ref, lam_ref, lamp_ref, pw_ref,
                     d_ref, wglu_ref, bglu_ref, y_ref, st_ref,
                     v_ref, sb_ref, cs_ref, carry_ref):
    ci = pl.program_id(1)
    chunk = zs_ref.shape[0]
    seg = chunk // SUBLANES

    @pl.when(ci == 0)
    def _():
        carry_ref[...] = s0_ref[...]

    u = zs_ref[...]
    up = jnp.dot(pm_ref[...], u.astype(BF16), preferred_element_type=F32).astype(BF16)
    re, im = slice(0, SLAB_ST), slice(SLAB_ST, 2 * SLAB_ST)
    for m in range(N_SLABS):
        v_ref[m] = jnp.dot(up[:, m * SLAB_CH:(m + 1) * SLAB_CH], b_ref[m], preferred_element_type=F32)
    y_parts = [None] * N_SLABS
    for pair in range(0, N_SLABS, 2):
        slabs = (pair, pair + 1)
        zero = jnp.zeros((SUBLANES, SLAB_ST), F32)
        state = {m: (zero, zero) for m in slabs}
        for i in range(seg):
            rows = slice(i * SUBLANES, (i + 1) * SUBLANES)
            for m in slabs:
                lr = lam_ref[m, 0]
                li = lam_ref[m, 1]
                sr, si = state[m]
                nsr = lr * sr - li * si + v_ref[m, rows, re]
                nsi = lr * si + li * sr + v_ref[m, rows, im]
                v_ref[m, rows, re] = nsr
                v_ref[m, rows, im] = nsi
                state[m] = (nsr, nsi)
        for m in slabs:
            base = m * 2 * SLAB_ST
            sr, si = state[m]
            er = carry_ref[0:1, base:base + SLAB_ST]
            ei = carry_ref[0:1, base + SLAB_ST:base + 2 * SLAB_ST]
            pr = lamp_ref[m, 0:1, :]
            pi = lamp_ref[m, 1:2, :]
            for r in range(SUBLANES):
                cs_ref[m, r:r + 1, re] = er
                cs_ref[m, r:r + 1, im] = ei
                ner = sr[r:r + 1] + pr * er - pi * ei
                nei = si[r:r + 1] + pr * ei + pi * er
                er, ei = ner, nei
            carry_ref[0:1, base:base + SLAB_ST] = er
            carry_ref[0:1, base + SLAB_ST:base + 2 * SLAB_ST] = ei
        for m in slabs:
            csr = cs_ref[m, :, re]
            csi = cs_ref[m, :, im]
            for i2 in range(seg // 2):
                rows16 = slice(i2 * 2 * SUBLANES, (i2 + 1) * 2 * SUBLANES)
                rows_r, rows_i = [], []
                for h in range(2):
                    i = i2 * 2 + h
                    rows = slice(i * SUBLANES, (i + 1) * SUBLANES)
                    qr = pw_ref[m, 0, rows, :]
                    qi = pw_ref[m, 1, rows, :]
                    rows_r.append(v_ref[m, rows, re] + qr * csr - qi * csi)
                    rows_i.append(v_ref[m, rows, im] + qr * csi + qi * csr)
                sb_ref[m, rows16, re] = jnp.concatenate(rows_r, axis=0).astype(BF16)
                sb_ref[m, rows16, im] = jnp.concatenate(rows_i, axis=0).astype(BF16)
            y_parts[m] = jnp.dot(sb_ref[m], c_ref[m], preferred_element_type=F32)
    y_perm = jnp.concatenate(y_parts, axis=1)
    hi = y_perm.astype(BF16)
    lo = (y_perm - hi.astype(F32)).astype(BF16)
    pmt = pmt_ref[...]
    y_nat = jnp.dot(pmt, hi, preferred_element_type=F32) + jnp.dot(pmt, lo, preferred_element_type=F32)
    y_ref[...] = _ssm_tail(y_nat, u, d_ref, wglu_ref, bglu_ref, False).astype(BF16)
    st_ref[...] = carry_ref[...]


def _ssm_prompt(zs, s0, l, pm, pmt, tabs, ssm_d, w_glu, b_glu):
    b_sz, seq, _ = zs.shape
    chunk = L_SSM
    full = lambda *shape: pl.BlockSpec(shape, lambda b, c: (0,) * len(shape))
    tab = lambda *shape: pl.BlockSpec((None,) + shape, lambda b, c: (l,) + (0,) * len(shape))
    lsel = lambda b, c: (l, 0, 0)
    return pl.pallas_call(
        _ssm_prompt_body,
        grid=(b_sz, seq // chunk),
        in_specs=[
            pl.BlockSpec((None, chunk, SSM_WIDTH), lambda b, c: (b, c, 0)),
            pl.BlockSpec((None, 1, STATE_W), lambda b, c: (b, 0, 0)),
            full(chunk, chunk), full(chunk, chunk),
            tab(N_SLABS, SLAB_CH, 2 * SLAB_ST),
            tab(N_SLABS, 2 * SLAB_ST, SLAB_CH),
            tab(N_SLABS, 2, SUBLANES, SLAB_ST), tab(N_SLABS, 2, SLAB_ST),
            tab(N_SLABS, 2, chunk, SLAB_ST),
            pl.BlockSpec((None, 1, SSM_WIDTH), lsel),
            pl.BlockSpec((None, SSM_WIDTH, SSM_WIDTH), lsel),
            pl.BlockSpec((None, 1, SSM_WIDTH), lsel),
        ],
        out_specs=[
            pl.BlockSpec((None, chunk, SSM_WIDTH), lambda b, c: (b, c, 0)),
            pl.BlockSpec((None, 1, STATE_W), lambda b, c: (b, 0, 0)),
        ],
        out_shape=[jax.ShapeDtypeStruct((b_sz, seq, SSM_WIDTH), BF16),
                   jax.ShapeDtypeStruct((b_sz, 1, STATE_W), F32)],
        scratch_shapes=[
            pltpu.VMEM((N_SLABS, chunk, 2 * SLAB_ST), F32),
            pltpu.VMEM((N_SLABS, chunk, 2 * SLAB_ST), BF16),
            pltpu.VMEM((N_SLABS, SUBLANES, 2 * SLAB_ST), F32),
            pltpu.VMEM((1, STATE_W), F32),
        ],
        compiler_params=_cparams(("arbitrary", "arbitrary"), 56),
        name="ssm_prompt",
    )(zs, s0, pm, pmt, tabs["b_bf"], tabs["c_bf"], tabs["lam_rows"], tabs["lam_seg"], tabs["pw_rows"],
      ssm_d, w_glu, b_glu)


def _ssm_sample_body(u_ref, s0_ref, b_ref, c_ref, lam_ref, d_ref, wglu_ref, bglu_ref, y_ref, st_ref):
    u = u_ref[...]
    y_parts = []
    for m in range(N_SLABS):
        base = m * 2 * SLAB_ST
        v = _mm(u[:, m * SLAB_CH:(m + 1) * SLAB_CH], b_ref[m], True)
        lr = lam_ref[m, 0:1, :]
        li = lam_ref[m, 1:2, :]
        sr0 = s0_ref[:, base:base + SLAB_ST]
        si0 = s0_ref[:, base + SLAB_ST:base + 2 * SLAB_ST]
        sr = lr * sr0 - li * si0 + v[:, 0:SLAB_ST]
        si = lr * si0 + li * sr0 + v[:, SLAB_ST:2 * SLAB_ST]
        st_ref[:, base:base + SLAB_ST] = sr
        st_ref[:, base + SLAB_ST:base + 2 * SLAB_ST] = si
        y_parts.append(_mm(jnp.concatenate([sr, si], axis=1), c_ref[m], True))
    y_raw = jnp.concatenate(y_parts, axis=1)
    y_ref[...] = _ssm_tail(y_raw, u, d_ref, wglu_ref, bglu_ref, True)


def _ssm_sample(zs, s0, l, tabs, ssm_d, w_glu, b_glu):
    n = zs.shape[0]
    full = lambda *shape: pl.BlockSpec(shape, lambda i: (0,) * len(shape))
    tab = lambda *shape: pl.BlockSpec((None,) + shape, lambda i: (l,) + (0,) * len(shape))
    lsel = lambda i: (l, 0, 0)
    return pl.pallas_call(
        _ssm_sample_body,
        grid=(1,),
        in_specs=[
            full(n, SSM_WIDTH), tab(n, STATE_W),
            tab(N_SLABS, SLAB_CH, 2 * SLAB_ST), tab(N_SLABS, 2 * SLAB_ST, SLAB_CH),
            tab(N_SLABS, 2, SLAB_ST),
            pl.BlockSpec((None, 1, SSM_WIDTH), lsel),
            pl.BlockSpec((None, SSM_WIDTH, SSM_WIDTH), lsel),
            pl.BlockSpec((None, 1, SSM_WIDTH), lsel),
        ],
        out_specs=[full(n, SSM_WIDTH), full(n, STATE_W)],
        out_shape=[jax.ShapeDtypeStruct((n, SSM_WIDTH), F32), jax.ShapeDtypeStruct((n, STATE_W), F32)],
        compiler_params=_cparams(("arbitrary",), 32),
        name="ssm_sample",
    )(zs, s0, tabs["b_f32"], tabs["c_f32"], tabs["lam"], ssm_d, w_glu, b_glu)


def _ssm_tables(a_re, a_im, log_dt, b_re, b_im, c_re, c_im, seg):
    depth = a_re.shape[0]
    dt = jnp.exp(log_dt)[..., None]
    xr, xi = a_re * dt, a_im * dt
    mag = jnp.exp(xr)
    lr, li = mag * jnp.cos(xi), mag * jnp.sin(xi)
    den = a_re * a_re + a_im * a_im
    nr, ni = lr - 1.0, li
    cr = (nr * a_re + ni * a_im) / den
    cim = (ni * a_re - nr * a_im) / den
    bbr = cr[..., None] * b_re - cim[..., None] * b_im
    bbi = cr[..., None] * b_im + cim[..., None] * b_re
    eye = jnp.eye(SLAB_GROUPS, dtype=F32)

    def b_slab(t):
        t = t.reshape(depth, N_SLABS, SLAB_GROUPS, SSM_N, SSM_GROUP)
        return jnp.einsum('lmgnc,gh->lmgchn', t, eye).reshape(depth, N_SLABS, SLAB_CH, SLAB_ST)

    def c_slab(t):
        t = t.reshape(depth, N_SLABS, SLAB_GROUPS, SSM_GROUP, SSM_N)
        return jnp.einsum('lmgcn,gh->lmgnhc', t, eye).reshape(depth, N_SLABS, SLAB_ST, SLAB_CH)

    b_mat = jnp.concatenate([b_slab(bbr), b_slab(bbi)], axis=3)
    c_mat = jnp.concatenate([c_slab(c_re), c_slab(-c_im)], axis=2)

    def powers(ks):
        k = jnp.asarray(ks, F32)[:, None, None, None]
        mk = jnp.exp(k * xr[None])
        both = jnp.stack([mk * jnp.cos(k * xi[None]), mk * jnp.sin(k * xi[None])], axis=0)
        return both.reshape(2, len(ks), depth, N_SLABS, SLAB_ST).transpose(2, 3, 0, 1, 4)

    pw = powers(range(1, seg + 1))
    lam = powers([1])
    rep = lambda t: jnp.repeat(t, SUBLANES, axis=3)
    return dict(b_f32=b_mat, c_f32=c_mat, b_bf=b_mat.astype(BF16), c_bf=c_mat.astype(BF16),
                lam=lam[:, :, :, 0], lam_rows=rep(lam), lam_seg=powers([seg])[:, :, :, 0], pw_rows=rep(pw))


def _state_to_slab(s):
    b = s.shape[0]
    return s.reshape(b, N_SLABS, SLAB_GROUPS, SSM_N, 2).transpose(0, 1, 4, 2, 3).reshape(b, STATE_W)


def _slab_to_state(x):
    b = x.shape[0]
    return x.reshape(b, N_SLABS, 2, SLAB_GROUPS, SSM_N).transpose(0, 1, 3, 4, 2).reshape(b, SSM_GROUPS, SSM_N, 2)


def _ln_silu(y, g, b):
    mu = jnp.mean(y, axis=-1, keepdims=True)
    var = jnp.mean(jnp.square(y - mu), axis=-1, keepdims=True)
    t = (y - mu) * lax.rsqrt(var + EPS) * g + b
    return t * _sigmoid(t)


def _conv_tile(zc_ref, w_ref, b_ref, lg_ref, lb_ref, tail_ref, abuf_ref, shift_ref, y_ref, between=()):
    rows = zc_ref.shape[0]

    @pl.when(pl.program_id(1) == 0)
    def _():
        abuf_ref[0:CONV_HALO, :] = jnp.zeros((CONV_HALO, CONV_WIDTH), F32)

    z = zc_ref[...]
    a = z[:, 0:CONV_WIDTH] * _sigmoid(z[:, CONV_WIDTH:2 * CONV_WIDTH])
    abuf_ref[CONV_HALO:CONV_HALO + rows, :] = a
    span = shift_ref.shape[1]
    for c in range(1, SUBLANES):
        shift_ref[c - 1] = abuf_ref[c:c + span, :]
    first = CONV_HALO - (CONV_K - 1)
    blk = CONV_ROW_BLOCK
    n_blk = rows // blk
    pending = list(between)
    every = -(-n_blk // (len(pending) + 1)) if pending else n_blk
    for r in range(n_blk):
        if pending and r % every == 0:
            pending.pop(0)()
        acc = jnp.zeros((blk, CONV_WIDTH), F32) + b_ref[...]
        for j in range(CONV_K):
            c = (first + j) % SUBLANES
            lo = first + j - c + r * blk
            src = abuf_ref[lo:lo + blk, :] if c == 0 else shift_ref[c - 1, lo:lo + blk, :]
            acc = acc + jnp.concatenate([w_ref[j]] * (blk // SUBLANES), axis=0) * src
        y_ref[r * blk:(r + 1) * blk, :] = _ln_silu(acc, lg_ref[...], lb_ref[...]).astype(y_ref.dtype)
    for thunk in pending:
        thunk()
    last = a[rows - CONV_HALO:rows]
    tail_ref[...] = last
    abuf_ref[0:CONV_HALO, :] = last


def _conv_sample_body(zc_ref, hist_ref, w_ref, b_ref, lg_ref, lb_ref, y_ref, a_ref):
    z = zc_ref[...]
    a = z[:, 0:CONV_WIDTH] * _sigmoid(z[:, CONV_WIDTH:2 * CONV_WIDTH])
    a_ref[...] = a
    acc = b_ref[...] + w_ref[CONV_K - 1] * a
    for j in range(CONV_K - 1):
        acc = acc + w_ref[j] * hist_ref[j]
    y_ref[...] = _ln_silu(acc, lg_ref[...], lb_ref[...])


def _conv_sample(zc, hist_t, l, conv_w4, conv_b, ln_g, ln_b):
    n = zc.shape[0]
    full = lambda *shape: pl.BlockSpec(shape, lambda i: (0,) * len(shape))
    lsel = lambda i: (l, 0, 0)
    return pl.pallas_call(
        _conv_sample_body,
        grid=(1,),
        in_specs=[
            full(n, 2 * CONV_WIDTH),
            pl.BlockSpec((None, CONV_K - 1, n, CONV_WIDTH), lambda i: (l, 0, 0, 0)),
            pl.BlockSpec((None, CONV_K, 1, CONV_WIDTH), lambda i: (l, 0, 0, 0)),
            pl.BlockSpec((None, 1, CONV_WIDTH), lsel),
            pl.BlockSpec((None, 1, CONV_WIDTH), lsel),
            pl.BlockSpec((None, 1, CONV_WIDTH), lsel),
        ],
        out_specs=[full(n, CONV_WIDTH), full(n, CONV_WIDTH)],
        out_shape=[jax.ShapeDtypeStruct((n, CONV_WIDTH), F32)] * 2,
        compiler_params=_cparams(("arbitrary",), 32),
        name="conv_sample",
    )(zc, hist_t, conv_w4, conv_b, ln_g, ln_b)


def _attn_prompt_body(q_ref, kc_ref, kp_ref, vc_ref, vp_ref, oa_ref, ob_ref, la_ref, lb_ref, *, dil):
    j = pl.program_id(1)
    r = pl.program_id(2)
    blk = ATTN_BLOCK
    nblk = q_ref.shape[0] // blk
    q_all = q_ref[...]
    k_all = jnp.concatenate([kp_ref[...], kc_ref[...]], axis=0)
    v_all = jnp.concatenate([vp_ref[...], vc_ref[...]], axis=0)
    qi = lax.broadcasted_iota(jnp.int32, (blk, 2 * blk), 0)
    ki = lax.broadcasted_iota(jnp.int32, (blk, 2 * blk), 1)
    dist = qi + blk - ki
    band = jnp.where(dist >= 0, jnp.where(dist <= blk, 1, 0), 0)
    first_ok = jnp.where(ki >= blk, 1, jnp.where(j > 0, 1, 0))
    lane = lax.broadcasted_iota(jnp.int32, (1, LANES), 1)
    lo = lane < HEAD_DIM
    for s_blk in range(nblk):
        q = q_all[s_blk * blk:(s_blk + 1) * blk]
        k = k_all[s_blk * blk:(s_blk + 2) * blk]
        v = v_all[s_blk * blk:(s_blk + 2) * blk]
        valid = ((band * first_ok) if s_blk == 0 else band) > 0
        o_parts, lse_parts = [], []
        for hp in range(HEADS_PER_GROUP // 2):
            sl = slice(hp * LANES, (hp + 1) * LANES)
            q2, k2, v2 = q[:, sl], k[:, sl], v[:, sl]
            res = []
            for half in range(2):
                keep = lo if half == 0 else jnp.logical_not(lo)
                qm = jnp.where(keep, q2, jnp.zeros_like(q2))
                s = lax.dot_general(qm, k2, (((1,), (1,)), ((), ())), preferred_element_type=F32)
                s = jnp.where(valid, s, NEG_BIG)
                m = jnp.max(s, axis=-1, keepdims=True)
                p = jnp.exp(s - m)
                den = jnp.sum(p, axis=-1, keepdims=True)
                o = jnp.dot(p.astype(BF16), v2, preferred_element_type=F32)
                res.append((o / den, m + jnp.log(den)))
            o_parts.append(jnp.where(lo, res[0][0], res[1][0]))
            lse_parts.append(jnp.where(lo, res[0][1], res[1][1]))
        for val, ref in zip(o_parts + lse_parts, (oa_ref, ob_ref, la_ref, lb_ref)):
            if dil == 1:
                ref[s_blk * blk:(s_blk + 1) * blk, :] = val
            else:
                ref[pl.ds(r + s_blk * blk * dil, blk, stride=dil), :] = val


def _attn_prompt(q, k, v, dil):
    b_sz, _, m_len, gw = q.shape
    blk = ATTN_BLOCK
    seq = m_len * dil
    nblk = ATTN_STEP_BLOCKS[dil]
    cur = lambda b, j, r: (b, r, j, 0)
    prev = lambda b, j, r: (b, r, jnp.maximum(j * nblk - 1, 0), 0)
    bs = lambda f: pl.BlockSpec((None, None, nblk * blk, gw), f)
    bp = lambda f: pl.BlockSpec((None, None, blk, gw), f)
    span = nblk * blk * dil
    return pl.pallas_call(
        functools.partial(_attn_prompt_body, dil=dil),
        grid=(b_sz, m_len // (nblk * blk), dil),
        in_specs=[bs(cur), bs(cur), bp(prev), bs(cur), bp(prev)],
        out_specs=[pl.BlockSpec((None, span, LANES), lambda b, j, r: (b, j, 0))] * 4,
        out_shape=[jax.ShapeDtypeStruct((b_sz, seq, LANES), F32)] * 4,
        compiler_params=_cparams(("arbitrary", "arbitrary", "arbitrary"), 48),
        name=f"attn_prompt_d{dil}",
    )(q, k, k, v, v)


def _attn_roll_body(*refs):
    ng = len(ATTN_GROUPS)
    q_ref, kn_ref, vn_ref = refs[:3]
    c_refs = refs[3:3 + ng]
    y_ref = refs[-1 - ng]
    o_refs = refs[-ng:]
    scale = HEAD_DIM ** -0.5
    outs = [[None] * ng for _ in range(HEADS_PER_GROUP)]
    lses = [[None] * ng for _ in range(HEADS_PER_GROUP)]

    def columns(x_ref):
        x = jnp.concatenate([x_ref[...], jnp.zeros((LANES - SUBLANES, LANES), F32)], axis=0)
        return x.T

    def head_col(t, head):
        r, half = head // 2, head % 2
        return t[half * HEAD_DIM:(half + 1) * HEAD_DIM, r:r + 1]

    q_t, kn_t, vn_t = columns(q_ref), columns(kn_ref), columns(vn_ref)
    pairs = [(g, h) for g in range(ng) for h in range(HEADS_PER_GROUP)]
    col = lambda t, g, h: head_col(t, g * HEADS_PER_GROUP + h)
    lanes = [lax.broadcasted_iota(jnp.int32, (1, c.shape[-1]), 1) for c in c_refs]
    for g, h in pairs:
        c_ref, o_ref = c_refs[g], o_refs[g]
        buf_len = c_ref.shape[-1]
        last = lanes[g] == buf_len - 1
        o_ref[0, h] = jnp.where(last, col(kn_t, g, h), pltpu.roll(c_ref[0, h], buf_len - 1, 1))
        o_ref[1, h] = jnp.where(last, col(vn_t, g, h), pltpu.roll(c_ref[1, h], buf_len - 1, 1))
    s_all, sn_all = {}, {}
    for g, h in pairs:
        qc = col(q_t, g, h)
        used = (lanes[g] % ATTN_GROUPS[g][1]) == 0
        s_all[g, h] = jnp.where(used, jnp.sum(c_refs[g][0, h] * qc, axis=0, keepdims=True) * scale, NEG_BIG)
        sn_all[g, h] = jnp.sum(col(kn_t, g, h) * qc, axis=0, keepdims=True) * scale
    m_all = {k: jnp.maximum(jnp.max(s_all[k], axis=1, keepdims=True), sn_all[k]) for k in pairs}
    p_all = {k: jnp.exp(s_all[k] - m_all[k]) for k in pairs}
    pn_all = {k: jnp.exp(sn_all[k] - m_all[k]) for k in pairs}
    den_all = {k: jnp.sum(p_all[k], axis=1, keepdims=True) + pn_all[k] for k in pairs}
    for g, h in pairs:
        k = (g, h)
        pv = jnp.sum(c_refs[g][1, h] * p_all[k], axis=1, keepdims=True)
        outs[h][g] = (pv + pn_all[k] * col(vn_t, g, h)) / den_all[k]
        lses[h][g] = m_all[k] + jnp.log(den_all[k])
    ys = []
    for h in range(HEADS_PER_GROUP):
        top = functools.reduce(jnp.maximum, lses[h])
        ws = [jnp.exp(t - top) for t in lses[h]]
        ys.append(sum(w * o for w, o in zip(ws, outs[h])) / sum(ws))
    lane = lax.broadcasted_iota(jnp.int32, (1, LANES), 1)
    square = jnp.zeros((LANES, LANES), F32)
    for r in range(HEADS_PER_GROUP // 2):
        square = jnp.where(lane == r, jnp.concatenate([ys[2 * r], ys[2 * r + 1]], axis=0), square)
    y_ref[...] = square.T[0:SUBLANES]


def _attn_roll(q, kn, vn, caches_t, prev, l):
    n = q.shape[0]
    ng = len(ATTN_GROUPS)
    small = pl.BlockSpec((None, SUBLANES, LANES), lambda b: (b, 0, 0))
    c_specs = []
    for (win, dil), c in zip(ATTN_GROUPS, caches_t):
        assert c.shape[-1] == win and win % dil == 0
        c_specs.append(pl.BlockSpec((None, None) + c.shape[2:], lambda b: (l, b, 0, 0, 0, 0)))
    in_specs = [small, small, small] + c_specs
    args = [q, kn, vn, *caches_t]
    aliases = {}
    if prev is not None:
        in_specs += [pl.BlockSpec(memory_space=pl.ANY)] * ng
        aliases = {len(args) + g: 1 + g for g in range(ng)}
        args += list(prev)
    return pl.pallas_call(
        _attn_roll_body,
        grid=(n,),
        in_specs=in_specs,
        out_specs=[small] + c_specs,
        out_shape=[jax.ShapeDtypeStruct((n, SUBLANES, LANES), F32)]
        + [jax.ShapeDtypeStruct(c.shape, c.dtype) for c in caches_t],
        input_output_aliases=aliases,
        compiler_params=_cparams(("arbitrary",), 48),
        name="attn_roll_sample",
    )(*args)


def _merge_body(*refs, sample):
    if sample:
        (x_ref, mod_ref, g_ref, wg_ref, bg_ref, ys_ref, yc_ref, ya_ref,
         ws_ref, wc_ref, wa_ref, wo_ref, o_ref) = refs
    else:
        (x_ref, mod_ref, g_ref, wg_ref, bg_ref, ys_ref, zc_ref, cw_ref, cb_ref, lg_ref, lb_ref,
         *attn_refs, ws_ref, wc_ref, wa_ref, wo_ref, o_ref, tail_ref,
         abuf_ref, shift_ref, ycs_ref, hm_ref, gate_ref) = refs
    precise = sample
    d = D_MODEL
    x = x_ref[...]
    h = _norm_mod(x, g_ref[...], mod_ref[1], mod_ref[0])
    hm = h if precise else h.astype(BF16)

    def gate_logits(bi):
        return _mm(hm, wg_ref[:, bi * d:(bi + 1) * d], precise) + bg_ref[:, bi * d:(bi + 1) * d]

    if sample:
        ya = ya_ref[...]
        yc = yc_ref[...]
        gates = [gate_logits(bi) for bi in range(3)]
    else:
        hm_ref[...] = hm

        def stage_gate(bi):
            def thunk():
                gate_ref[bi] = jnp.dot(hm_ref[...], wg_ref[:, bi * d:(bi + 1) * d],
                                       preferred_element_type=F32)
            return thunk

        _conv_tile(zc_ref, cw_ref, cb_ref, lg_ref, lb_ref, tail_ref, abuf_ref, shift_ref, ycs_ref,
                   between=[stage_gate(bi) for bi in range(3)])
        yc = ycs_ref[...]
        gates = [gate_ref[bi] + bg_ref[:, bi * d:(bi + 1) * d] for bi in range(3)]
        halves = []
        for hp in range(2):
            o0, l0, o1, l1, o2, l2 = [attn_refs[4 * g + s_ + hp][...] for g in range(3) for s_ in (0, 2)]
            top = jnp.maximum(jnp.maximum(l0, l1), l2)
            e0, e1, e2 = jnp.exp(l0 - top), jnp.exp(l1 - top), jnp.exp(l2 - top)
            halves.append((e0 * o0 + e1 * o1 + e2 * o2) / (e0 + e1 + e2))
        ya = jnp.concatenate(halves, axis=1)
    merged = None
    for bi, (y, w_ref) in enumerate(((ys_ref[...], ws_ref), (yc, wc_ref), (ya, wa_ref))):
        part = _sigmoid(gates[bi]) * _mm(y, w_ref[...], precise)
        merged = part if merged is None else merged + part
    o_ref[...] = x + mod_ref[2] * _mm(merged, wo_ref[...], precise)


def _merge_prompt(x, modp, l, gnorm, w_gate, b_gate, y_ssm, zc, conv_w, conv_b, ln_g, ln_b, attn,
                  w_bs, w_bc, w_ba, w_out):
    b_sz, seq, d = x.shape
    tm = TM_MERGE
    row = lambda w: pl.BlockSpec((None, tm, w), lambda b, i: (b, i, 0))
    lsel = lambda b, i: (l, 0, 0)
    wsp = lambda r, c: pl.BlockSpec((None, r, c), lsel)
    wsp1 = lambda r, c: pl.BlockSpec((None, r, c), lsel, pipeline_mode=pl.Buffered(1))
    in_specs = [
        row(d),
        pl.BlockSpec((None, None, 6, 1, d), lambda b, i: (l, b, 0, 0, 0)),
        wsp(1, d), wsp1(d, 3 * d), wsp(1, 3 * d),
        row(SSM_WIDTH), row(2 * CONV_WIDTH),
        pl.BlockSpec((None, CONV_K, SUBLANES, CONV_WIDTH), lambda b, i: (l, 0, 0, 0)),
        wsp(1, CONV_WIDTH), wsp(1, CONV_WIDTH), wsp(1, CONV_WIDTH),
    ] + [row(LANES)] * 12 + [wsp1(SSM_WIDTH, d), wsp1(CONV_WIDTH, d), wsp1(ATTN_OUT, d), wsp1(d, d)]
    flat = [t for group in attn for t in group]
    return pl.pallas_call(
        functools.partial(_merge_body, sample=False),
        grid=(b_sz, seq // tm),
        in_specs=in_specs,
        out_specs=[row(d), pl.BlockSpec((None, CONV_HALO, CONV_WIDTH), lambda b, i: (b, 0, 0))],
        out_shape=[jax.ShapeDtypeStruct((b_sz, seq, d), F32),
                   jax.ShapeDtypeStruct((b_sz, CONV_HALO, CONV_WIDTH), F32)],
        scratch_shapes=[pltpu.VMEM((tm + CONV_HALO, CONV_WIDTH), F32),
                        pltpu.VMEM((SUBLANES - 1, tm + CONV_HALO - SUBLANES, CONV_WIDTH), F32),
                        pltpu.VMEM((tm, CONV_WIDTH), BF16),
                        pltpu.VMEM((tm, d), BF16),
                        pltpu.VMEM((3, tm, d), F32)],
        compiler_params=_cparams(("arbitrary", "arbitrary"), 56),
        name="merge_prompt",
    )(x, modp, gnorm, w_gate, b_gate, y_ssm, zc, conv_w, conv_b, ln_g, ln_b, *flat, w_bs, w_bc, w_ba, w_out)


def _merge_sample(x, mods, l, gnorm, w_gate, b_gate, y_ssm, y_conv, y_attn, w_bs, w_bc, w_ba, w_out):
    n, d = x.shape
    full = lambda *shape: pl.BlockSpec(shape, lambda i: (0,) * len(shape))
    lsel = lambda i: (l, 0, 0)
    wsp = lambda r, c: pl.BlockSpec((None, r, c), lsel)
    in_specs = [
        full(n, d),
        pl.BlockSpec((None, 6, n, d), lambda i: (l, 0, 0, 0)),
        wsp(1, d), wsp(d, 3 * d), wsp(1, 3 * d),
        full(n, SSM_WIDTH), full(n, CONV_WIDTH), full(n, ATTN_OUT),
        wsp(SSM_WIDTH, d), wsp(CONV_WIDTH, d), wsp(ATTN_OUT, d), wsp(d, d),
    ]
    return pl.pallas_call(
        functools.partial(_merge_body, sample=True),
        grid=(1,),
        in_specs=in_specs,
        out_specs=full(n, d),
        out_shape=jax.ShapeDtypeStruct((n, d), F32),
        compiler_params=_cparams(("arbitrary",), 56),
        name="merge_sample",
    )(x, mods, gnorm, w_gate, b_gate, y_ssm, y_conv, y_attn, w_bs, w_bc, w_ba, w_out)


def _shift_rows(x, hist, k):
    r = pltpu.roll(x, k, 0)
    row = lax.broadcasted_iota(jnp.int32, hist.shape, 0)
    head = jnp.where(row < k, pltpu.roll(hist, k, 0), r[0:SUBLANES])
    return jnp.concatenate([head, r[SUBLANES:]], axis=0)


def _ffn_prompt_body(x_ref, mod_ref, g_ref, wup_ref, cw_ref, cb_ref, wdn_ref, o_ref, tail_ref,
                     hist_ref, hb_ref, up_ref, act_ref):
    i = pl.program_id(1)
    rows = x_ref.shape[0]
    f2 = 2 * FFN_HIDDEN

    @pl.when(i == 0)
    def _():
        hist_ref[...] = jnp.zeros((HIST, f2), F32)

    x = x_ref[...]
    hb_ref[...] = _norm_mod(x, g_ref[...], mod_ref[4], mod_ref[3]).astype(BF16)
    acc = jnp.zeros((rows, D_MODEL), F32)
    cw = FFN_CHUNK
    n_chunks = FFN_HIDDEN // cw

    def project(c):
        for half, base in enumerate((c * cw, FFN_HIDDEN + c * cw)):
            up_ref[c % 2, half] = jnp.dot(hb_ref[...], wup_ref[:, base:base + cw],
                                          preferred_element_type=F32)

    def contract(c, acc):
        return acc + jnp.dot(act_ref[c % 2], wdn_ref[c * cw:(c + 1) * cw, :], preferred_element_type=F32)

    project(0)
    for c in range(n_chunks):
        if c + 1 < n_chunks:
            project(c + 1)
        if c > 0:
            acc = contract(c - 1, acc)
        halves = []
        for half, base in enumerate((c * cw, FFN_HIDDEN + c * cw)):
            cols = slice(base, base + cw)
            up = up_ref[c % 2, half]
            hist = hist_ref[:, cols]
            hist_ref[:, cols] = up[rows - HIST:rows]
            halves.append(cw_ref[0:1, cols] * _shift_rows(up, hist, 2)
                          + cw_ref[1:2, cols] * _shift_rows(up, hist, 1)
                          + cw_ref[2:3, cols] * up + cb_ref[:, cols])
        act_ref[c % 2] = (_gelu_tanh(halves[0]) * halves[1]).astype(BF16)
    acc = contract(n_chunks - 1, acc)
    o_ref[...] = x_ref[...] + mod_ref[5] * acc
    tail_ref[...] = hist_ref[...]


def _ffn_prompt(x, modp, l, gnorm, w_up, conv_w, conv_b, w_down):
    b_sz, seq, d = x.shape
    tm = TM_FFN
    f2 = 2 * FFN_HIDDEN
    lsel = lambda b, i: (l, 0, 0)
    once = pl.Buffered(1)
    return pl.pallas_call(
        _ffn_prompt_body,
        grid=(b_sz, seq // tm),
        in_specs=[
            pl.BlockSpec((None, tm, d), lambda b, i: (b, i, 0)),
            pl.BlockSpec((None, None, 6, 1, d), lambda b, i: (l, b, 0, 0, 0)),
            pl.BlockSpec((None, 1, d), lsel),
            pl.BlockSpec((None, d, f2), lsel, pipeline_mode=once),
            pl.BlockSpec((None, FFN_CONV_K, f2), lsel),
            pl.BlockSpec((None, 1, f2), lsel),
            pl.BlockSpec((None, FFN_HIDDEN, d), lsel, pipeline_mode=once),
        ],
        out_specs=[
            pl.BlockSpec((None, tm, d), lambda b, i: (b, i, 0)),
            pl.BlockSpec((None, HIST, f2), lambda b, i: (b, 0, 0)),
        ],
        out_shape=[jax.ShapeDtypeStruct((b_sz, seq, d), F32),
                   jax.ShapeDtypeStruct((b_sz, HIST, f2), F32)],
        scratch_shapes=[pltpu.VMEM((HIST, f2), F32),
                        pltpu.VMEM((tm, d), BF16),
                        pltpu.VMEM((2, 2, tm, FFN_CHUNK), F32),
                        pltpu.VMEM((2, tm, FFN_CHUNK), BF16)],
        compiler_params=_cparams(("arbitrary", "arbitrary"), 56),
        name="ffn_prompt",
    )(x, modp, gnorm, w_up, conv_w, conv_b, w_down)


def _ffn_sample_body(x_ref, mod_ref, g_ref, wup_ref, c0_ref, c1_ref, cw_ref, cb_ref, wdn_ref, o_ref, up_ref):
    x = x_ref[...]
    h2 = _norm_mod(x, g_ref[...], mod_ref[4], mod_ref[3])
    up = _mm(h2, wup_ref[...], True)
    up_ref[...] = up
    cv = cw_ref[0:1, :] * c0_ref[...] + cw_ref[1:2, :] * c1_ref[...] + cw_ref[2:3, :] * up + cb_ref[...]
    act = _gelu_tanh(cv[:, 0:FFN_HIDDEN]) * cv[:, FFN_HIDDEN:2 * FFN_HIDDEN]
    o_ref[...] = x + mod_ref[5] * _mm(act, wdn_ref[...], True)


def _ffn_sample(x, mods, l, gnorm, w_up, c0, c1, conv_w, conv_b, w_down):
    n, d = x.shape
    f2 = 2 * FFN_HIDDEN
    full = lambda *shape: pl.BlockSpec(shape, lambda i: (0,) * len(shape))
    lsel = lambda i: (l, 0, 0)
    once = pl.Buffered(1)
    return pl.pallas_call(
        _ffn_sample_body,
        grid=(1,),
        in_specs=[
            full(n, d),
            pl.BlockSpec((None, 6, n, d), lambda i: (l, 0, 0, 0)),
            pl.BlockSpec((None, 1, d), lsel),
            pl.BlockSpec((None, d, f2), lsel, pipeline_mode=once),
            pl.BlockSpec((None, n, f2), lsel), pl.BlockSpec((None, n, f2), lsel),
            pl.BlockSpec((None, FFN_CONV_K, f2), lsel),
            pl.BlockSpec((None, 1, f2), lsel),
            pl.BlockSpec((None, FFN_HIDDEN, d), lsel, pipeline_mode=once),
        ],
        out_specs=[full(n, d), full(n, f2)],
        out_shape=[jax.ShapeDtypeStruct((n, d), F32), jax.ShapeDtypeStruct((n, f2), F32)],
        compiler_params=_cparams(("arbitrary",), 56),
        name="ffn_sample",
    )(x, mods, gnorm, w_up, c0, c1, conv_w, conv_b, w_down)


def _rope_tables(pos):
    half = HEAD_DIM // 2
    inv = ROPE_THETA ** (-jnp.arange(half, dtype=F32) / half)
    ang = pos.astype(F32)[:, None] * inv[None, :]
    cos, sin = jnp.cos(ang), jnp.sin(ang)
    cos_h = jnp.concatenate([cos, cos], axis=1)
    sin_h = jnp.concatenate([-sin, sin], axis=1)
    reps = LANES // HEAD_DIM
    return jnp.tile(cos_h, (1, reps)), jnp.tile(sin_h, (1, reps))


def _residue_perm(rows, dil):
    i = jnp.arange(rows)
    src = (i % (rows // dil)) * dil + i // (rows // dil)
    return (src[:, None] == jnp.arange(rows)[None, :]).astype(BF16)


def _block_ones(dtype):
    i = jnp.arange(MXU_DIM) // HEAD_DIM
    return (i[:, None] == i[None, :]).astype(dtype)


def kernel(x_prompt, x_sample, state_ssm, cache_conv, cache_kv_w128, cache_kv_w512, cache_kv_w2048, cache_ffn, c_prompt, c_sample, w_ada, b_ada, g_norm_mix, w_in, ssm_a_re, ssm_a_im, ssm_log_dt, ssm_b_re, ssm_b_im, ssm_c_re, ssm_c_im, ssm_d, ssm_w_glu, ssm_b_glu, conv_w, conv_b, conv_ln_g, conv_ln_b, attn_gq, attn_gk, w_gate, b_gate, w_br_ssm, w_br_conv, w_br_attn, w_out, g_norm_ffn, ffn_w_up, ffn_conv_w, ffn_conv_b, ffn_w_down):
    bp, seq, d = x_prompt.shape
    ns = x_sample.shape[0]
    depth = w_ada.shape[0]
    f2 = 2 * FFN_HIDDEN
    assert x_sample.shape[1] == 1
    assert all(seq % (ATTN_BLOCK * dil * ATTN_STEP_BLOCKS[dil]) == 0 for _, dil in ATTN_GROUPS)
    caches = (cache_kv_w128, cache_kv_w512, cache_kv_w2048)

    pad = (-(bp + ns)) % SUBLANES
    c_all = jnp.concatenate([c_prompt, c_sample, jnp.zeros((pad, d), F32)], axis=0)
    mod = _ada_call(c_all, w_ada, b_ada)
    modp = mod[:, :bp].reshape(depth, bp, 6, 1, d)
    mods = mod[:, bp:bp + ns].reshape(depth, ns, 6, d).transpose(0, 2, 1, 3)

    bf = lambda w: w.astype(BF16)
    w_in_b, w_gate_b, w_out_b = bf(w_in), bf(w_gate), bf(w_out)
    w_bs_b, w_bc_b, w_ba_b = bf(w_br_ssm), bf(w_br_conv), bf(w_br_attn)
    w_up_b, w_dn_b, w_glu_b = bf(ffn_w_up), bf(ffn_w_down), bf(ssm_w_glu)

    row3 = lambda t: t.reshape(depth, 1, t.shape[-1])
    g_mix, g_ffn = row3(g_norm_mix), row3(g_norm_ffn)
    gq = row3(jnp.tile(attn_gq, (1, ATTN_HEADS)))
    gk = row3(jnp.tile(attn_gk, (1, ATTN_HEADS)))
    b_gate3, ssm_d3, b_glu3 = row3(b_gate), row3(ssm_d), row3(ssm_b_glu)
    conv_b3, ln_g3, ln_b3, ffn_cb3 = row3(conv_b), row3(conv_ln_g), row3(conv_ln_b), row3(ffn_conv_b)
    conv_w4 = conv_w.reshape(depth, CONV_K, 1, CONV_WIDTH)
    conv_w8 = jnp.repeat(conv_w4, SUBLANES, axis=2)

    cos_p, sin_p = _rope_tables(jnp.arange(seq, dtype=jnp.int32))
    cos_s, sin_s = _rope_tables(jnp.full((ns,), PAST_LEN, dtype=jnp.int32))
    ones_b, ones_f = _block_ones(BF16), _block_ones(F32)
    p4 = _residue_perm(TM_INPROJ, ATTN_GROUPS[1][1])
    p16 = _residue_perm(TM_INPROJ, ATTN_GROUPS[2][1])
    seg = L_SSM // SUBLANES
    pm = _residue_perm(L_SSM, seg)
    pmt = pm.T
    zero_state = jnp.zeros((bp, 1, STATE_W), F32)

    tabs = _ssm_tables(ssm_a_re, ssm_a_im, ssm_log_dt, ssm_b_re, ssm_b_im, ssm_c_re, ssm_c_im, seg)
    state_slab = _state_to_slab(state_ssm.reshape((depth * ns,) + state_ssm.shape[2:])).reshape(depth, ns, STATE_W)
    conv_hist = cache_conv.transpose(0, 2, 1, 3)
    ffn_old, ffn_new = cache_ffn[:, :, 0], cache_ffn[:, :, 1]
    caches_t = [c.transpose(0, 1, 3, 4, 5, 2) for c in caches]

    yp = x_prompt
    ys = x_sample.reshape(ns, d)
    st_p, st_s, a_tails, a_news, up_tails, up_news, k_tails, v_tails = [], [], [], [], [], [], [], []
    rolled = None
    for l in range(depth):
        (zs, zc, k_tail, v_tail, q0, k0, v0, q1, k1, v1, q2, k2, v2) = _inproj_prompt(
            yp, modp, l, g_mix, w_in_b, gq, gk, cos_p, sin_p, ones_b, p4, p16)
        y_ssm, st = _ssm_prompt(zs, zero_state, l, pm, pmt, tabs, ssm_d3, w_glu_b, b_glu3)
        attn = [_attn_prompt(q0, k0, v0, 1), _attn_prompt(q1, k1, v1, ATTN_GROUPS[1][1]),
                _attn_prompt(q2, k2, v2, ATTN_GROUPS[2][1])]
        x_mid, a_tail = _merge_prompt(yp, modp, l, g_mix, w_gate_b, b_gate3, y_ssm, zc, conv_w8, conv_b3,
                                      ln_g3, ln_b3, attn, w_bs_b, w_bc_b, w_ba_b, w_out_b)
        yp, up_tail = _ffn_prompt(x_mid, modp, l, g_ffn, w_up_b, ffn_conv_w, ffn_cb3, w_dn_b)
        st_p.append(st)
        a_tails.append(a_tail)
        up_tails.append(up_tail)
        k_tails.append(k_tail)
        v_tails.append(v_tail)

        zs_s, zc_s, q_s, k_s, v_s = _inproj_sample(ys, mods, l, g_mix, w_in, gq, gk, cos_s, sin_s, ones_f)
        y_ssm_s, st_l = _ssm_sample(zs_s, state_slab, l, tabs, ssm_d3, ssm_w_glu, b_glu3)
        y_conv_s, a_s = _conv_sample(zc_s, conv_hist, l, conv_w4, conv_b3, ln_g3, ln_b3)
        rows8 = lambda t: jnp.pad(t.reshape(ns, ATTN_WIDTH // LANES, LANES),
                                  ((0, 0), (0, SUBLANES - ATTN_WIDTH // LANES), (0, 0)))
        y_attn_s, *rolled = _attn_roll(rows8(q_s), rows8(k_s), rows8(v_s), caches_t, rolled, l)
        y_attn_s = y_attn_s[:, :ATTN_OUT // LANES].reshape(ns, ATTN_OUT)
        x_mid_s = _merge_sample(ys, mods, l, g_mix, w_gate, b_gate3, y_ssm_s, y_conv_s, y_attn_s,
                                w_br_ssm, w_br_conv, w_br_attn, w_out)
        ys, up_s = _ffn_sample(x_mid_s, mods, l, g_ffn, ffn_w_up, ffn_old, ffn_new, ffn_conv_w, ffn_cb3,
                               ffn_w_down)
        st_s.append(st_l)
        a_news.append(a_s)
        up_news.append(up_s)

    stk = lambda t: jnp.stack(t, axis=0)
    unslab = lambda t, n: _slab_to_state(t.reshape(depth * n, STATE_W)).reshape(depth, n, SSM_GROUPS, SSM_N, 2)
    ssm_p = unslab(stk(st_p), bp)
    ssm_s = unslab(stk(st_s), ns)
    conv_p = stk(a_tails)[:, :, CONV_HALO - (CONV_K - 1):]
    conv_s = jnp.concatenate([cache_conv[:, :, 1:], stk(a_news)[:, :, None, :]], axis=2)
    ffn_p = stk(up_tails)[:, :, HIST - (FFN_CONV_K - 1):]
    ffn_s = jnp.stack([ffn_new, stk(up_news)], axis=2)
    kt, vt = stk(k_tails), stk(v_tails)
    gw = HEADS_PER_GROUP * HEAD_DIM
    kv_p = []
    for gi, (win, _) in enumerate(ATTN_GROUPS):
        keep = min(win, seq)
        pick = lambda t: t[:, :, gi * gw:(gi + 1) * gw, t.shape[-1] - keep:].reshape(
            depth, bp, HEADS_PER_GROUP, HEAD_DIM, keep)
        kv_p.append(jnp.stack([pick(kt), pick(vt)], axis=2).transpose(0, 1, 5, 2, 3, 4))
    kv_s = [t.transpose(0, 1, 5, 2, 3, 4) for t in rolled]
    return (yp, ys.reshape(ns, 1, d), ssm_p, ssm_s, conv_p, conv_s,
            kv_p[0], kv_s[0], kv_p[1], kv_s[1], kv_p[2], kv_s[2], ffn_p, ffn_s)
```
